```python
import math
import jax
import jax.numpy as jnp
from jax import lax
import numpy as np

D_MODEL = 2048
BATCH = 2
SEQ = 4096
DEPTH = 1
DEC_BATCH = 32
DEC_SEQ = 8
PAST_LEN = 8192
PAGE_SIZE = 128

HEAD_DIM = 128
ATTN_WIDTH = D_MODEL // 2
N_HEADS = ATTN_WIDTH // HEAD_DIM
LRU_WIDTH = D_MODEL - ATTN_WIDTH
LRU_BLOCKS = 8
LRU_BLOCK_W = LRU_WIDTH // LRU_BLOCKS
LRU_CONV_W = 4
LRU_C = 8.0
D_FF = 3 * D_MODEL
FFN_CONV_W = 3
Q_BLOCK = 128
N_MOD = 6
EPS = 1e-6
SB_SCALE = 1.0 / math.sqrt(HEAD_DIM)
SB_BIAS_INIT = -5.0
N_IN = 3 * ATTN_WIDTH + 2 * LRU_WIDTH

kernel_name = 'hymba_stickbreak_rglru_convffn_step'


def _rmsnorm(x, g):
    xf = x.astype(jnp.float32)
    y = xf * lax.rsqrt(jnp.mean(xf * xf, axis=-1, keepdims=True) + EPS)
    return y.astype(x.dtype) * g


def _causal_dwconv(x, prev, w, b):
    width = w.shape[0]
    t = x.shape[1]
    xp = jnp.concatenate([prev.astype(x.dtype), x], axis=1)
    y = b + w[0] * xp[:, 0:t]
    for i in range(1, width):
        y = y + w[i] * xp[:, i:i + t]
    return y, xp[:, -(width - 1):]


def _rglru(xc, h0, w_a, b_a, w_x, b_x, lam, reset_first):
    bsz, t, _ = xc.shape
    xf = xc.astype(jnp.float32)
    xb = xf.reshape(bsz, t, LRU_BLOCKS, LRU_BLOCK_W)
    r = jax.nn.sigmoid(jnp.einsum('btnc,ncd->btnd', xb, w_a.astype(jnp.float32)) + b_a).reshape(bsz, t, LRU_WIDTH)
    i = jax.nn.sigmoid(jnp.einsum('btnc,ncd->btnd', xb, w_x.astype(jnp.float32)) + b_x).reshape(bsz, t, LRU_WIDTH)
    log_a = -LRU_C * r * jax.nn.softplus(-lam.astype(jnp.float32))
    a = jnp.exp(log_a)
    mult = jnp.sqrt(-jnp.expm1(2.0 * log_a))
    if reset_first:
        mult = mult.at[:, 0].set(1.0)
    b = mult * i * xf
    b = b.at[:, 0].add(a[:, 0] * h0.astype(jnp.float32))

    def combine(left, right):
        a1, b1 = left
        a2, b2 = right
        return a1 * a2, a2 * b1 + b2

    _, h = lax.associative_scan(combine, (a, b), axis=1)
    return h, h[:, -1]


def _sb_prompt(q, k, v, bias):
    bsz, s, h, d = q.shape
    nb = s // Q_BLOCK
    qb = q.reshape(bsz, nb, Q_BLOCK, h, d).transpose(1, 0, 2, 3, 4)
    kpos = jnp.arange(s)
    vf = v.astype(jnp.float32)
    bh = bias.astype(jnp.float32)[None, :, None, None]

    def block(args):
        qi, idx = args
        z = jnp.einsum('bqhd,bkhd->bhqk', qi, k).astype(jnp.float32) * SB_SCALE + bh
        qpos = idx * Q_BLOCK + jnp.arange(Q_BLOCK)
        mask = kpos[None, :] < qpos[:, None]
        lm = jnp.where(mask, jax.nn.log_sigmoid(-z), 0.0)
        rc = lax.cumsum(lm, axis=3, reverse=True)
        w = jnp.where(mask, jnp.exp(jax.nn.log_sigmoid(z) + rc - lm), 0.0)
        return jnp.einsum('bhqk,bkhd->bqhd', w, vf)

    o = lax.map(block, (qb, jnp.arange(nb)))
    return o.transpose(1, 0, 2, 3, 4).reshape(bsz, s, h * d).astype(q.dtype)


def _sb_sample(q, k, v, bias, cache_k, cache_v, page_table):
    bsz, t, h, d = q.shape
    bh = bias.astype(jnp.float32)[None, :, None, None]
    z = jnp.einsum('bqhd,bkhd->bhqk', q, k).astype(jnp.float32) * SB_SCALE + bh
    tpos = jnp.arange(t)
    mask = tpos[None, :] < tpos[:, None]
    lm = jnp.where(mask, jax.nn.log_sigmoid(-z), 0.0)
    rc = lax.cumsum(lm, axis=3, reverse=True)
    w = jnp.where(mask, jnp.exp(jax.nn.log_sigmoid(z) + rc - lm), 0.0)
    o = jnp.einsum('bhqk,bkhd->bqhd', w, v.astype(jnp.float32))
    surv = rc[..., 0]

    def step(carry, pages):
        o_acc, sv = carry
        kp = cache_k[pages]
        vp = cache_v[pages].astype(jnp.float32)
        zp = jnp.einsum('bqhd,bkhd->bhqk', q, kp).astype(jnp.float32) * SB_SCALE + bh
        lmp = jax.nn.log_sigmoid(-zp)
        rcp = lax.cumsum(lmp, axis=3, reverse=True)
        wp = jnp.exp(jax.nn.log_sigmoid(zp) + rcp - lmp + sv[..., None])
        o_acc = o_acc + jnp.einsum('bhqk,bkhd->bqhd', wp, vp)
        return (o_acc, sv + rcp[..., 0]), None

    (o, _), _ = lax.scan(step, (o, surv), page_table.T, reverse=True)
    return o.reshape(bsz, t, h * d).astype(q.dtype)


def _layer(x, c, attn_fn, lru_conv_prev, lru_h0, ffn_conv_prev, reset_first,
           w_ada, b_ada, g_norm1, w_in, g_q, g_k, b_sb, w_lru_conv, b_lru_conv,
           w_lru_a, b_lru_a, w_lru_x, b_lru_x, lru_lambda, w_out,
           g_norm2, w_up, w_ffn_conv, b_ffn_conv, w_down):
    bsz, t, _ = x.shape
    mod = (jax.nn.silu(c) @ w_ada + b_ada).reshape(bsz, N_MOD, 1, D_MODEL)
    sh1, sc1, gt1, sh2, sc2, gt2 = (mod[:, 0], mod[:, 1], mod[:, 2], mod[:, 3], mod[:, 4], mod[:, 5])

    hn = _rmsnorm(x, g_norm1) * (1.0 + sc1) + sh1
    proj = hn @ w_in
    q, k, v, xl, gl = jnp.split(proj, [ATTN_WIDTH, 2 * ATTN_WIDTH, 3 * ATTN_WIDTH, 3 * ATTN_WIDTH + LRU_WIDTH], axis=-1)
    q = _rmsnorm(q.reshape(bsz, t, N_HEADS, HEAD_DIM), g_q)
    k = _rmsnorm(k.reshape(bsz, t, N_HEADS, HEAD_DIM), g_k)
    v = v.reshape(bsz, t, N_HEADS, HEAD_DIM)
    attn = attn_fn(q, k, v, b_sb)

    xc, lru_conv_new = _causal_dwconv(xl, lru_conv_prev, w_lru_conv, b_lru_conv)
    hl, lru_h_new = _rglru(xc, lru_h0, w_lru_a, b_lru_a, w_lru_x, b_lru_x, lru_lambda, reset_first)
    lru = hl.astype(x.dtype) * jax.nn.gelu(gl)

    mix = jnp.concatenate([attn.astype(x.dtype), lru], axis=-1) @ w_out
    x = x + gt1 * mix

    hn2 = _rmsnorm(x, g_norm2) * (1.0 + sc2) + sh2
    u = hn2 @ w_up
    uc, ffn_conv_new = _causal_dwconv(u, ffn_conv_prev, w_ffn_conv, b_ffn_conv)
    ug, uv = jnp.split(uc, 2, axis=-1)
    x = x + gt2 * ((jax.nn.gelu(ug) * uv) @ w_down)
    return x, k, v, lru_conv_new, lru_h_new.astype(x.dtype), ffn_conv_new


def setup_inputs(seed: int = 0) -> dict:
    key = jax.random.key(seed)
    ks = jax.random.split(key, 32)
    n_pages = PAST_LEN // PAGE_SIZE
    n_used = DEC_BATCH * n_pages
    n_pool = n_used + max(1, n_used // 4)

    def nrm(k, shape, scale):
        return jax.random.normal(k, shape, jnp.float32) * scale

    a0 = jax.random.uniform(ks[20], (DEPTH, LRU_WIDTH), jnp.float32, minval=0.9, maxval=0.999)
    a1 = a0 ** (1.0 / LRU_C)
    return {
        'x_prompt': nrm(ks[0], (BATCH, SEQ, D_MODEL), 1.0),
        'x_sample': nrm(ks[1], (DEC_BATCH, DEC_SEQ, D_MODEL), 1.0),
        'c_prompt': nrm(ks[2], (BATCH, D_MODEL), 1.0),
        'c_sample': nrm(ks[3], (DEC_BATCH, D_MODEL), 1.0),
        'cache_k': nrm(ks[4], (DEPTH, n_pool, PAGE_SIZE, N_HEADS, HEAD_DIM), 1.0),
        'cache_v': nrm(ks[5], (DEPTH, n_pool, PAGE_SIZE, N_HEADS, HEAD_DIM), 1.0),
        'page_table': jax.random.permutation(ks[6], n_pool)[:n_used].reshape(DEC_BATCH, n_pages).astype(jnp.int32),
        'state_lru_conv': nrm(ks[7], (DEPTH, DEC_BATCH, LRU_CONV_W - 1, LRU_WIDTH), 1.0),
        'state_lru_h': nrm(ks[8], (DEPTH, DEC_BATCH, LRU_WIDTH), 0.3),
        'state_ffn_conv': nrm(ks[9], (DEPTH, DEC_BATCH, FFN_CONV_W - 1, 2 * D_FF), 1.0),
        'w_ada': nrm(ks[10], (DEPTH, D_MODEL, N_MOD * D_MODEL), 0.5 * D_MODEL ** -0.5),
        'b_ada': nrm(ks[11], (DEPTH, N_MOD * D_MODEL), 0.02),
        'g_norm1': 1.0 + nrm(ks[12], (DEPTH, D_MODEL), 0.02),
        'w_in': nrm(ks[13], (DEPTH, D_MODEL, N_IN), D_MODEL ** -0.5),
        'g_q': 1.0 + nrm(ks[14], (DEPTH, HEAD_DIM), 0.02),
        'g_k': 1.0 + nrm(ks[15], (DEPTH, HEAD_DIM), 0.02),
        'b_sb': SB_BIAS_INIT + nrm(ks[29], (DEPTH, N_HEADS), 0.1),
        'w_lru_conv': nrm(ks[16], (DEPTH, LRU_CONV_W, LRU_WIDTH), LRU_CONV_W ** -0.5),
        'b_lru_conv': nrm(ks[17], (DEPTH, LRU_WIDTH), 0.02),
        'w_lru_a': nrm(ks[18], (DEPTH, LRU_BLOCKS, LRU_BLOCK_W, LRU_BLOCK_W), LRU_BLOCK_W ** -0.5),
        'b_lru_a': nrm(ks[19], (DEPTH, LRU_BLOCKS, LRU_BLOCK_W), 0.02),
        'w_lru_x': nrm(ks[21], (DEPTH, LRU_BLOCKS, LRU_BLOCK_W, LRU_BLOCK_W), LRU_BLOCK_W ** -0.5),
        'b_lru_x': nrm(ks[22], (DEPTH, LRU_BLOCKS, LRU_BLOCK_W), 0.02),
        'lru_lambda': jnp.log(a1) - jnp.log1p(-a1),
        'w_out': nrm(ks[23], (DEPTH, D_MODEL, D_MODEL), D_MODEL ** -0.5),
        'g_norm2': 1.0 + nrm(ks[24], (DEPTH, D_MODEL), 0.02),
        'w_up': nrm(ks[25], (DEPTH, D_MODEL, 2 * D_FF), D_MODEL ** -0.5),
        'w_ffn_conv': nrm(ks[26], (DEPTH, FFN_CONV_W, 2 * D_FF), FFN_CONV_W ** -0.5),
        'b_ffn_conv': nrm(ks[27], (DEPTH, 2 * D_FF), 0.02),
        'w_down': nrm(ks[28], (DEPTH, D_FF, D_MODEL), D_FF ** -0.5),
    }


def reference(x_prompt, x_sample, c_prompt, c_sample, cache_k, cache_v, page_table,
              state_lru_conv, state_lru_h, state_ffn_conv,
              w_ada, b_ada, g_norm1, w_in, g_q, g_k, b_sb, w_lru_conv, b_lru_conv,
              w_lru_a, b_lru_a, w_lru_x, b_lru_x, lru_lambda, w_out,
              g_norm2, w_up, w_ffn_conv, b_ffn_conv, w_down):
    xp = x_prompt
    xs = x_sample
    kp_l, vp_l, ks_l, vs_l = [], [], [], []
    lcp_l, lcs_l, lhp_l, lhs_l, fcp_l, fcs_l = [], [], [], [], [], []
    for l in range(DEPTH):
        params = (w_ada[l], b_ada[l], g_norm1[l], w_in[l], g_q[l], g_k[l], b_sb[l], w_lru_conv[l], b_lru_conv[l],
                  w_lru_a[l], b_lru_a[l], w_lru_x[l], b_lru_x[l], lru_lambda[l], w_out[l],
                  g_norm2[l], w_up[l], w_ffn_conv[l], b_ffn_conv[l], w_down[l])
        zc = jnp.zeros((xp.shape[0], LRU_CONV_W - 1, LRU_WIDTH), xp.dtype)
        zh = jnp.zeros((xp.shape[0], LRU_WIDTH), xp.dtype)
        zf = jnp.zeros((xp.shape[0], FFN_CONV_W - 1, 2 * D_FF), xp.dtype)
        xp, kp, vp, lcp, lhp, fcp = _layer(xp, c_prompt, _sb_prompt, zc, zh, zf, True, *params)
        bp, sp = kp.shape[0], kp.shape[1]
        kp_l.append(kp.reshape(bp, sp // PAGE_SIZE, PAGE_SIZE, N_HEADS, HEAD_DIM))
        vp_l.append(vp.reshape(bp, sp // PAGE_SIZE, PAGE_SIZE, N_HEADS, HEAD_DIM))
        lcp_l.append(lcp)
        lhp_l.append(lhp)
        fcp_l.append(fcp)
        ck = cache_k[l]
        cv = cache_v[l]
        sample_attn = lambda q, k, v, bias: _sb_sample(q, k, v, bias, ck, cv, page_table)
        xs, ksm, vsm, lcs, lhs, fcs = _layer(xs, c_sample, sample_attn, state_lru_conv[l], state_lru_h[l],
                                             state_ffn_conv[l], False, *params)
        ks_l.append(ksm)
        vs_l.append(vsm)
        lcs_l.append(lcs)
        lhs_l.append(lhs)
        fcs_l.append(fcs)
    return (xp, xs, jnp.stack(kp_l), jnp.stack(vp_l), jnp.stack(ks_l), jnp.stack(vs_l),
            jnp.stack(lcp_l), jnp.stack(lcs_l), jnp.stack(lhp_l), jnp.stack(lhs_l),
            jnp.stack(fcp_l), jnp.stack(fcs_l))
```

```python
import functools
import math

import jax
import jax.numpy as jnp
from jax import lax
from jax.experimental import pallas as pl
from jax.experimental.pallas import tpu as pltpu

F32 = jnp.float32
BF16 = jnp.bfloat16

EPS = 1e-6
LRU_C = 8.0
HEAD_DIM = 128
LRU_BLOCKS = 8
LRU_CONV_W = 4
FFN_CONV_W = 3
N_MOD = 6
SB_SCALE = 1.0 / math.sqrt(HEAD_DIM)

SUBLANES = 8
VMEM_LIMIT = 56 * 1024 * 1024

ROW_TILE = 512
ATTN_TQ = 256
ATTN_TK = 256
ADA_TN = 1024
FFN_TF = 512
PAGES_PER_STEP = 8


def _cparams(n_axes):
    return pltpu.CompilerParams(dimension_semantics=("arbitrary",) * n_axes,
                                vmem_limit_bytes=VMEM_LIMIT)


def _softplus(z):
    return jnp.maximum(z, 0.0) + jnp.log1p(jnp.exp(-jnp.abs(z)))


def _gelu_tanh(x):
    c = math.sqrt(2.0 / math.pi)
    return x * (0.5 * (1.0 + jnp.tanh(c * (x + 0.044715 * (x * x * x)))))


def _split_hi_lo(x):
    hi = x.astype(BF16)
    lo = (x - hi.astype(F32)).astype(BF16)
    return hi, lo


def _suffix_sum_matrix(n_keys):
    j = lax.broadcasted_iota(jnp.int32, (2 * n_keys, n_keys), 0) % n_keys
    s = lax.broadcasted_iota(jnp.int32, (2 * n_keys, n_keys), 1)
    return jnp.where(j >= s, 1.0, 0.0).astype(BF16)


def _ada_kernel(c_ref, w_ref, b_ref, o_ref):
    c = c_ref[...]
    s = (c * jax.nn.sigmoid(c)).astype(BF16)
    o_ref[...] = jnp.dot(s, w_ref[...].astype(BF16), preferred_element_type=F32) + b_ref[...]


def _ada_call(c_all, w_ada, b_ada):
    m, d = c_all.shape
    n = w_ada.shape[1]
    return pl.pallas_call(
        _ada_kernel,
        grid=(n // ADA_TN,),
        in_specs=[pl.BlockSpec((m, d), lambda j: (0, 0)),
                  pl.BlockSpec((d, ADA_TN), lambda j: (0, j)),
                  pl.BlockSpec((1, ADA_TN), lambda j: (0, j))],
        out_specs=pl.BlockSpec((m, ADA_TN), lambda j: (0, j)),
        out_shape=jax.ShapeDtypeStruct((m, n), F32),
        compiler_params=_cparams(1),
        name="adaln",
    )(c_all, w_ada, b_ada.reshape(1, n))


def _head_rmsnorm(a, g, n_heads):
    outs = []
    for h in range(n_heads):
        ah = a[:, h * HEAD_DIM:(h + 1) * HEAD_DIM]
        ms = jnp.mean(ah * ah, axis=-1, keepdims=True)
        outs.append(ah * lax.rsqrt(ms + EPS) * g)
    return outs


def _inproj_kernel(x_ref, sh_ref, sc_ref, g1_ref, w_ref, gq_ref, gk_ref,
                   q_ref, kf_ref, kb_ref, vf_ref, vb_ref, xl_ref, gl_ref, hn_ref, *, n_heads):
    j = pl.program_id(1)

    @pl.when(j == 0)
    def _():
        x = x_ref[...]
        ms = jnp.mean(x * x, axis=-1, keepdims=True)
        y = x * lax.rsqrt(ms + EPS) * g1_ref[...]
        hn = y * (1.0 + sc_ref[...]) + sh_ref[...]
        hn_ref[...] = hn.reshape(hn_ref.shape).astype(BF16)

    acc = jnp.dot(hn_ref[...], w_ref[...], preferred_element_type=F32)

    @pl.when(j == 0)
    def _():
        for h, qh in enumerate(_head_rmsnorm(acc, gq_ref[...], n_heads)):
            q_ref[:, h * HEAD_DIM:(h + 1) * HEAD_DIM] = (qh * SB_SCALE).astype(BF16)

    @pl.when(j == 1)
    def _():
        for h, kh in enumerate(_head_rmsnorm(acc, gk_ref[...], n_heads)):
            kf_ref[:, h, :] = kh
            kb_ref[:, h * HEAD_DIM:(h + 1) * HEAD_DIM] = kh.astype(BF16)

    @pl.when(j == 2)
    def _():
        for h in range(n_heads):
            vf_ref[:, h, :] = acc[:, h * HEAD_DIM:(h + 1) * HEAD_DIM]
        vb_ref[...] = acc.astype(BF16)

    @pl.when(j == 3)
    def _():
        xl_ref[...] = acc

    @pl.when(j == 4)
    def _():
        gl_ref[...] = acc


def _inproj_call(x3, sh, sc, g1, w_in_bf, g_q, g_k, *, groups_per_block, mod_index):
    n_groups, rows, d = x3.shape
    g = groups_per_block
    tm = g * rows
    m = n_groups * rows
    n_blocks = n_groups // g
    width = w_in_bf.shape[1] // 5
    n_heads = width // HEAD_DIM
    row_spec = pl.BlockSpec((tm, width), lambda i, j: (i, 0))
    head_spec = pl.BlockSpec((tm, n_heads, HEAD_DIM), lambda i, j: (i, 0, 0))
    mod_spec = pl.BlockSpec((g, 1, d), lambda i, j: (mod_index(i), 0, 0))
    f32_out = jax.ShapeDtypeStruct((m, width), F32)
    head_out = jax.ShapeDtypeStruct((m, n_heads, HEAD_DIM), F32)
    bf_out = jax.ShapeDtypeStruct((m, width), BF16)
    return pl.pallas_call(
        functools.partial(_inproj_kernel, n_heads=n_heads),
        grid=(n_blocks, 5),
        in_specs=[pl.BlockSpec((g, rows, d), lambda i, j: (i, 0, 0)),
                  mod_spec, mod_spec,
                  pl.BlockSpec((1, d), lambda i, j: (0, 0)),
                  pl.BlockSpec((d, width), lambda i, j: (0, j)),
                  pl.BlockSpec((1, HEAD_DIM), lambda i, j: (0, 0)),
                  pl.BlockSpec((1, HEAD_DIM), lambda i, j: (0, 0))],
        out_specs=[row_spec, head_spec, row_spec, head_spec, row_spec, row_spec, row_spec],
        out_shape=[bf_out, head_out, bf_out, head_out, bf_out, f32_out, f32_out],
        scratch_shapes=[pltpu.VMEM((tm, d), BF16)],
        compiler_params=_cparams(2),
        name="inproj",
    )(x3, sh, sc, g1.reshape(1, d), w_in_bf, g_q.reshape(1, HEAD_DIM), g_k.reshape(1, HEAD_DIM))


def _sb_tile(z, later, suffix_mat, mask):
    sp = _softplus(z)
    if mask is not None:
        sp = jnp.where(mask, sp, 0.0)
    hi, lo = _split_hi_lo(sp)
    cums = jnp.dot(jnp.concatenate([hi, lo], axis=1), suffix_mat, preferred_element_type=F32)
    w = jnp.exp(z - cums - later)
    if mask is not None:
        w = jnp.where(mask, w, 0.0)
    return w, later + cums[:, 0:1]


def _attn_prompt_kernel(bias_ref, q_ref, k_ref, v_ref, o_ref):
    h = pl.program_id(1)
    qi = pl.program_id(2)
    bias = bias_ref[h]
    q = q_ref[...]
    suffix_mat = _suffix_sum_matrix(ATTN_TK)

    def tile(j, carry, mask):
        o, later = carry
        start = pl.multiple_of(j * ATTN_TK, ATTN_TK)
        kb = k_ref[pl.ds(start, ATTN_TK), :]
        vb = v_ref[pl.ds(start, ATTN_TK), :]
        z = lax.dot_general(q, kb, (((1,), (1,)), ((), ())), preferred_element_type=F32) + bias
        w, later = _sb_tile(z, later, suffix_mat, mask)
        o = o + jnp.dot(w.astype(BF16), vb, preferred_element_type=F32)
        return o, later

    row = lax.broadcasted_iota(jnp.int32, (ATTN_TQ, ATTN_TK), 0)
    col = lax.broadcasted_iota(jnp.int32, (ATTN_TQ, ATTN_TK), 1)
    carry = (jnp.zeros((ATTN_TQ, HEAD_DIM), F32), jnp.zeros((ATTN_TQ, 1), F32))
    carry = tile(qi, carry, col < row)
    o, _ = lax.fori_loop(0, qi, lambda t, c: tile(qi - 1 - t, c, None), carry)
    o_ref[...] = o.astype(o_ref.dtype)


def _attn_prompt_call(q, k, v, b_sb, *, batch, seq):
    m, width = q.shape
    n_heads = width // HEAD_DIM
    nq = seq // ATTN_TQ
    q_spec = pl.BlockSpec((ATTN_TQ, HEAD_DIM), lambda b, h, i: (b * nq + i, h))
    kv_spec = pl.BlockSpec((seq, HEAD_DIM), lambda b, h, i: (b, h))
    return pl.pallas_call(
        _attn_prompt_kernel,
        grid=(batch, n_heads, nq),
        in_specs=[pl.BlockSpec(memory_space=pltpu.SMEM), q_spec, kv_spec, kv_spec],
        out_specs=q_spec,
        out_shape=jax.ShapeDtypeStruct((m, width), BF16),
        compiler_params=_cparams(3),
        name="attn_prompt",
    )(b_sb, q, k, v)


def _attn_sample_kernel(pt_ref, bias_ref, q_ref, knew_ref, vnew_ref, *rest, n_heads, n_new):
    del pt_ref
    pages = PAGES_PER_STEP
    k_refs = rest[:pages]
    v_refs = rest[pages:2 * pages]
    o_ref = rest[2 * pages]
    kpad_ref, vpad_ref, acc_ref, later_ref = rest[2 * pages + 1:]
    g = pl.program_id(1)
    page_len = kpad_ref.shape[0]
    suffix_mat = _suffix_sum_matrix(page_len)
    bias = jnp.concatenate([jnp.full((n_new, 1), bias_ref[h], F32) for h in range(n_heads)], axis=0)

    def visit(k_page, v_page, mask):
        zs = []
        for h in range(n_heads):
            qh = q_ref[0, :, h * HEAD_DIM:(h + 1) * HEAD_DIM].astype(BF16)
            kh = k_page[:, h, :].astype(BF16)
            zs.append(lax.dot_general(qh, kh, (((1,), (1,)), ((), ())), preferred_element_type=F32))
        z = jnp.concatenate(zs, axis=0) + bias
        w, later = _sb_tile(z, later_ref[...], suffix_mat, mask)
        later_ref[...] = later
        for h in range(n_heads):
            wh = w[h * n_new:(h + 1) * n_new].astype(BF16)
            vh = v_page[:, h, :].astype(BF16)
            acc_ref[h] += jnp.dot(wh, vh, preferred_element_type=F32)

    @pl.when(g == 0)
    def _():
        acc_ref[...] = jnp.zeros_like(acc_ref)
        later_ref[...] = jnp.zeros_like(later_ref)
        kpad_ref[...] = jnp.zeros_like(kpad_ref)
        vpad_ref[...] = jnp.zeros_like(vpad_ref)
        kpad_ref[0:n_new] = knew_ref[0, 0]
        vpad_ref[0:n_new] = vnew_ref[0, 0]
        row = lax.broadcasted_iota(jnp.int32, (n_heads * n_new, page_len), 0) % n_new
        col = lax.broadcasted_iota(jnp.int32, (n_heads * n_new, page_len), 1)
        visit(kpad_ref, vpad_ref, col < row)

    for p in range(pages):
        visit(k_refs[p].at[0, 0], v_refs[p].at[0, 0], None)

    @pl.when(g == pl.num_programs(1) - 1)
    def _():
        for h in range(n_heads):
            o_ref[0, :, h * HEAD_DIM:(h + 1) * HEAD_DIM] = acc_ref[h].astype(o_ref.dtype)


def _attn_sample_call(q3, knew5, vnew5, cache_k, cache_v, page_table, b_sb):
    bsz, n_new, width = q3.shape
    n_heads = width // HEAD_DIM
    n_pages = page_table.shape[1]
    page_len = cache_k.shape[2]
    pages = PAGES_PER_STEP
    steps = n_pages // pages

    def page_spec(p):
        def index(b, g, pt):
            return (0, pt[b, n_pages - 1 - (g * pages + p)], 0, 0, 0)
        return pl.BlockSpec((1, 1, page_len, n_heads, HEAD_DIM), index)

    new_spec = pl.BlockSpec((1, 1, n_new, n_heads, HEAD_DIM), lambda b, g, pt: (0, b, 0, 0, 0))
    q_spec = pl.BlockSpec((1, n_new, width), lambda b, g, pt: (b, 0, 0))
    grid_spec = pltpu.PrefetchScalarGridSpec(
        num_scalar_prefetch=1,
        grid=(bsz, steps),
        in_specs=([pl.BlockSpec(memory_space=pltpu.SMEM), q_spec, new_spec, new_spec]
                  + [page_spec(p) for p in range(pages)] * 2),
        out_specs=q_spec,
        scratch_shapes=[pltpu.VMEM((page_len, n_heads, HEAD_DIM), F32),
                        pltpu.VMEM((page_len, n_heads, HEAD_DIM), F32),
                        pltpu.VMEM((n_heads, n_new, HEAD_DIM), F32),
                        pltpu.VMEM((n_heads * n_new, 1), F32)],
    )
    return pl.pallas_call(
        functools.partial(_attn_sample_kernel, n_heads=n_heads, n_new=n_new),
        grid_spec=grid_spec,
        out_shape=jax.ShapeDtypeStruct((bsz, n_new, width), F32),
        compiler_params=_cparams(2),
        name="attn_sample",
    )(page_table, b_sb, q3, knew5, vnew5, *([cache_k] * pages), *([cache_v] * pages))


def _segmented_linear_scan(a, b, seg_len):
    rows = lax.broadcasted_iota(jnp.int32, a.shape, 0) % seg_len
    d = 1
    while d < seg_len:
        keep = rows >= d
        a_prev = pltpu.roll(a, d, 0)
        b_prev = pltpu.roll(b, d, 0)
        b = jnp.where(keep, a * b_prev + b, b)
        a = jnp.where(keep, a * a_prev, a)
        d *= 2
    return a, b


def _lru_kernel(xl_ref, gl_ref, cw_ref, cb_ref, wa_ref, ba_ref, wx_ref, bx_ref, lam_ref,
                prev_ref, h0_ref, o_ref, hlast_ref, xp_ref, hcar_ref,
                *, groups, rows, blocks_per_seq, fresh):
    i = pl.program_id(0)
    tm = groups * rows
    width = xl_ref.shape[-1]
    first_block = (i % blocks_per_seq) == 0
    pad = SUBLANES

    @pl.when(first_block)
    def _():
        if fresh:
            xp_ref[:, 0:pad, :] = jnp.zeros((groups, pad, width), F32)
            hcar_ref[...] = jnp.zeros_like(hcar_ref)
        else:
            xp_ref[:, 0:pad, :] = prev_ref[...]
            hcar_ref[...] = h0_ref[...]

    xp_ref[:, pad:pad + rows, :] = xl_ref[...].reshape(groups, rows, width)
    cw = cw_ref[...]
    xc = cb_ref[...] + cw[0:1] * xp_ref[:, pl.ds(pad - 3, rows), :]
    for t in range(1, LRU_CONV_W):
        xc = xc + cw[t:t + 1] * xp_ref[:, pl.ds(pad - 3 + t, rows), :]
    xp_ref[:, 0:pad, :] = xp_ref[:, rows:rows + pad, :]
    xc = xc.reshape(tm, width)

    xcb = xc.astype(BF16)
    bw = width // LRU_BLOCKS
    r_parts, i_parts = [], []
    for n in range(LRU_BLOCKS):
        xb = xcb[:, n * bw:(n + 1) * bw]
        r_parts.append(jnp.dot(xb, wa_ref[n], preferred_element_type=F32))
        i_parts.append(jnp.dot(xb, wx_ref[n], preferred_element_type=F32))
    r = jax.nn.sigmoid(jnp.concatenate(r_parts, axis=1) + ba_ref[...])
    ig = jax.nn.sigmoid(jnp.concatenate(i_parts, axis=1) + bx_ref[...])
    log_a = (-LRU_C) * r * _softplus(-lam_ref[...])
    a = jnp.exp(log_a)
    mult = jnp.sqrt(-jnp.tanh(log_a) * (a * a + 1.0))
    if fresh:
        t_row = lax.broadcasted_iota(jnp.int32, (tm, 1), 0)
        start_row = jnp.where(first_block, 0, -1)
        mult = jnp.where(t_row == start_row, 1.0, mult)
    b = mult * ig * xc

    a_cum, h = _segmented_linear_scan(a, b, rows)
    h_in = jnp.broadcast_to(hcar_ref[...], (groups, rows, width)).reshape(tm, width)
    h = h + a_cum * h_in
    h_last = h.reshape(groups, rows, width)[:, rows - 1:rows, :]
    hcar_ref[...] = h_last
    hlast_ref[...] = h_last
    o_ref[...] = (h * _gelu_tanh(gl_ref[...])).astype(o_ref.dtype)


def _lru_call(xl, gl, cw, cb, wa_bf, ba, wx_bf, bx, lam, prev_pad, h0, *,
              n_seq, groups, rows, fresh):
    m, width = xl.shape
    tm = groups * rows
    n_blocks = m // tm
    blocks_per_seq = n_blocks // (n_seq // groups)
    bw = width // LRU_BLOCKS
    row_spec = pl.BlockSpec((tm, width), lambda i: (i, 0))
    vec_spec = pl.BlockSpec((1, width), lambda i: (0, 0))
    w_spec = pl.BlockSpec((LRU_BLOCKS, bw, bw), lambda i: (0, 0, 0))
    seq_index = lambda i: (i // blocks_per_seq, 0, 0)
    return pl.pallas_call(
        functools.partial(_lru_kernel, groups=groups, rows=rows,
                          blocks_per_seq=blocks_per_seq, fresh=fresh),
        grid=(n_blocks,),
        in_specs=[row_spec, row_spec,
                  pl.BlockSpec((LRU_CONV_W, width), lambda i: (0, 0)), vec_spec,
                  w_spec, vec_spec, w_spec, vec_spec, vec_spec,
                  pl.BlockSpec((groups, SUBLANES, width), seq_index),
                  pl.BlockSpec((groups, 1, width), seq_index)],
        out_specs=[row_spec, pl.BlockSpec((groups, 1, width), seq_index)],
        out_shape=[jax.ShapeDtypeStruct((m, width), BF16),
                   jax.ShapeDtypeStruct((n_seq, 1, width), F32)],
        scratch_shapes=[pltpu.VMEM((groups, rows + SUBLANES, width), F32),
                        pltpu.VMEM((groups, 1, width), F32)],
        compiler_params=_cparams(1),
        name="rglru",
    )(xl, gl, cw, cb.reshape(1, width), wa_bf, ba.reshape(1, width), wx_bf, bx.reshape(1, width),
      lam.reshape(1, width), prev_pad, h0)


def _outproj_kernel(attn_ref, lru_ref, x_ref, gt_ref, sh_ref, sc_ref, g2_ref, w_ref, x1_ref, hn_ref):
    half = attn_ref.shape[1]
    mix = jnp.dot(attn_ref[...].astype(BF16), w_ref[0:half, :], preferred_element_type=F32)
    mix = mix + jnp.dot(lru_ref[...], w_ref[half:2 * half, :], preferred_element_type=F32)
    x1 = x_ref[...] + gt_ref[...] * mix.reshape(x_ref.shape)
    x1_ref[...] = x1
    ms = jnp.mean(x1 * x1, axis=-1, keepdims=True)
    y = x1 * lax.rsqrt(ms + EPS) * g2_ref[...]
    hn = y * (1.0 + sc_ref[...]) + sh_ref[...]
    hn_ref[...] = hn.reshape(hn_ref.shape).astype(BF16)


def _outproj_call(attn, lru, x3, gt, sh, sc, g2, w_out_bf, *, groups_per_block, mod_index):
    n_groups, rows, d = x3.shape
    g = groups_per_block
    tm = g * rows
    m = n_groups * rows
    half = attn.shape[1]
    half_spec = pl.BlockSpec((tm, half), lambda i: (i, 0))
    x_spec = pl.BlockSpec((g, rows, d), lambda i: (i, 0, 0))
    mod_spec = pl.BlockSpec((g, 1, d), lambda i: (mod_index(i), 0, 0))
    return pl.pallas_call(
        _outproj_kernel,
        grid=(n_groups // g,),
        in_specs=[half_spec, half_spec, x_spec, mod_spec, mod_spec, mod_spec,
                  pl.BlockSpec((1, d), lambda i: (0, 0)),
                  pl.BlockSpec((d, d), lambda i: (0, 0))],
        out_specs=[x_spec, pl.BlockSpec((tm, d), lambda i: (i, 0))],
        out_shape=[jax.ShapeDtypeStruct((n_groups, rows, d), F32),
                   jax.ShapeDtypeStruct((m, d), BF16)],
        compiler_params=_cparams(1),
        name="outproj",
    )(attn, lru, x3, gt, sh, sc, g2.reshape(1, d), w_out_bf)


def _ffn_kernel(hn_ref, wg_ref, wv_ref, cwg_ref, cwv_ref, cbg_ref, cbv_ref, wd_ref,
                x1_ref, gt_ref, pg_ref, pv_ref,
                y_ref, tg_ref, tv_ref, acc_ref, cg_ref, cv_ref,
                *, groups, rows, blocks_per_seq, fresh):
    i = pl.program_id(0)
    j = pl.program_id(1)
    tm = groups * rows
    tf = wg_ref.shape[1]
    first_block = (i % blocks_per_seq) == 0
    pad = SUBLANES
    t_row = lax.broadcasted_iota(jnp.int32, (tm, tf), 0) % rows
    hn = hn_ref[...]

    def conv_half(w_ref, cw_ref, cb_ref, p_ref, carry_ref, tail_ref):
        u = jnp.dot(hn, w_ref[...], preferred_element_type=F32)
        u3 = u.reshape(groups, rows, tf)

        @pl.when(first_block)
        def _():
            carry_ref[j] = jnp.zeros((groups, pad, tf), F32)
            if not fresh:
                carry_ref[j, :, pad - 2:pad, :] = p_ref[...]

        prev = carry_ref[j]
        p1 = jnp.broadcast_to(prev[:, pad - 1:pad, :], (groups, rows, tf)).reshape(tm, tf)
        p0 = jnp.broadcast_to(prev[:, pad - 2:pad - 1, :], (groups, rows, tf)).reshape(tm, tf)
        u1 = jnp.where(t_row == 0, p1, pltpu.roll(u, 1, 0))
        u2 = jnp.where(t_row == 0, p0, jnp.where(t_row == 1, p1, pltpu.roll(u, 2, 0)))
        cw = cw_ref[...]
        uc = cb_ref[...] + cw[0:1] * u2
        uc = uc + cw[1:2] * u1
        uc = uc + cw[2:3] * u
        carry_ref[j] = u3[:, rows - pad:rows, :]
        tail_ref[...] = u3[:, rows - 2:rows, :]
        return uc

    ug = conv_half(wg_ref, cwg_ref, cbg_ref, pg_ref, cg_ref, tg_ref)
    uv = conv_half(wv_ref, cwv_ref, cbv_ref, pv_ref, cv_ref, tv_ref)
    gated = (_gelu_tanh(ug) * uv).astype(BF16)
    part = jnp.dot(gated, wd_ref[...], preferred_element_type=F32)

    @pl.when(j == 0)
    def _():
        acc_ref[...] = part

    @pl.when(j > 0)
    def _():
        acc_ref[...] += part

    @pl.when(j == pl.num_programs(1) - 1)
    def _():
        y_ref[...] = x1_ref[...] + gt_ref[...] * acc_ref[...].reshape(y_ref.shape)


def _ffn_call(hn2, w_up_bf, cw, cb, w_down_bf, x1, gt, prev_g, prev_v, *,
              n_seq, groups_per_block, mod_index, fresh):
    n_groups, rows, d = x1.shape
    g = groups_per_block
    tm = g * rows
    n_blocks = n_groups // g
    blocks_per_seq = n_blocks // (n_seq // g)
    d_ff = w_down_bf.shape[0]
    tf = FFN_TF
    nj = d_ff // tf
    seq_index = lambda i: i // blocks_per_seq
    x_spec = pl.BlockSpec((g, rows, d), lambda i, j: (i, 0, 0))
    tail_spec = pl.BlockSpec((g, FFN_CONV_W - 1, tf), lambda i, j: (seq_index(i), 0, j))
    return pl.pallas_call(
        functools.partial(_ffn_kernel, groups=g, rows=rows, blocks_per_seq=blocks_per_seq, fresh=fresh),
        grid=(n_blocks, nj),
        in_specs=[pl.BlockSpec((tm, d), lambda i, j: (i, 0)),
                  pl.BlockSpec((d, tf), lambda i, j: (0, j)),
                  pl.BlockSpec((d, tf), lambda i, j: (0, j + nj)),
                  pl.BlockSpec((FFN_CONV_W, tf), lambda i, j: (0, j)),
                  pl.BlockSpec((FFN_CONV_W, tf), lambda i, j: (0, j + nj)),
                  pl.BlockSpec((1, tf), lambda i, j: (0, j)),
                  pl.BlockSpec((1, tf), lambda i, j: (0, j + nj)),
                  pl.BlockSpec((tf, d), lambda i, j: (j, 0)),
                  x_spec,
                  pl.BlockSpec((g, 1, d), lambda i, j: (mod_index(i), 0, 0)),
                  tail_spec, tail_spec],
        out_specs=[x_spec, tail_spec, tail_spec],
        out_shape=[jax.ShapeDtypeStruct((n_groups, rows, d), F32),
                   jax.ShapeDtypeStruct((n_seq, FFN_CONV_W - 1, d_ff), F32),
                   jax.ShapeDtypeStruct((n_seq, FFN_CONV_W - 1, d_ff), F32)],
        scratch_shapes=[pltpu.VMEM((tm, d), F32),
                        pltpu.VMEM((nj, g, SUBLANES, tf), F32),
                        pltpu.VMEM((nj, g, SUBLANES, tf), F32)],
        compiler_params=_cparams(2),
        name="convffn",
    )(hn2, w_up_bf, w_up_bf, cw, cw, cb.reshape(1, 2 * d_ff), cb.reshape(1, 2 * d_ff), w_down_bf,
      x1, gt, prev_g, prev_v)


def _run_group(x, mod, attn_fn, lru_prev_pad, lru_h0, ffn_prev, fresh, groups_per_block, rows_per_block, p):
    n_seq, t, d = x.shape
    rows = rows_per_block // groups_per_block
    x3 = x.reshape(n_seq * t // rows, rows, d)
    blocks_per_seq = max(1, t // rows_per_block)
    if groups_per_block == 1:
        mod_index = lambda i: i // blocks_per_seq
    else:
        mod_index = lambda i: i
    mods = [mod[:, n:n + 1, :] for n in range(N_MOD)]
    sh1, sc1, gt1, sh2, sc2, gt2 = mods

    q, kf, kb, vf, vb, xl, gl = _inproj_call(
        x3, sh1, sc1, p["g_norm1"], p["w_in"], p["g_q"], p["g_k"],
        groups_per_block=groups_per_block, mod_index=mod_index)
    attn = attn_fn(q, kf, kb, vf, vb)
    lru, h_last = _lru_call(
        xl, gl, p["w_lru_conv"], p["b_lru_conv"], p["w_lru_a"], p["b_lru_a"], p["w_lru_x"], p["b_lru_x"],
        p["lru_lambda"], lru_prev_pad, lru_h0,
        n_seq=n_seq, groups=groups_per_block, rows=rows, fresh=fresh)
    x1, hn2 = _outproj_call(attn, lru, x3, gt1, sh2, sc2, p["g_norm2"], p["w_out"],
                            groups_per_block=groups_per_block, mod_index=mod_index)
    d_ff = p["w_down"].shape[0]
    y, tail_g, tail_v = _ffn_call(
        hn2, p["w_up"], p["w_ffn_conv"], p["b_ffn_conv"], p["w_down"], x1, gt2,
        ffn_prev[:, :, :d_ff], ffn_prev[:, :, d_ff:],
        n_seq=n_seq, groups_per_block=groups_per_block, mod_index=mod_index, fresh=fresh)
    width = xl.shape[1]
    lru_conv_new = xl.reshape(n_seq, t, width)[:, t - (LRU_CONV_W - 1):, :]
    ffn_conv_new = jnp.concatenate([tail_g, tail_v], axis=-1)
    return (y.reshape(n_seq, t, d), kf, vf, lru_conv_new, h_last.reshape(n_seq, width), ffn_conv_new)


def kernel(x_prompt, x_sample, c_prompt, c_sample, cache_k, cache_v, page_table, state_lru_conv, state_lru_h, state_ffn_conv, w_ada, b_ada, g_norm1, w_in, g_q, g_k, b_sb, w_lru_conv, b_lru_conv, w_lru_a, b_lru_a, w_lru_x, b_lru_x, lru_lambda, w_out, g_norm2, w_up, w_ffn_conv, b_ffn_conv, w_down):
    depth = w_ada.shape[0]
    assert depth == 1, "single-layer step"
    l = 0
    bsz, seq, d = x_prompt.shape
    dec_b, dec_t, _ = x_sample.shape
    n_heads = cache_k.shape[3]
    width = n_heads * HEAD_DIM
    lru_w = state_lru_h.shape[-1]
    d_ff2 = state_ffn_conv.shape[-1]

    p = dict(
        g_norm1=g_norm1[l], w_in=w_in[l].astype(BF16), g_q=g_q[l], g_k=g_k[l],
        w_lru_conv=w_lru_conv[l], b_lru_conv=b_lru_conv[l],
        w_lru_a=w_lru_a[l].astype(BF16), b_lru_a=b_lru_a[l],
        w_lru_x=w_lru_x[l].astype(BF16), b_lru_x=b_lru_x[l], lru_lambda=lru_lambda[l],
        w_out=w_out[l].astype(BF16), g_norm2=g_norm2[l], w_up=w_up[l].astype(BF16),
        w_ffn_conv=w_ffn_conv[l], b_ffn_conv=b_ffn_conv[l], w_down=w_down[l].astype(BF16))

    n_c = bsz + dec_b
    pad_c = (-n_c) % SUBLANES
    c_all = jnp.concatenate([c_prompt, c_sample, jnp.zeros((pad_c, d), F32)], axis=0)
    mod = _ada_call(c_all, w_ada[l], b_ada[l])
    mod_p = mod[:bsz].reshape(bsz, N_MOD, d)
    mod_s = mod[bsz:n_c].reshape(dec_b, N_MOD, d)

    def prompt_attn(q, kf, kb, vf, vb):
        return _attn_prompt_call(q, kb, vb, b_sb[l], batch=bsz, seq=seq)

    zeros_prev = jnp.zeros((bsz, SUBLANES, lru_w), F32)
    zeros_h = jnp.zeros((bsz, 1, lru_w), F32)
    zeros_ffn = jnp.zeros((bsz, FFN_CONV_W - 1, d_ff2), F32)
    yp, kp, vp, lcp, lhp, fcp = _run_group(
        x_prompt, mod_p, prompt_attn, zeros_prev, zeros_h, zeros_ffn, True, 1, ROW_TILE, p)

    def sample_attn(q, kf, kb, vf, vb):
        q3 = q.astype(F32).reshape(dec_b, dec_t, width)
        k5 = kf.reshape(1, dec_b, dec_t, n_heads, HEAD_DIM)
        v5 = vf.reshape(1, dec_b, dec_t, n_heads, HEAD_DIM)
        o = _attn_sample_call(q3, k5, v5, cache_k[l:l + 1], cache_v[l:l + 1], page_table, b_sb[l])
        return o.reshape(dec_b * dec_t, width)

    prev_pad = jnp.concatenate(
        [jnp.zeros((dec_b, SUBLANES - (LRU_CONV_W - 1), lru_w), F32), state_lru_conv[l]], axis=1)
    ys, ks, vs, lcs, lhs, fcs = _run_group(
        x_sample, mod_s, sample_attn, prev_pad, state_lru_h[l].reshape(dec_b, 1, lru_w),
        state_ffn_conv[l], False, dec_b, dec_b * dec_t, p)

    n_pages_p = seq // cache_k.shape[2]
    page = cache_k.shape[2]
    return (yp, ys,
            kp.reshape(1, bsz, n_pages_p, page, n_heads, HEAD_DIM),
            vp.reshape(1, bsz, n_pages_p, page, n_heads, HEAD_DIM),
            ks.reshape(1, dec_b, dec_t, n_heads, HEAD_DIM),
            vs.reshape(1, dec_b, dec_t, n_heads, HEAD_DIM),
            lcp[None], lcs[None], lhp[None], lhs[None], fcp[None], fcs[None])
```

```python
import functools
import math

import jax
import jax.numpy as jnp
from jax import lax
from jax.experimental import pallas as pl
from jax.experimental.pallas import tpu as pltpu

F32 = jnp.float32
BF16 = jnp.bfloat16

EPS = 1e-6
LRU_C = 8.0
HEAD_DIM = 128
LRU_BLOCKS = 8
LRU_CONV_W = 4
FFN_CONV_W = 3
N_MOD = 6
SB_SCALE = 1.0 / math.sqrt(HEAD_DIM)

SUBLANES = 8
VMEM_LIMIT = 56 * 1024 * 1024

ROW_TILE = 512
ATTN_TQ = 256
ATTN_TK = 256
ATTN_HEADS = 4
ADA_TN = 1024
FFN_TF = 512
PAGES_PER_STEP = 8


def _cparams(n_axes):
    return pltpu.CompilerParams(dimension_semantics=("arbitrary",) * n_axes,
                                vmem_limit_bytes=VMEM_LIMIT)


def _softplus(z):
    return jnp.maximum(z, 0.0) + jnp.log1p(jnp.exp(-jnp.abs(z)))


def _gelu_tanh(x):
    c = math.sqrt(2.0 / math.pi)
    return x * (0.5 * (1.0 + jnp.tanh(c * (x + 0.044715 * (x * x * x)))))


def _split_hi_lo(x):
    hi = x.astype(BF16)
    lo = (x - hi.astype(F32)).astype(BF16)
    return hi, lo


def _suffix_sum_matrix(n_keys):
    j = lax.broadcasted_iota(jnp.int32, (2 * n_keys, n_keys), 0) % n_keys
    s = lax.broadcasted_iota(jnp.int32, (2 * n_keys, n_keys), 1)
    return jnp.where(j >= s, 1.0, 0.0).astype(BF16)


def _ada_kernel(c_ref, w_ref, b_ref, o_ref):
    c = c_ref[...]
    s = (c * jax.nn.sigmoid(c)).astype(BF16)
    o_ref[...] = jnp.dot(s, w_ref[...].astype(BF16), preferred_element_type=F32) + b_ref[...]


def _ada_call(c_all, w_ada, b_ada):
    m, d = c_all.shape
    n = w_ada.shape[1]
    return pl.pallas_call(
        _ada_kernel,
        grid=(n // ADA_TN,),
        in_specs=[pl.BlockSpec((m, d), lambda j: (0, 0)),
                  pl.BlockSpec((d, ADA_TN), lambda j: (0, j)),
                  pl.BlockSpec((1, ADA_TN), lambda j: (0, j))],
        out_specs=pl.BlockSpec((m, ADA_TN), lambda j: (0, j)),
        out_shape=jax.ShapeDtypeStruct((m, n), F32),
        compiler_params=_cparams(1),
        name="adaln",
    )(c_all, w_ada, b_ada.reshape(1, n))


def _head_rmsnorm(a, g, n_heads):
    outs = []
    for h in range(n_heads):
        ah = a[:, h * HEAD_DIM:(h + 1) * HEAD_DIM]
        ms = jnp.mean(ah * ah, axis=-1, keepdims=True)
        outs.append(ah * lax.rsqrt(ms + EPS) * g)
    return outs


def _inproj_kernel(x_ref, sh_ref, sc_ref, g1_ref, w_ref, gq_ref, gk_ref,
                   q_ref, kf_ref, kb_ref, vf_ref, vb_ref, xl_ref, gl_ref, hn_ref, *, n_heads):
    j = pl.program_id(1)

    @pl.when(j == 0)
    def _():
        x = x_ref[...]
        ms = jnp.mean(x * x, axis=-1, keepdims=True)
        y = x * lax.rsqrt(ms + EPS) * g1_ref[...]
        hn = y * (1.0 + sc_ref[...]) + sh_ref[...]
        hn_ref[...] = hn.reshape(hn_ref.shape).astype(BF16)

    acc = jnp.dot(hn_ref[...], w_ref[...], preferred_element_type=F32)

    @pl.when(j == 0)
    def _():
        for h, qh in enumerate(_head_rmsnorm(acc, gq_ref[...], n_heads)):
            q_ref[:, h * HEAD_DIM:(h + 1) * HEAD_DIM] = (qh * SB_SCALE).astype(BF16)

    @pl.when(j == 1)
    def _():
        tm = acc.shape[0]
        for h, kh in enumerate(_head_rmsnorm(acc, gk_ref[...], n_heads)):
            kf_ref[pl.ds(h, tm, stride=n_heads), :] = kh
            kb_ref[:, h * HEAD_DIM:(h + 1) * HEAD_DIM] = kh.astype(BF16)

    @pl.when(j == 2)
    def _():
        tm = acc.shape[0]
        for h in range(n_heads):
            vf_ref[pl.ds(h, tm, stride=n_heads), :] = acc[:, h * HEAD_DIM:(h + 1) * HEAD_DIM]
        vb_ref[...] = acc.astype(BF16)

    @pl.when(j == 3)
    def _():
        xl_ref[...] = acc

    @pl.when(j == 4)
    def _():
        gl_ref[...] = acc


def _inproj_call(x3, sh, sc, g1, w_in_bf, g_q, g_k, *, groups_per_block, mod_index):
    n_groups, rows, d = x3.shape
    g = groups_per_block
    tm = g * rows
    m = n_groups * rows
    n_blocks = n_groups // g
    width = w_in_bf.shape[1] // 5
    n_heads = width // HEAD_DIM
    row_spec = pl.BlockSpec((tm, width), lambda i, j: (i, 0))
    head_spec = pl.BlockSpec((tm * n_heads, HEAD_DIM), lambda i, j: (i, 0))
    mod_spec = pl.BlockSpec((g, 1, d), lambda i, j: (mod_index(i), 0, 0))
    f32_out = jax.ShapeDtypeStruct((m, width), F32)
    head_out = jax.ShapeDtypeStruct((m * n_heads, HEAD_DIM), F32)
    bf_out = jax.ShapeDtypeStruct((m, width), BF16)
    return pl.pallas_call(
        functools.partial(_inproj_kernel, n_heads=n_heads),
        grid=(n_blocks, 5),
        in_specs=[pl.BlockSpec((g, rows, d), lambda i, j: (i, 0, 0)),
                  mod_spec, mod_spec,
                  pl.BlockSpec((1, d), lambda i, j: (0, 0)),
                  pl.BlockSpec((d, width), lambda i, j: (0, j)),
                  pl.BlockSpec((1, HEAD_DIM), lambda i, j: (0, 0)),
                  pl.BlockSpec((1, HEAD_DIM), lambda i, j: (0, 0))],
        out_specs=[row_spec, head_spec, row_spec, head_spec, row_spec, row_spec, row_spec],
        out_shape=[bf_out, head_out, bf_out, head_out, bf_out, f32_out, f32_out],
        scratch_shapes=[pltpu.VMEM((tm, d), BF16)],
        compiler_params=_cparams(2),
        name="inproj",
    )(x3, sh, sc, g1.reshape(1, d), w_in_bf, g_q.reshape(1, HEAD_DIM), g_k.reshape(1, HEAD_DIM))


def _sb_mass(z, suffix_mat, mask):
    sp = _softplus(z)
    if mask is not None:
        sp = jnp.where(mask, sp, 0.0)
    hi, lo = _split_hi_lo(sp)
    return jnp.dot(jnp.concatenate([hi, lo], axis=1), suffix_mat, preferred_element_type=F32)


def _sb_weights(z, cums, later, mask):
    w = jnp.exp(z - cums - later)
    if mask is not None:
        w = jnp.where(mask, w, 0.0)
    return w, later + cums[:, 0:1]


def _attn_prompt_kernel(bias_ref, q_ref, k_ref, v_ref, o_ref):
    hg = pl.program_id(1)
    qi = pl.program_id(2)
    suffix_mat = _suffix_sum_matrix(ATTN_TK)
    heads = range(ATTN_HEADS)
    cols = [slice(hh * HEAD_DIM, (hh + 1) * HEAD_DIM) for hh in heads]
    biases = [bias_ref[hg * ATTN_HEADS + hh] for hh in heads]
    qs = [q_ref[:, c] for c in cols]

    def tile(j, carry, mask):
        start = pl.multiple_of(j * ATTN_TK, ATTN_TK)
        nt = (((1,), (1,)), ((), ()))
        zs = [lax.dot_general(qs[hh], k_ref[pl.ds(start, ATTN_TK), cols[hh]], nt,
                              preferred_element_type=F32) + biases[hh] for hh in heads]
        cums = [_sb_mass(zs[hh], suffix_mat, mask) for hh in heads]
        ws = [_sb_weights(zs[hh], cums[hh], carry[hh][1], mask) for hh in heads]
        return tuple(
            (carry[hh][0] + jnp.dot(ws[hh][0].astype(BF16), v_ref[pl.ds(start, ATTN_TK), cols[hh]],
                                    preferred_element_type=F32), ws[hh][1]) for hh in heads)

    row = lax.broadcasted_iota(jnp.int32, (ATTN_TQ, ATTN_TK), 0)
    col = lax.broadcasted_iota(jnp.int32, (ATTN_TQ, ATTN_TK), 1)
    carry = tuple((jnp.zeros((ATTN_TQ, HEAD_DIM), F32), jnp.zeros((ATTN_TQ, 1), F32)) for _ in heads)
    carry = tile(qi, carry, col < row)
    carry = lax.fori_loop(0, qi, lambda t, c: tile(qi - 1 - t, c, None), carry)
    for hh in heads:
        o_ref[:, cols[hh]] = carry[hh][0].astype(o_ref.dtype)


def _attn_prompt_call(q, k, v, b_sb, *, batch, seq):
    m, width = q.shape
    n_heads = width // HEAD_DIM
    nq = seq // ATTN_TQ
    gw = ATTN_HEADS * HEAD_DIM
    q_spec = pl.BlockSpec((ATTN_TQ, gw), lambda b, h, i: (b * nq + i, h))
    kv_spec = pl.BlockSpec((seq, gw), lambda b, h, i: (b, h))
    return pl.pallas_call(
        _attn_prompt_kernel,
        grid=(batch, n_heads // ATTN_HEADS, nq),
        in_specs=[pl.BlockSpec(memory_space=pltpu.SMEM), q_spec, kv_spec, kv_spec],
        out_specs=q_spec,
        out_shape=jax.ShapeDtypeStruct((m, width), BF16),
        compiler_params=_cparams(3),
        name="attn_prompt",
    )(b_sb, q, k, v)


def _attn_sample_kernel(pt_ref, bias_ref, q_ref, knew_ref, vnew_ref, *rest, n_heads, n_new):
    del pt_ref
    pages = PAGES_PER_STEP
    k_refs = rest[:pages]
    v_refs = rest[pages:2 * pages]
    o_ref = rest[2 * pages]
    kpad_ref, vpad_ref, acc_ref, later_ref = rest[2 * pages + 1:]
    g = pl.program_id(1)
    page_len = kpad_ref.shape[0] // n_heads
    suffix_mat = _suffix_sum_matrix(page_len)
    bias = jnp.concatenate([jnp.full((n_new, 1), bias_ref[h], F32) for h in range(n_heads)], axis=0)
    nt = (((1,), (1,)), ((), ()))

    def visit(k_pages, v_pages, mask):
        qh = [q_ref[0, :, h * HEAD_DIM:(h + 1) * HEAD_DIM].astype(BF16) for h in range(n_heads)]
        zs = []
        for kp in k_pages:
            z = [lax.dot_general(qh[h], kp[pl.ds(h, page_len, stride=n_heads), :].astype(BF16), nt,
                                 preferred_element_type=F32) for h in range(n_heads)]
            zs.append(jnp.concatenate(z, axis=0) + bias)
        cums = [_sb_mass(z, suffix_mat, mask) for z in zs]
        later = later_ref[...]
        ws = []
        for z, c in zip(zs, cums):
            w, later = _sb_weights(z, c, later, mask)
            ws.append(w)
        later_ref[...] = later
        acc = [acc_ref[h] for h in range(n_heads)]
        for w, vp in zip(ws, v_pages):
            for h in range(n_heads):
                wh = w[h * n_new:(h + 1) * n_new].astype(BF16)
                vh = vp[pl.ds(h, page_len, stride=n_heads), :].astype(BF16)
                acc[h] = acc[h] + jnp.dot(wh, vh, preferred_element_type=F32)
        for h in range(n_heads):
            acc_ref[h] = acc[h]

    @pl.when(g == 0)
    def _():
        acc_ref[...] = jnp.zeros_like(acc_ref)
        later_ref[...] = jnp.zeros_like(later_ref)
        kpad_ref[...] = jnp.zeros_like(kpad_ref)
        vpad_ref[...] = jnp.zeros_like(vpad_ref)
        kpad_ref[0:n_new * n_heads, :] = knew_ref[...]
        vpad_ref[0:n_new * n_heads, :] = vnew_ref[...]
        row = lax.broadcasted_iota(jnp.int32, (n_heads * n_new, page_len), 0) % n_new
        col = lax.broadcasted_iota(jnp.int32, (n_heads * n_new, page_len), 1)
        visit([kpad_ref], [vpad_ref], col < row)

    visit([r.at[0] for r in k_refs], [r.at[0] for r in v_refs], None)

    @pl.when(g == pl.num_programs(1) - 1)
    def _():
        for h in range(n_heads):
            o_ref[0, :, h * HEAD_DIM:(h + 1) * HEAD_DIM] = acc_ref[h].astype(o_ref.dtype)


def _attn_sample_call(q3, knew, vnew, cache_k, cache_v, page_table, b_sb):
    bsz, n_new, width = q3.shape
    n_heads = width // HEAD_DIM
    n_pages = page_table.shape[1]
    page_rows = cache_k.shape[1]
    pages = PAGES_PER_STEP
    steps = n_pages // pages

    def page_spec(p):
        def index(b, g, pt):
            return (pt[b, n_pages - 1 - (g * pages + p)], 0, 0)
        return pl.BlockSpec((1, page_rows, HEAD_DIM), index)

    new_spec = pl.BlockSpec((n_new * n_heads, HEAD_DIM), lambda b, g, pt: (b, 0))
    q_spec = pl.BlockSpec((1, n_new, width), lambda b, g, pt: (b, 0, 0))
    grid_spec = pltpu.PrefetchScalarGridSpec(
        num_scalar_prefetch=1,
        grid=(bsz, steps),
        in_specs=([pl.BlockSpec(memory_space=pltpu.SMEM), q_spec, new_spec, new_spec]
                  + [page_spec(p) for p in range(pages)] * 2),
        out_specs=q_spec,
        scratch_shapes=[pltpu.VMEM((page_rows, HEAD_DIM), F32),
                        pltpu.VMEM((page_rows, HEAD_DIM), F32),
                        pltpu.VMEM((n_heads, n_new, HEAD_DIM), F32),
                        pltpu.VMEM((n_heads * n_new, 1), F32)],
    )
    return pl.pallas_call(
        functools.partial(_attn_sample_kernel, n_heads=n_heads, n_new=n_new),
        grid_spec=grid_spec,
        out_shape=jax.ShapeDtypeStruct((bsz, n_new, width), F32),
        compiler_params=_cparams(2),
        name="attn_sample",
    )(page_table, b_sb, q3, knew, vnew, *([cache_k] * pages), *([cache_v] * pages))


def _segmented_linear_scan(a, b, seg_len):
    rows = lax.broadcasted_iota(jnp.int32, a.shape, 0) % seg_len
    d = 1
    while d < seg_len:
        keep = rows >= d
        a_prev = pltpu.roll(a, d, 0)
        b_prev = pltpu.roll(b, d, 0)
        b = jnp.where(keep, a * b_prev + b, b)
        a = jnp.where(keep, a * a_prev, a)
        d *= 2
    return a, b


def _lru_kernel(xl_ref, gl_ref, cw_ref, cb_ref, wa_ref, ba_ref, wx_ref, bx_ref, lam_ref,
                prev_ref, h0_ref, o_ref, hlast_ref, xp_ref, hcar_ref,
                *, groups, rows, blocks_per_seq, fresh):
    i = pl.program_id(0)
    tm = groups * rows
    width = xl_ref.shape[-1]
    first_block = (i % blocks_per_seq) == 0
    pad = SUBLANES

    @pl.when(first_block)
    def _():
        if fresh:
            xp_ref[:, 0:pad, :] = jnp.zeros((groups, pad, width), F32)
            hcar_ref[...] = jnp.zeros_like(hcar_ref)
        else:
            xp_ref[:, 0:pad, :] = prev_ref[...]
            hcar_ref[...] = h0_ref[...]

    xp_ref[:, pad:pad + rows, :] = xl_ref[...].reshape(groups, rows, width)
    cw = cw_ref[...]
    xc = cb_ref[...] + cw[0:1] * xp_ref[:, pl.ds(pad - 3, rows), :]
    for t in range(1, LRU_CONV_W):
        xc = xc + cw[t:t + 1] * xp_ref[:, pl.ds(pad - 3 + t, rows), :]
    xp_ref[:, 0:pad, :] = xp_ref[:, rows:rows + pad, :]
    xc = xc.reshape(tm, width)

    xcb = xc.astype(BF16)
    bw = width // LRU_BLOCKS
    r_parts, i_parts = [], []
    for n in range(LRU_BLOCKS):
        xb = xcb[:, n * bw:(n + 1) * bw]
        r_parts.append(jnp.dot(xb, wa_ref[n], preferred_element_type=F32))
        i_parts.append(jnp.dot(xb, wx_ref[n], preferred_element_type=F32))
    r = jax.nn.sigmoid(jnp.concatenate(r_parts, axis=1) + ba_ref[...])
    ig = jax.nn.sigmoid(jnp.concatenate(i_parts, axis=1) + bx_ref[...])
    log_a = (-LRU_C) * r * _softplus(-lam_ref[...])
    a = jnp.exp(log_a)
    mult = jnp.sqrt(-jnp.tanh(log_a) * (a * a + 1.0))
    if fresh:
        t_row = lax.broadcasted_iota(jnp.int32, (tm, 1), 0)
        start_row = jnp.where(first_block, 0, -1)
        mult = jnp.where(t_row == start_row, 1.0, mult)
    b = mult * ig * xc

    a_cum, h = _segmented_linear_scan(a, b, rows)
    h_in = jnp.broadcast_to(hcar_ref[...], (groups, rows, width)).reshape(tm, width)
    h = h + a_cum * h_in
    h_last = h.reshape(groups, rows, width)[:, rows - 1:rows, :]
    hcar_ref[...] = h_last
    hlast_ref[...] = h_last
    o_ref[...] = (h * _gelu_tanh(gl_ref[...])).astype(o_ref.dtype)


def _lru_call(xl, gl, cw, cb, wa_bf, ba, wx_bf, bx, lam, prev_pad, h0, *,
              n_seq, groups, rows, fresh):
    m, width = xl.shape
    tm = groups * rows
    n_blocks = m // tm
    blocks_per_seq = n_blocks // (n_seq // groups)
    bw = width // LRU_BLOCKS
    row_spec = pl.BlockSpec((tm, width), lambda i: (i, 0))
    vec_spec = pl.BlockSpec((1, width), lambda i: (0, 0))
    w_spec = pl.BlockSpec((LRU_BLOCKS, bw, bw), lambda i: (0, 0, 0))
    seq_index = lambda i: (i // blocks_per_seq, 0, 0)
    return pl.pallas_call(
        functools.partial(_lru_kernel, groups=groups, rows=rows,
                          blocks_per_seq=blocks_per_seq, fresh=fresh),
        grid=(n_blocks,),
        in_specs=[row_spec, row_spec,
                  pl.BlockSpec((LRU_CONV_W, width), lambda i: (0, 0)), vec_spec,
                  w_spec, vec_spec, w_spec, vec_spec, vec_spec,
                  pl.BlockSpec((groups, SUBLANES, width), seq_index),
                  pl.BlockSpec((groups, 1, width), seq_index)],
        out_specs=[row_spec, pl.BlockSpec((groups, 1, width), seq_index)],
        out_shape=[jax.ShapeDtypeStruct((m, width), BF16),
                   jax.ShapeDtypeStruct((n_seq, 1, width), F32)],
        scratch_shapes=[pltpu.VMEM((groups, rows + SUBLANES, width), F32),
                        pltpu.VMEM((groups, 1, width), F32)],
        compiler_params=_cparams(1),
        name="rglru",
    )(xl, gl, cw, cb.reshape(1, width), wa_bf, ba.reshape(1, width), wx_bf, bx.reshape(1, width),
      lam.reshape(1, width), prev_pad, h0)


def _outproj_kernel(attn_ref, lru_ref, x_ref, gt_ref, sh_ref, sc_ref, g2_ref, w_ref, x1_ref, hn_ref):
    half = attn_ref.shape[1]
    mix = jnp.dot(attn_ref[...].astype(BF16), w_ref[0:half, :], preferred_element_type=F32)
    mix = mix + jnp.dot(lru_ref[...], w_ref[half:2 * half, :], preferred_element_type=F32)
    x1 = x_ref[...] + gt_ref[...] * mix.reshape(x_ref.shape)
    x1_ref[...] = x1
    ms = jnp.mean(x1 * x1, axis=-1, keepdims=True)
    y = x1 * lax.rsqrt(ms + EPS) * g2_ref[...]
    hn = y * (1.0 + sc_ref[...]) + sh_ref[...]
    hn_ref[...] = hn.reshape(hn_ref.shape).astype(BF16)


def _outproj_call(attn, lru, x3, gt, sh, sc, g2, w_out_bf, *, groups_per_block, mod_index):
    n_groups, rows, d = x3.shape
    g = groups_per_block
    tm = g * rows
    m = n_groups * rows
    half = attn.shape[1]
    half_spec = pl.BlockSpec((tm, half), lambda i: (i, 0))
    x_spec = pl.BlockSpec((g, rows, d), lambda i: (i, 0, 0))
    mod_spec = pl.BlockSpec((g, 1, d), lambda i: (mod_index(i), 0, 0))
    return pl.pallas_call(
        _outproj_kernel,
        grid=(n_groups // g,),
        in_specs=[half_spec, half_spec, x_spec, mod_spec, mod_spec, mod_spec,
                  pl.BlockSpec((1, d), lambda i: (0, 0)),
                  pl.BlockSpec((d, d), lambda i: (0, 0))],
        out_specs=[x_spec, pl.BlockSpec((tm, d), lambda i: (i, 0))],
        out_shape=[jax.ShapeDtypeStruct((n_groups, rows, d), F32),
                   jax.ShapeDtypeStruct((m, d), BF16)],
        compiler_params=_cparams(1),
        name="outproj",
    )(attn, lru, x3, gt, sh, sc, g2.reshape(1, d), w_out_bf)


def _ffn_kernel(hn_ref, wg_ref, wv_ref, cwg_ref, cwv_ref, cbg_ref, cbv_ref, wd_ref,
                x1_ref, gt_ref, pg_ref, pv_ref,
                y_ref, tg_ref, tv_ref, acc_ref, cg_ref, cv_ref,
                *, groups, rows, blocks_per_seq, fresh):
    i = pl.program_id(0)
    j = pl.program_id(1)
    tm = groups * rows
    tf = wg_ref.shape[1]
    first_block = (i % blocks_per_seq) == 0
    pad = SUBLANES
    if not fresh and blocks_per_seq != 1:
        raise NotImplementedError("continuing sequences must fit one row block")

    @pl.when(j == 0)
    def _():
        acc_ref[...] = jnp.zeros_like(acc_ref)

    @pl.when(jnp.logical_and(i == 0, j == 0))
    def _():
        cg_ref[...] = jnp.zeros_like(cg_ref)
        cv_ref[...] = jnp.zeros_like(cv_ref)

    hn = hn_ref[...]

    def conv_half(w_ref, cw_ref, cb_ref, p_ref, carry_ref, tail_ref):
        u = jnp.dot(hn, w_ref[...], preferred_element_type=F32)
        u3 = u.reshape(groups, rows, tf)
        if fresh:
            prev = jnp.where(first_block, 0.0, carry_ref[j])
            p0, p1 = prev[:, pad - 2:pad - 1, :], prev[:, pad - 1:pad, :]
        else:
            p0, p1 = p_ref[:, 0:1, :], p_ref[:, 1:2, :]
        u1 = pltpu.roll(u, 1, 0)
        u2 = pltpu.roll(u, 2, 0)
        if groups == 1:
            top = lax.broadcasted_iota(jnp.int32, (pad, tf), 0)
            u1_top = jnp.where(top == 0, p1[0], u1[0:pad])
            u2_top = jnp.where(top == 0, p0[0], jnp.where(top == 1, p1[0], u2[0:pad]))
            u1 = jnp.concatenate([u1_top, u1[pad:]], axis=0)
            u2 = jnp.concatenate([u2_top, u2[pad:]], axis=0)
        else:
            t_row = lax.broadcasted_iota(jnp.int32, (tm, tf), 0) % rows
            p1r = jnp.broadcast_to(p1, (groups, rows, tf)).reshape(tm, tf)
            p0r = jnp.broadcast_to(p0, (groups, rows, tf)).reshape(tm, tf)
            u1 = jnp.where(t_row == 0, p1r, u1)
            u2 = jnp.where(t_row == 0, p0r, jnp.where(t_row == 1, p1r, u2))
        cw = cw_ref[...]
        uc = cb_ref[...] + cw[0:1] * u2
        uc = uc + cw[1:2] * u1
        uc = uc + cw[2:3] * u
        carry_ref[j] = u3[:, rows - pad:rows, :]
        tail_ref[...] = u3[:, rows - 2:rows, :]
        return uc

    ug = conv_half(wg_ref, cwg_ref, cbg_ref, pg_ref, cg_ref, tg_ref)
    uv = conv_half(wv_ref, cwv_ref, cbv_ref, pv_ref, cv_ref, tv_ref)
    gated = (_gelu_tanh(ug) * uv).astype(BF16)
    acc_ref[...] += jnp.dot(gated, wd_ref[...], preferred_element_type=F32)

    @pl.when(j == pl.num_programs(1) - 1)
    def _():
        y_ref[...] = x1_ref[...] + gt_ref[...] * acc_ref[...].reshape(y_ref.shape)


def _ffn_call(hn2, w_up_bf, cw, cb, w_down_bf, x1, gt, prev_g, prev_v, *,
              n_seq, groups_per_block, mod_index, fresh):
    n_groups, rows, d = x1.shape
    g = groups_per_block
    tm = g * rows
    n_blocks = n_groups // g
    blocks_per_seq = n_blocks // (n_seq // g)
    d_ff = w_down_bf.shape[0]
    tf = FFN_TF
    nj = d_ff // tf
    seq_index = lambda i: i // blocks_per_seq
    x_spec = pl.BlockSpec((g, rows, d), lambda i, j: (i, 0, 0))
    prev_spec = pl.BlockSpec((g, FFN_CONV_W - 1, tf), lambda i, j: (seq_index(i), 0, j))
    tail_spec = pl.BlockSpec((g, FFN_CONV_W - 1, tf), lambda i, j: (i, 0, j))
    tail_out = jax.ShapeDtypeStruct((n_groups, FFN_CONV_W - 1, d_ff), F32)
    y, tail_g, tail_v = pl.pallas_call(
        functools.partial(_ffn_kernel, groups=g, rows=rows, blocks_per_seq=blocks_per_seq, fresh=fresh),
        grid=(n_blocks, nj),
        in_specs=[pl.BlockSpec((tm, d), lambda i, j: (i, 0)),
                  pl.BlockSpec((d, tf), lambda i, j: (0, j)),
                  pl.BlockSpec((d, tf), lambda i, j: (0, j + nj)),
                  pl.BlockSpec((FFN_CONV_W, tf), lambda i, j: (0, j)),
                  pl.BlockSpec((FFN_CONV_W, tf), lambda i, j: (0, j + nj)),
                  pl.BlockSpec((1, tf), lambda i, j: (0, j)),
                  pl.BlockSpec((1, tf), lambda i, j: (0, j + nj)),
                  pl.BlockSpec((tf, d), lambda i, j: (j, 0)),
                  x_spec,
                  pl.BlockSpec((g, 1, d), lambda i, j: (mod_index(i), 0, 0)),
                  prev_spec, prev_spec],
        out_specs=[x_spec, tail_spec, tail_spec],
        out_shape=[jax.ShapeDtypeStruct((n_groups, rows, d), F32), tail_out, tail_out],
        scratch_shapes=[pltpu.VMEM((tm, d), F32),
                        pltpu.VMEM((nj, g, SUBLANES, tf), F32),
                        pltpu.VMEM((nj, g, SUBLANES, tf), F32)],
        compiler_params=_cparams(2),
        name="convffn",
    )(hn2, w_up_bf, w_up_bf, cw, cw, cb.reshape(1, 2 * d_ff), cb.reshape(1, 2 * d_ff), w_down_bf,
      x1, gt, prev_g, prev_v)
    last = lambda t: t.reshape(n_seq, n_groups // n_seq, FFN_CONV_W - 1, d_ff)[:, -1]
    return y, last(tail_g), last(tail_v)


def _run_group(x, mod, attn_fn, lru_prev_pad, lru_h0, ffn_prev, fresh, groups_per_block, rows_per_block, p):
    n_seq, t, d = x.shape
    rows = rows_per_block // groups_per_block
    x3 = x.reshape(n_seq * t // rows, rows, d)
    blocks_per_seq = max(1, t // rows_per_block)
    if groups_per_block == 1:
        mod_index = lambda i: i // blocks_per_seq
    else:
        mod_index = lambda i: i
    mods = [mod[:, n:n + 1, :] for n in range(N_MOD)]
    sh1, sc1, gt1, sh2, sc2, gt2 = mods

    q, kf, kb, vf, vb, xl, gl = _inproj_call(
        x3, sh1, sc1, p["g_norm1"], p["w_in"], p["g_q"], p["g_k"],
        groups_per_block=groups_per_block, mod_index=mod_index)
    attn = attn_fn(q, kf, kb, vf, vb)
    lru, h_last = _lru_call(
        xl, gl, p["w_lru_conv"], p["b_lru_conv"], p["w_lru_a"], p["b_lru_a"], p["w_lru_x"], p["b_lru_x"],
        p["lru_lambda"], lru_prev_pad, lru_h0,
        n_seq=n_seq, groups=groups_per_block, rows=rows, fresh=fresh)
    x1, hn2 = _outproj_call(attn, lru, x3, gt1, sh2, sc2, p["g_norm2"], p["w_out"],
                            groups_per_block=groups_per_block, mod_index=mod_index)
    d_ff = p["w_down"].shape[0]
    y, tail_g, tail_v = _ffn_call(
        hn2, p["w_up"], p["w_ffn_conv"], p["b_ffn_conv"], p["w_down"], x1, gt2,
        ffn_prev[:, :, :d_ff], ffn_prev[:, :, d_ff:],
        n_seq=n_seq, groups_per_block=groups_per_block, mod_index=mod_index, fresh=fresh)
    width = xl.shape[1]
    lru_conv_new = xl.reshape(n_seq, t, width)[:, t - (LRU_CONV_W - 1):, :]
    ffn_conv_new = jnp.concatenate([tail_g, tail_v], axis=-1)
    return (y.reshape(n_seq, t, d), kf, vf, lru_conv_new, h_last.reshape(n_seq, width), ffn_conv_new)


def kernel(x_prompt, x_sample, c_prompt, c_sample, cache_k, cache_v, page_table, state_lru_conv, state_lru_h, state_ffn_conv, w_ada, b_ada, g_norm1, w_in, g_q, g_k, b_sb, w_lru_conv, b_lru_conv, w_lru_a, b_lru_a, w_lru_x, b_lru_x, lru_lambda, w_out, g_norm2, w_up, w_ffn_conv, b_ffn_conv, w_down):
    depth = w_ada.shape[0]
    assert depth == 1, "single-layer step"
    l = 0
    bsz, seq, d = x_prompt.shape
    dec_b, dec_t, _ = x_sample.shape
    page, n_heads = cache_k.shape[2], cache_k.shape[3]
    width = n_heads * HEAD_DIM
    lru_w = state_lru_h.shape[-1]
    d_ff2 = state_ffn_conv.shape[-1]

    p = dict(
        g_norm1=g_norm1[l], w_in=w_in[l].astype(BF16), g_q=g_q[l], g_k=g_k[l],
        w_lru_conv=w_lru_conv[l], b_lru_conv=b_lru_conv[l],
        w_lru_a=w_lru_a[l].astype(BF16), b_lru_a=b_lru_a[l],
        w_lru_x=w_lru_x[l].astype(BF16), b_lru_x=b_lru_x[l], lru_lambda=lru_lambda[l],
        w_out=w_out[l].astype(BF16), g_norm2=g_norm2[l], w_up=w_up[l].astype(BF16),
        w_ffn_conv=w_ffn_conv[l], b_ffn_conv=b_ffn_conv[l], w_down=w_down[l].astype(BF16))

    n_c = bsz + dec_b
    pad_c = (-n_c) % SUBLANES
    c_all = jnp.concatenate([c_prompt, c_sample, jnp.zeros((pad_c, d), F32)], axis=0)
    mod = _ada_call(c_all, w_ada[l], b_ada[l])
    mod_p = mod[:bsz].reshape(bsz, N_MOD, d)
    mod_s = mod[bsz:n_c].reshape(dec_b, N_MOD, d)

    def prompt_attn(q, kf, kb, vf, vb):
        return _attn_prompt_call(q, kb, vb, b_sb[l], batch=bsz, seq=seq)

    zeros_prev = jnp.zeros((bsz, SUBLANES, lru_w), F32)
    zeros_h = jnp.zeros((bsz, 1, lru_w), F32)
    zeros_ffn = jnp.zeros((bsz, FFN_CONV_W - 1, d_ff2), F32)
    yp, kp, vp, lcp, lhp, fcp = _run_group(
        x_prompt, mod_p, prompt_attn, zeros_prev, zeros_h, zeros_ffn, True, 1, ROW_TILE, p)

    def sample_attn(q, kf, kb, vf, vb):
        q3 = q.astype(F32).reshape(dec_b, dec_t, width)
        pool = cache_k.shape[1]
        ck = cache_k[l].reshape(pool, page * n_heads, HEAD_DIM)
        cv = cache_v[l].reshape(pool, page * n_heads, HEAD_DIM)
        o = _attn_sample_call(q3, kf, vf, ck, cv, page_table, b_sb[l])
        return o.reshape(dec_b * dec_t, width)

    prev_pad = jnp.concatenate(
        [jnp.zeros((dec_b, SUBLANES - (LRU_CONV_W - 1), lru_w), F32), state_lru_conv[l]], axis=1)
    ys, ks, vs, lcs, lhs, fcs = _run_group(
        x_sample, mod_s, sample_attn, prev_pad, state_lru_h[l].reshape(dec_b, 1, lru_w),
        state_ffn_conv[l], False, dec_b, dec_b * dec_t, p)

    n_pages_p = seq // page
    return (yp, ys,
            kp.reshape(1, bsz, n_pages_p, page, n_heads, HEAD_DIM),
            vp.reshape(1, bsz, n_pages_p, page, n_heads, HEAD_DIM),
            ks.reshape(1, dec_b, dec_t, n_heads, HEAD_DIM),
            vs.reshape(1, dec_b, dec_t, n_heads, HEAD_DIM),
            lcp[None], lcs[None], lhp[None], lhs[None], fcp[None], fcs[None])
```

```python
import functools
import math

import jax
import jax.numpy as jnp
from jax import lax
from jax.experimental import pallas as pl
from jax.experimental.pallas import tpu as pltpu

F32 = jnp.float32
BF16 = jnp.bfloat16

EPS = 1e-6
LRU_C = 8.0
HEAD_DIM = 128
LRU_BLOCKS = 8
LRU_CONV_W = 4
FFN_CONV_W = 3
N_MOD = 6
SB_SCALE = 1.0 / math.sqrt(HEAD_DIM)

SUBLANES = 8
VMEM_LIMIT = 56 * 1024 * 1024

ROW_TILE = 512
INPROJ_SUB_BLOCKS = 2
ATTN_TQ = 256
ATTN_TK = 256
ATTN_HEADS = 4
ADA_TN = 1024
FFN_TF = 512
FFN_SUB_BLOCKS = 2
PAGES_PER_STEP = 8


def _cparams(n_axes):
    return pltpu.CompilerParams(dimension_semantics=("arbitrary",) * n_axes,
                                vmem_limit_bytes=VMEM_LIMIT)


def _softplus(z):
    return jnp.maximum(z, 0.0) + jnp.log1p(jnp.exp(-jnp.abs(z)))


def _softplus_logits(z):
    return jnp.maximum(z, 0.0) + jnp.log(1.0 + jnp.exp(-jnp.abs(z)))


def _gelu_tanh(x):
    c = math.sqrt(2.0 / math.pi)
    return x * (0.5 * (1.0 + jnp.tanh(c * (x + 0.044715 * (x * x * x)))))


def _split_hi_lo(x):
    hi = x.astype(BF16)
    lo = (x - hi.astype(F32)).astype(BF16)
    return hi, lo


def _suffix_sum_matrix(n_keys):
    j = lax.broadcasted_iota(jnp.int32, (2 * n_keys, n_keys), 0) % n_keys
    s = lax.broadcasted_iota(jnp.int32, (2 * n_keys, n_keys), 1)
    return jnp.where(j >= s, 1.0, 0.0).astype(BF16)


def _ada_kernel(c_ref, w_ref, b_ref, o_ref):
    c = c_ref[...]
    s = (c * jax.nn.sigmoid(c)).astype(BF16)
    o_ref[...] = jnp.dot(s, w_ref[...].astype(BF16), preferred_element_type=F32) + b_ref[...]


def _ada_call(c_all, w_ada, b_ada):
    m, d = c_all.shape
    n = w_ada.shape[1]
    return pl.pallas_call(
        _ada_kernel,
        grid=(n // ADA_TN,),
        in_specs=[pl.BlockSpec((m, d), lambda j: (0, 0)),
                  pl.BlockSpec((d, ADA_TN), lambda j: (0, j)),
                  pl.BlockSpec((1, ADA_TN), lambda j: (0, j))],
        out_specs=pl.BlockSpec((m, ADA_TN), lambda j: (0, j)),
        out_shape=jax.ShapeDtypeStruct((m, n), F32),
        compiler_params=_cparams(1),
        name="adaln",
    )(c_all, w_ada, b_ada.reshape(1, n))


def _head_rmsnorm(a, g, n_heads):
    outs = []
    for h in range(n_heads):
        ah = a[:, h * HEAD_DIM:(h + 1) * HEAD_DIM]
        ms = jnp.mean(ah * ah, axis=-1, keepdims=True)
        outs.append(ah * lax.rsqrt(ms + EPS) * g)
    return outs


def _inproj_kernel(x_ref, sh_ref, sc_ref, g1_ref, w_ref, gq_ref, gk_ref,
                   q_ref, kf_ref, kb_ref, vf_ref, vb_ref, xl_ref, gl_ref, hn_ref, *, n_heads, n_sub):
    j = pl.program_id(1)
    tm, d = hn_ref.shape
    sub = tm // n_sub
    row_slices = [slice(s * sub, (s + 1) * sub) for s in range(n_sub)]
    head_cols = [slice(h * HEAD_DIM, (h + 1) * HEAD_DIM) for h in range(n_heads)]

    def normalize(rs):
        x = x_ref[...] if n_sub == 1 else x_ref[:, rs, :]
        ms = jnp.mean(x * x, axis=-1, keepdims=True)
        y = x * lax.rsqrt(ms + EPS) * g1_ref[...]
        hn = y * (1.0 + sc_ref[...]) + sh_ref[...]
        hn_ref[rs, :] = hn.reshape(sub, d).astype(BF16)

    def project(epilogue):
        accs = [jnp.dot(hn_ref[rs, :], w_ref[...], preferred_element_type=F32) for rs in row_slices]
        for rs, acc in zip(row_slices, accs):
            epilogue(rs, acc)

    def head_rows(rs, h):
        return pl.ds(rs.start * n_heads + h, sub, stride=n_heads)

    def q_epilogue(rs, acc):
        for h, qh in enumerate(_head_rmsnorm(acc, gq_ref[...], n_heads)):
            q_ref[rs, head_cols[h]] = (qh * SB_SCALE).astype(BF16)

    def k_epilogue(rs, acc):
        for h, kh in enumerate(_head_rmsnorm(acc, gk_ref[...], n_heads)):
            kf_ref[head_rows(rs, h), :] = kh
            kb_ref[rs, head_cols[h]] = kh.astype(BF16)

    def v_epilogue(rs, acc):
        for h in range(n_heads):
            vf_ref[head_rows(rs, h), :] = acc[:, head_cols[h]]
        vb_ref[rs, :] = acc.astype(BF16)

    def store_to(ref):
        def epilogue(rs, acc):
            ref[rs, :] = acc
        return epilogue

    @pl.when(j == 0)
    def _():
        for rs in row_slices:
            normalize(rs)
        project(q_epilogue)

    for t, epilogue in enumerate([k_epilogue, v_epilogue, store_to(xl_ref), store_to(gl_ref)], start=1):
        pl.when(j == t)(functools.partial(project, epilogue))


def _inproj_call(x3, sh, sc, g1, w_in_bf, g_q, g_k, *, groups_per_block, mod_index):
    n_groups, rows, d = x3.shape
    g = groups_per_block
    tm = g * rows
    m = n_groups * rows
    n_blocks = n_groups // g
    width = w_in_bf.shape[1] // 5
    n_heads = width // HEAD_DIM
    row_spec = pl.BlockSpec((tm, width), lambda i, j: (i, 0))
    head_spec = pl.BlockSpec((tm * n_heads, HEAD_DIM), lambda i, j: (i, 0))
    mod_spec = pl.BlockSpec((g, 1, d), lambda i, j: (mod_index(i), 0, 0))
    f32_out = jax.ShapeDtypeStruct((m, width), F32)
    head_out = jax.ShapeDtypeStruct((m * n_heads, HEAD_DIM), F32)
    bf_out = jax.ShapeDtypeStruct((m, width), BF16)
    return pl.pallas_call(
        functools.partial(_inproj_kernel, n_heads=n_heads, n_sub=INPROJ_SUB_BLOCKS if g == 1 else 1),
        grid=(n_blocks, 5),
        in_specs=[pl.BlockSpec((g, rows, d), lambda i, j: (i, 0, 0)),
                  mod_spec, mod_spec,
                  pl.BlockSpec((1, d), lambda i, j: (0, 0)),
                  pl.BlockSpec((d, width), lambda i, j: (0, j)),
                  pl.BlockSpec((1, HEAD_DIM), lambda i, j: (0, 0)),
                  pl.BlockSpec((1, HEAD_DIM), lambda i, j: (0, 0))],
        out_specs=[row_spec, head_spec, row_spec, head_spec, row_spec, row_spec, row_spec],
        out_shape=[bf_out, head_out, bf_out, head_out, bf_out, f32_out, f32_out],
        scratch_shapes=[pltpu.VMEM((tm, d), BF16)],
        compiler_params=_cparams(2),
        name="inproj",
    )(x3, sh, sc, g1.reshape(1, d), w_in_bf, g_q.reshape(1, HEAD_DIM), g_k.reshape(1, HEAD_DIM))


def _sb_mass(z, suffix_mat, mask):
    sp = _softplus_logits(z)
    if mask is not None:
        sp = jnp.where(mask, sp, 0.0)
    hi, lo = _split_hi_lo(sp)
    return jnp.dot(jnp.concatenate([hi, lo], axis=1), suffix_mat, preferred_element_type=F32)


def _sb_weights(z, cums, later, mask):
    w = jnp.exp(z - cums - later)
    if mask is not None:
        w = jnp.where(mask, w, 0.0)
    return w, later + cums[:, 0:1]


def _attn_prompt_kernel(bias_ref, q_ref, k_ref, v_ref, o_ref):
    hg = pl.program_id(1)
    qi = pl.program_id(2)
    suffix_mat = _suffix_sum_matrix(ATTN_TK)
    heads = range(ATTN_HEADS)
    cols = [slice(hh * HEAD_DIM, (hh + 1) * HEAD_DIM) for hh in heads]
    biases = [bias_ref[hg * ATTN_HEADS + hh] for hh in heads]
    qs = [q_ref[:, c] for c in cols]

    def tile(j, carry, mask):
        start = pl.multiple_of(j * ATTN_TK, ATTN_TK)
        nt = (((1,), (1,)), ((), ()))
        zs = [lax.dot_general(qs[hh], k_ref[pl.ds(start, ATTN_TK), cols[hh]], nt,
                              preferred_element_type=F32) + biases[hh] for hh in heads]
        cums = [_sb_mass(zs[hh], suffix_mat, mask) for hh in heads]
        ws = [_sb_weights(zs[hh], cums[hh], carry[hh][1], mask) for hh in heads]
        return tuple(
            (carry[hh][0] + jnp.dot(ws[hh][0].astype(BF16), v_ref[pl.ds(start, ATTN_TK), cols[hh]],
                                    preferred_element_type=F32), ws[hh][1]) for hh in heads)

    row = lax.broadcasted_iota(jnp.int32, (ATTN_TQ, ATTN_TK), 0)
    col = lax.broadcasted_iota(jnp.int32, (ATTN_TQ, ATTN_TK), 1)
    carry = tuple((jnp.zeros((ATTN_TQ, HEAD_DIM), F32), jnp.zeros((ATTN_TQ, 1), F32)) for _ in heads)
    carry = tile(qi, carry, col < row)
    carry = lax.fori_loop(0, qi, lambda t, c: tile(qi - 1 - t, c, None), carry)
    for hh in heads:
        o_ref[:, cols[hh]] = carry[hh][0].astype(o_ref.dtype)


def _attn_prompt_call(q, k, v, b_sb, *, batch, seq):
    m, width = q.shape
    n_heads = width // HEAD_DIM
    nq = seq // ATTN_TQ
    gw = ATTN_HEADS * HEAD_DIM
    q_spec = pl.BlockSpec((ATTN_TQ, gw), lambda b, h, i: (b * nq + i, h))
    kv_spec = pl.BlockSpec((seq, gw), lambda b, h, i: (b, h))
    return pl.pallas_call(
        _attn_prompt_kernel,
        grid=(batch, n_heads // ATTN_HEADS, nq),
        in_specs=[pl.BlockSpec(memory_space=pltpu.SMEM), q_spec, kv_spec, kv_spec],
        out_specs=q_spec,
        out_shape=jax.ShapeDtypeStruct((m, width), BF16),
        compiler_params=_cparams(3),
        name="attn_prompt",
    )(b_sb, q, k, v)


def _attn_sample_kernel(pt_ref, bias_ref, q_ref, knew_ref, vnew_ref, *rest, n_heads, n_new):
    del pt_ref
    pages = PAGES_PER_STEP
    k_refs = rest[:pages]
    v_refs = rest[pages:2 * pages]
    o_ref = rest[2 * pages]
    kpad_ref, vpad_ref, acc_ref, later_ref = rest[2 * pages + 1:]
    g = pl.program_id(1)
    page_len = kpad_ref.shape[0] // n_heads
    suffix_mat = _suffix_sum_matrix(page_len)
    bias = jnp.concatenate([jnp.full((n_new, 1), bias_ref[h], F32) for h in range(n_heads)], axis=0)
    nt = (((1,), (1,)), ((), ()))

    def visit(k_pages, v_pages, mask):
        qh = [q_ref[0, :, h * HEAD_DIM:(h + 1) * HEAD_DIM].astype(BF16) for h in range(n_heads)]
        zs = []
        for kp in k_pages:
            z = [lax.dot_general(qh[h], kp[pl.ds(h, page_len, stride=n_heads), :].astype(BF16), nt,
                                 preferred_element_type=F32) for h in range(n_heads)]
            zs.append(jnp.concatenate(z, axis=0) + bias)
        cums = [_sb_mass(z, suffix_mat, mask) for z in zs]
        later = later_ref[...]
        ws = []
        for z, c in zip(zs, cums):
            w, later = _sb_weights(z, c, later, mask)
            ws.append(w)
        later_ref[...] = later
        acc = [acc_ref[h] for h in range(n_heads)]
        for w, vp in zip(ws, v_pages):
            for h in range(n_heads):
                wh = w[h * n_new:(h + 1) * n_new].astype(BF16)
                vh = vp[pl.ds(h, page_len, stride=n_heads), :].astype(BF16)
                acc[h] = acc[h] + jnp.dot(wh, vh, preferred_element_type=F32)
        for h in range(n_heads):
            acc_ref[h] = acc[h]

    @pl.when(g == 0)
    def _():
        acc_ref[...] = jnp.zeros_like(acc_ref)
        later_ref[...] = jnp.zeros_like(later_ref)
        kpad_ref[...] = jnp.zeros_like(kpad_ref)
        vpad_ref[...] = jnp.zeros_like(vpad_ref)
        kpad_ref[0:n_new * n_heads, :] = knew_ref[...]
        vpad_ref[0:n_new * n_heads, :] = vnew_ref[...]
        row = lax.broadcasted_iota(jnp.int32, (n_heads * n_new, page_len), 0) % n_new
        col = lax.broadcasted_iota(jnp.int32, (n_heads * n_new, page_len), 1)
        visit([kpad_ref], [vpad_ref], col < row)

    visit([r.at[0] for r in k_refs], [r.at[0] for r in v_refs], None)

    @pl.when(g == pl.num_programs(1) - 1)
    def _():
        for h in range(n_heads):
            o_ref[0, :, h * HEAD_DIM:(h + 1) * HEAD_DIM] = acc_ref[h].astype(o_ref.dtype)


def _attn_sample_call(q3, knew, vnew, cache_k, cache_v, page_table, b_sb):
    bsz, n_new, width = q3.shape
    n_heads = width // HEAD_DIM
    n_pages = page_table.shape[1]
    page_rows = cache_k.shape[1]
    pages = PAGES_PER_STEP
    steps = n_pages // pages

    def page_spec(p):
        def index(b, g, pt):
            return (pt[b, n_pages - 1 - (g * pages + p)], 0, 0)
        return pl.BlockSpec((1, page_rows, HEAD_DIM), index)

    new_spec = pl.BlockSpec((n_new * n_heads, HEAD_DIM), lambda b, g, pt: (b, 0))
    q_spec = pl.BlockSpec((1, n_new, width), lambda b, g, pt: (b, 0, 0))
    grid_spec = pltpu.PrefetchScalarGridSpec(
        num_scalar_prefetch=1,
        grid=(bsz, steps),
        in_specs=([pl.BlockSpec(memory_space=pltpu.SMEM), q_spec, new_spec, new_spec]
                  + [page_spec(p) for p in range(pages)] * 2),
        out_specs=q_spec,
        scratch_shapes=[pltpu.VMEM((page_rows, HEAD_DIM), F32),
                        pltpu.VMEM((page_rows, HEAD_DIM), F32),
                        pltpu.VMEM((n_heads, n_new, HEAD_DIM), F32),
                        pltpu.VMEM((n_heads * n_new, 1), F32)],
    )
    return pl.pallas_call(
        functools.partial(_attn_sample_kernel, n_heads=n_heads, n_new=n_new),
        grid_spec=grid_spec,
        out_shape=jax.ShapeDtypeStruct((bsz, n_new, width), F32),
        compiler_params=_cparams(2),
        name="attn_sample",
    )(page_table, b_sb, q3, knew, vnew, *([cache_k] * pages), *([cache_v] * pages))


def _segmented_linear_scan(a, b, seg_len):
    rows = lax.broadcasted_iota(jnp.int32, a.shape, 0) % seg_len
    d = 1
    while d < seg_len:
        keep = rows >= d
        a_prev = pltpu.roll(a, d, 0)
        b_prev = pltpu.roll(b, d, 0)
        b = jnp.where(keep, a * b_prev + b, b)
        a = jnp.where(keep, a * a_prev, a)
        d *= 2
    return a, b


def _lru_kernel(xl_ref, gl_ref, cw_ref, cb_ref, wa_ref, ba_ref, wx_ref, bx_ref, lam_ref,
                prev_ref, h0_ref, o_ref, hlast_ref, xp_ref, hcar_ref,
                *, groups, rows, blocks_per_seq, fresh):
    i = pl.program_id(0)
    tm = groups * rows
    width = xl_ref.shape[-1]
    first_block = (i % blocks_per_seq) == 0
    pad = SUBLANES

    @pl.when(first_block)
    def _():
        if fresh:
            xp_ref[:, 0:pad, :] = jnp.zeros((groups, pad, width), F32)
            hcar_ref[...] = jnp.zeros_like(hcar_ref)
        else:
            xp_ref[:, 0:pad, :] = prev_ref[...]
            hcar_ref[...] = h0_ref[...]

    xp_ref[:, pad:pad + rows, :] = xl_ref[...].reshape(groups, rows, width)
    cw = cw_ref[...]
    xc = cb_ref[...] + cw[0:1] * xp_ref[:, pl.ds(pad - 3, rows), :]
    for t in range(1, LRU_CONV_W):
        xc = xc + cw[t:t + 1] * xp_ref[:, pl.ds(pad - 3 + t, rows), :]
    xp_ref[:, 0:pad, :] = xp_ref[:, rows:rows + pad, :]
    xc = xc.reshape(tm, width)

    xcb = xc.astype(BF16)
    bw = width // LRU_BLOCKS
    r_parts, i_parts = [], []
    for n in range(LRU_BLOCKS):
        xb = xcb[:, n * bw:(n + 1) * bw]
        r_parts.append(jnp.dot(xb, wa_ref[n], preferred_element_type=F32))
        i_parts.append(jnp.dot(xb, wx_ref[n], preferred_element_type=F32))
    r = jax.nn.sigmoid(jnp.concatenate(r_parts, axis=1) + ba_ref[...])
    ig = jax.nn.sigmoid(jnp.concatenate(i_parts, axis=1) + bx_ref[...])
    log_a = (-LRU_C) * r * _softplus(-lam_ref[...])
    a = jnp.exp(log_a)
    one_minus_a2 = -jnp.tanh(log_a) * (a * a + 1.0)
    mult = jnp.where(one_minus_a2 > 0.0, one_minus_a2 * lax.rsqrt(one_minus_a2), 0.0)
    if fresh:
        t_row = lax.broadcasted_iota(jnp.int32, (tm, 1), 0)
        start_row = jnp.where(first_block, 0, -1)
        mult = jnp.where(t_row == start_row, 1.0, mult)
    b = mult * ig * xc

    a_cum, h = _segmented_linear_scan(a, b, SUBLANES)
    if rows == SUBLANES:
        h_in = jnp.broadcast_to(hcar_ref[...], (groups, rows, width)).reshape(tm, width)
        h = h + a_cum * h_in
    else:
        assert groups == 1
        h_prev = hcar_ref[0]
        tiles = []
        for t in range(rows // SUBLANES):
            rs = slice(t * SUBLANES, (t + 1) * SUBLANES)
            h_t = h[rs] + a_cum[rs] * h_prev
            tiles.append(h_t)
            h_prev = h_t[SUBLANES - 1:SUBLANES]
        h = jnp.concatenate(tiles, axis=0)
    h_last = h.reshape(groups, rows, width)[:, rows - 1:rows, :]
    hcar_ref[...] = h_last
    hlast_ref[...] = h_last
    o_ref[...] = (h * _gelu_tanh(gl_ref[...])).astype(o_ref.dtype)


def _lru_call(xl, gl, cw, cb, wa_bf, ba, wx_bf, bx, lam, prev_pad, h0, *,
              n_seq, groups, rows, fresh):
    m, width = xl.shape
    tm = groups * rows
    n_blocks = m // tm
    blocks_per_seq = n_blocks // (n_seq // groups)
    bw = width // LRU_BLOCKS
    row_spec = pl.BlockSpec((tm, width), lambda i: (i, 0))
    vec_spec = pl.BlockSpec((1, width), lambda i: (0, 0))
    w_spec = pl.BlockSpec((LRU_BLOCKS, bw, bw), lambda i: (0, 0, 0))
    seq_index = lambda i: (i // blocks_per_seq, 0, 0)
    return pl.pallas_call(
        functools.partial(_lru_kernel, groups=groups, rows=rows,
                          blocks_per_seq=blocks_per_seq, fresh=fresh),
        grid=(n_blocks,),
        in_specs=[row_spec, row_spec,
                  pl.BlockSpec((LRU_CONV_W, width), lambda i: (0, 0)), vec_spec,
                  w_spec, vec_spec, w_spec, vec_spec, vec_spec,
                  pl.BlockSpec((groups, SUBLANES, width), seq_index),
                  pl.BlockSpec((groups, 1, width), seq_index)],
        out_specs=[row_spec, pl.BlockSpec((groups, 1, width), seq_index)],
        out_shape=[jax.ShapeDtypeStruct((m, width), BF16),
                   jax.ShapeDtypeStruct((n_seq, 1, width), F32)],
        scratch_shapes=[pltpu.VMEM((groups, rows + SUBLANES, width), F32),
                        pltpu.VMEM((groups, 1, width), F32)],
        compiler_params=_cparams(1),
        name="rglru",
    )(xl, gl, cw, cb.reshape(1, width), wa_bf, ba.reshape(1, width), wx_bf, bx.reshape(1, width),
      lam.reshape(1, width), prev_pad, h0)


def _outproj_kernel(attn_ref, lru_ref, x_ref, gt_ref, sh_ref, sc_ref, g2_ref, w_ref, x1_ref, hn_ref):
    half = attn_ref.shape[1]
    mix = jnp.dot(attn_ref[...].astype(BF16), w_ref[0:half, :], preferred_element_type=F32)
    mix = mix + jnp.dot(lru_ref[...], w_ref[half:2 * half, :], preferred_element_type=F32)
    x1 = x_ref[...] + gt_ref[...] * mix.reshape(x_ref.shape)
    x1_ref[...] = x1
    ms = jnp.mean(x1 * x1, axis=-1, keepdims=True)
    y = x1 * lax.rsqrt(ms + EPS) * g2_ref[...]
    hn = y * (1.0 + sc_ref[...]) + sh_ref[...]
    hn_ref[...] = hn.reshape(hn_ref.shape).astype(BF16)


def _outproj_call(attn, lru, x3, gt, sh, sc, g2, w_out_bf, *, groups_per_block, mod_index):
    n_groups, rows, d = x3.shape
    g = groups_per_block
    tm = g * rows
    m = n_groups * rows
    half = attn.shape[1]
    half_spec = pl.BlockSpec((tm, half), lambda i: (i, 0))
    x_spec = pl.BlockSpec((g, rows, d), lambda i: (i, 0, 0))
    mod_spec = pl.BlockSpec((g, 1, d), lambda i: (mod_index(i), 0, 0))
    return pl.pallas_call(
        _outproj_kernel,
        grid=(n_groups // g,),
        in_specs=[half_spec, half_spec, x_spec, mod_spec, mod_spec, mod_spec,
                  pl.BlockSpec((1, d), lambda i: (0, 0)),
                  pl.BlockSpec((d, d), lambda i: (0, 0))],
        out_specs=[x_spec, pl.BlockSpec((tm, d), lambda i: (i, 0))],
        out_shape=[jax.ShapeDtypeStruct((n_groups, rows, d), F32),
                   jax.ShapeDtypeStruct((m, d), BF16)],
        compiler_params=_cparams(1),
        name="outproj",
    )(attn, lru, x3, gt, sh, sc, g2.reshape(1, d), w_out_bf)


def _ffn_kernel(hn_ref, wg_ref, wv_ref, cwg_ref, cwv_ref, cbg_ref, cbv_ref, wd_ref,
                x1_ref, gt_ref, pg_ref, pv_ref,
                y_ref, tg_ref, tv_ref, acc_ref, cg_ref, cv_ref,
                *, groups, rows, blocks_per_seq, fresh):
    i = pl.program_id(0)
    j = pl.program_id(1)
    tm = groups * rows
    tf = wg_ref.shape[1]
    first_block = (i % blocks_per_seq) == 0
    pad = SUBLANES
    if not fresh and blocks_per_seq != 1:
        raise NotImplementedError("continuing sequences must fit one row block")

    @pl.when(j == 0)
    def _():
        acc_ref[...] = jnp.zeros_like(acc_ref)

    @pl.when(jnp.logical_and(i == 0, j == 0))
    def _():
        cg_ref[...] = jnp.zeros_like(cg_ref)
        cv_ref[...] = jnp.zeros_like(cv_ref)

    n_sub = FFN_SUB_BLOCKS if groups == 1 else 1
    sub = tm // n_sub
    row_slices = [slice(s * sub, (s + 1) * sub) for s in range(n_sub)]
    ups = []
    for rs in row_slices:
        hn = hn_ref[rs, :]
        ups.append((jnp.dot(hn, wg_ref[...], preferred_element_type=F32),
                    jnp.dot(hn, wv_ref[...], preferred_element_type=F32)))

    def boundary_rows(p_ref, carry_ref):
        if fresh:
            prev = jnp.where(first_block, 0.0, carry_ref[j])
            return prev[:, pad - 2:pad - 1, :], prev[:, pad - 1:pad, :]
        return p_ref[:, 0:1, :], p_ref[:, 1:2, :]

    def conv(u, p0, p1, cw_ref, cb_ref):
        u1 = pltpu.roll(u, 1, 0)
        u2 = pltpu.roll(u, 2, 0)
        if groups == 1:
            top = lax.broadcasted_iota(jnp.int32, (pad, tf), 0)
            u1_top = jnp.where(top == 0, p1, u1[0:pad])
            u2_top = jnp.where(top == 0, p0, jnp.where(top == 1, p1, u2[0:pad]))
            u1 = jnp.concatenate([u1_top, u1[pad:]], axis=0)
            u2 = jnp.concatenate([u2_top, u2[pad:]], axis=0)
        else:
            t_row = lax.broadcasted_iota(jnp.int32, (tm, tf), 0) % rows
            p1r = jnp.broadcast_to(p1, (groups, rows, tf)).reshape(tm, tf)
            p0r = jnp.broadcast_to(p0, (groups, rows, tf)).reshape(tm, tf)
            u1 = jnp.where(t_row == 0, p1r, u1)
            u2 = jnp.where(t_row == 0, p0r, jnp.where(t_row == 1, p1r, u2))
        cw = cw_ref[...]
        uc = cb_ref[...] + cw[0:1] * u2
        uc = uc + cw[1:2] * u1
        return uc + cw[2:3] * u

    pg = boundary_rows(pg_ref, cg_ref)
    pv = boundary_rows(pv_ref, cv_ref)
    if groups == 1:
        pg = (pg[0][0], pg[1][0])
        pv = (pv[0][0], pv[1][0])
    for s, rs in enumerate(row_slices):
        ug, uv = ups[s]
        gated = (_gelu_tanh(conv(ug, pg[0], pg[1], cwg_ref, cbg_ref))
                 * conv(uv, pv[0], pv[1], cwv_ref, cbv_ref)).astype(BF16)
        acc_ref[rs, :] += jnp.dot(gated, wd_ref[...], preferred_element_type=F32)
        pg = (ug[sub - 2:sub - 1], ug[sub - 1:sub])
        pv = (uv[sub - 2:sub - 1], uv[sub - 1:sub])

    ug3 = ups[-1][0].reshape(groups, rows // n_sub, tf)
    uv3 = ups[-1][1].reshape(groups, rows // n_sub, tf)
    last = rows // n_sub
    cg_ref[j] = ug3[:, last - pad:last, :]
    cv_ref[j] = uv3[:, last - pad:last, :]
    tg_ref[...] = ug3[:, last - 2:last, :]
    tv_ref[...] = uv3[:, last - 2:last, :]

    @pl.when(j == pl.num_programs(1) - 1)
    def _():
        y_ref[...] = x1_ref[...] + gt_ref[...] * acc_ref[...].reshape(y_ref.shape)


def _ffn_call(hn2, w_up_bf, cw, cb, w_down_bf, x1, gt, prev_g, prev_v, *,
              n_seq, groups_per_block, mod_index, fresh):
    n_groups, rows, d = x1.shape
    g = groups_per_block
    tm = g * rows
    n_blocks = n_groups // g
    blocks_per_seq = n_blocks // (n_seq // g)
    d_ff = w_down_bf.shape[0]
    tf = FFN_TF
    nj = d_ff // tf
    seq_index = lambda i: i // blocks_per_seq
    x_spec = pl.BlockSpec((g, rows, d), lambda i, j: (i, 0, 0))
    prev_spec = pl.BlockSpec((g, FFN_CONV_W - 1, tf), lambda i, j: (seq_index(i), 0, j))
    tail_spec = pl.BlockSpec((g, FFN_CONV_W - 1, tf), lambda i, j: (i, 0, j))
    tail_out = jax.ShapeDtypeStruct((n_groups, FFN_CONV_W - 1, d_ff), F32)
    y, tail_g, tail_v = pl.pallas_call(
        functools.partial(_ffn_kernel, groups=g, rows=rows, blocks_per_seq=blocks_per_seq, fresh=fresh),
        grid=(n_blocks, nj),
        in_specs=[pl.BlockSpec((tm, d), lambda i, j: (i, 0)),
                  pl.BlockSpec((d, tf), lambda i, j: (0, j)),
                  pl.BlockSpec((d, tf), lambda i, j: (0, j + nj)),
                  pl.BlockSpec((FFN_CONV_W, tf), lambda i, j: (0, j)),
                  pl.BlockSpec((FFN_CONV_W, tf), lambda i, j: (0, j + nj)),
                  pl.BlockSpec((1, tf), lambda i, j: (0, j)),
                  pl.BlockSpec((1, tf), lambda i, j: (0, j + nj)),
                  pl.BlockSpec((tf, d), lambda i, j: (j, 0)),
                  x_spec,
                  pl.BlockSpec((g, 1, d), lambda i, j: (mod_index(i), 0, 0)),
                  prev_spec, prev_spec],
        out_specs=[x_spec, tail_spec, tail_spec],
        out_shape=[jax.ShapeDtypeStruct((n_groups, rows, d), F32), tail_out, tail_out],
        scratch_shapes=[pltpu.VMEM((tm, d), F32),
                        pltpu.VMEM((nj, g, SUBLANES, tf), F32),
                        pltpu.VMEM((nj, g, SUBLANES, tf), F32)],
        compiler_params=_cparams(2),
        name="convffn",
    )(hn2, w_up_bf, w_up_bf, cw, cw, cb.reshape(1, 2 * d_ff), cb.reshape(1, 2 * d_ff), w_down_bf,
      x1, gt, prev_g, prev_v)
    last = lambda t: t.reshape(n_seq, n_groups // n_seq, FFN_CONV_W - 1, d_ff)[:, -1]
    return y, last(tail_g), last(tail_v)


def _run_group(x, mod, attn_fn, lru_prev_pad, lru_h0, ffn_prev, fresh, groups_per_block, rows_per_block, p):
    n_seq, t, d = x.shape
    rows = rows_per_block // groups_per_block
    x3 = x.reshape(n_seq * t // rows, rows, d)
    blocks_per_seq = max(1, t // rows_per_block)
    if groups_per_block == 1:
        mod_index = lambda i: i // blocks_per_seq
    else:
        mod_index = lambda i: i
    mods = [mod[:, n:n + 1, :] for n in range(N_MOD)]
    sh1, sc1, gt1, sh2, sc2, gt2 = mods

    q, kf, kb, vf, vb, xl, gl = _inproj_call(
        x3, sh1, sc1, p["g_norm1"], p["w_in"], p["g_q"], p["g_k"],
        groups_per_block=groups_per_block, mod_index=mod_index)
    attn = attn_fn(q, kf, kb, vf, vb)
    lru, h_last = _lru_call(
        xl, gl, p["w_lru_conv"], p["b_lru_conv"], p["w_lru_a"], p["b_lru_a"], p["w_lru_x"], p["b_lru_x"],
        p["lru_lambda"], lru_prev_pad, lru_h0,
        n_seq=n_seq, groups=groups_per_block, rows=rows, fresh=fresh)
    x1, hn2 = _outproj_call(attn, lru, x3, gt1, sh2, sc2, p["g_norm2"], p["w_out"],
                            groups_per_block=groups_per_block, mod_index=mod_index)
    d_ff = p["w_down"].shape[0]
    y, tail_g, tail_v = _ffn_call(
        hn2, p["w_up"], p["w_ffn_conv"], p["b_ffn_conv"], p["w_down"], x1, gt2,
        ffn_prev[:, :, :d_ff], ffn_prev[:, :, d_ff:],
        n_seq=n_seq, groups_per_block=groups_per_block, mod_index=mod_index, fresh=fresh)
    width = xl.shape[1]
    lru_conv_new = xl.reshape(n_seq, t, width)[:, t - (LRU_CONV_W - 1):, :]
    ffn_conv_new = jnp.concatenate([tail_g, tail_v], axis=-1)
    return (y.reshape(n_seq, t, d), kf, vf, lru_conv_new, h_last.reshape(n_seq, width), ffn_conv_new)


def kernel(x_prompt, x_sample, c_prompt, c_sample, cache_k, cache_v, page_table, state_lru_conv, state_lru_h, state_ffn_conv, w_ada, b_ada, g_norm1, w_in, g_q, g_k, b_sb, w_lru_conv, b_lru_conv, w_lru_a, b_lru_a, w_lru_x, b_lru_x, lru_lambda, w_out, g_norm2, w_up, w_ffn_conv, b_ffn_conv, w_down):
    depth = w_ada.shape[0]
    assert depth == 1, "single-layer step"
    l = 0
    bsz, seq, d = x_prompt.shape
    dec_b, dec_t, _ = x_sample.shape
    page, n_heads = cache_k.shape[2], cache_k.shape[3]
    width = n_heads * HEAD_DIM
    lru_w = state_lru_h.shape[-1]
    d_ff2 = state_ffn_conv.shape[-1]

    p = dict(
        g_norm1=g_norm1[l], w_in=w_in[l].astype(BF16), g_q=g_q[l], g_k=g_k[l],
        w_lru_conv=w_lru_conv[l], b_lru_conv=b_lru_conv[l],
        w_lru_a=w_lru_a[l].astype(BF16), b_lru_a=b_lru_a[l],
        w_lru_x=w_lru_x[l].astype(BF16), b_lru_x=b_lru_x[l], lru_lambda=lru_lambda[l],
        w_out=w_out[l].astype(BF16), g_norm2=g_norm2[l], w_up=w_up[l].astype(BF16),
        w_ffn_conv=w_ffn_conv[l], b_ffn_conv=b_ffn_conv[l], w_down=w_down[l].astype(BF16))

    n_c = bsz + dec_b
    pad_c = (-n_c) % SUBLANES
    c_all = jnp.concatenate([c_prompt, c_sample, jnp.zeros((pad_c, d), F32)], axis=0)
    mod = _ada_call(c_all, w_ada[l], b_ada[l])
    mod_p = mod[:bsz].reshape(bsz, N_MOD, d)
    mod_s = mod[bsz:n_c].reshape(dec_b, N_MOD, d)

    def prompt_attn(q, kf, kb, vf, vb):
        return _attn_prompt_call(q, kb, vb, b_sb[l], batch=bsz, seq=seq)

    zeros_prev = jnp.zeros((bsz, SUBLANES, lru_w), F32)
    zeros_h = jnp.zeros((bsz, 1, lru_w), F32)
    zeros_ffn = jnp.zeros((bsz, FFN_CONV_W - 1, d_ff2), F32)
    yp, kp, vp, lcp, lhp, fcp = _run_group(
        x_prompt, mod_p, prompt_attn, zeros_prev, zeros_h, zeros_ffn, True, 1, ROW_TILE, p)

    def sample_attn(q, kf, kb, vf, vb):
        q3 = q.astype(F32).reshape(dec_b, dec_t, width)
        pool = cache_k.shape[1]
        ck = cache_k[l].reshape(pool, page * n_heads, HEAD_DIM)
        cv = cache_v[l].reshape(pool, page * n_heads, HEAD_DIM)
        o = _attn_sample_call(q3, kf, vf, ck, cv, page_table, b_sb[l])
        return o.reshape(dec_b * dec_t, width)

    prev_pad = jnp.concatenate(
        [jnp.zeros((dec_b, SUBLANES - (LRU_CONV_W - 1), lru_w), F32), state_lru_conv[l]], axis=1)
    ys, ks, vs, lcs, lhs, fcs = _run_group(
        x_sample, mod_s, sample_attn, prev_pad, state_lru_h[l].reshape(dec_b, 1, lru_w),
        state_ffn_conv[l], False, dec_b, dec_b * dec_t, p)

    n_pages_p = seq // page
    return (yp, ys,
            kp.reshape(1, bsz, n_pages_p, page, n_heads, HEAD_DIM),
            vp.reshape(1, bsz, n_pages_p, page, n_heads, HEAD_DIM),
            ks.reshape(1, dec_b, dec_t, n_heads, HEAD_DIM),
            vs.reshape(1, dec_b, dec_t, n_heads, HEAD_DIM),
            lcp[None], lcs[None], lhp[None], lhs[None], fcp[None], fcs[None])
```

```python
import functools
import math

import jax
import jax.numpy as jnp
from jax import lax
from jax.experimental import pallas as pl
from jax.experimental.pallas import tpu as pltpu

F32 = jnp.float32
BF16 = jnp.bfloat16

EPS = 1e-6
LRU_C = 8.0
HEAD_DIM = 128
LRU_BLOCKS = 8
LRU_CONV_W = 4
FFN_CONV_W = 3
N_MOD = 6
SB_SCALE = 1.0 / math.sqrt(HEAD_DIM)

SUBLANES = 8
VMEM_LIMIT = 56 * 1024 * 1024

ROW_TILE = 512
INPROJ_SUB_BLOCKS = 2
MIXER_SUB_BLOCKS = 2
ATTN_TQ = 256
ATTN_TK = 256
ATTN_HEADS = 4
ADA_TN = 1024
FFN_TF = 512
FFN_SUB_BLOCKS = 2
PAGES_PER_STEP = 8


def _cparams(n_axes):
    return pltpu.CompilerParams(dimension_semantics=("arbitrary",) * n_axes,
                                vmem_limit_bytes=VMEM_LIMIT)


def _softplus(z):
    return jnp.maximum(z, 0.0) + jnp.log1p(jnp.exp(-jnp.abs(z)))


def _softplus_logits(z):
    return jnp.maximum(z, 0.0) + jnp.log(1.0 + jnp.exp(-jnp.abs(z)))


def _gelu_tanh(x):
    c = math.sqrt(2.0 / math.pi)
    return x * (0.5 * (1.0 + jnp.tanh(c * (x + 0.044715 * (x * x * x)))))


def _split_hi_lo(x):
    hi = x.astype(BF16)
    lo = (x - hi.astype(F32)).astype(BF16)
    return hi, lo


def _suffix_sum_matrix(n_keys):
    j = lax.broadcasted_iota(jnp.int32, (2 * n_keys, n_keys), 0) % n_keys
    s = lax.broadcasted_iota(jnp.int32, (2 * n_keys, n_keys), 1)
    return jnp.where(j >= s, 1.0, 0.0).astype(BF16)


def _ada_kernel(c_ref, w_ref, b_ref, o_ref):
    c = c_ref[...]
    s = (c * jax.nn.sigmoid(c)).astype(BF16)
    o_ref[...] = jnp.dot(s, w_ref[...].astype(BF16), preferred_element_type=F32) + b_ref[...]


def _ada_call(c_all, w_ada, b_ada):
    m, d = c_all.shape
    n = w_ada.shape[1]
    return pl.pallas_call(
        _ada_kernel,
        grid=(n // ADA_TN,),
        in_specs=[pl.BlockSpec((m, d), lambda j: (0, 0)),
                  pl.BlockSpec((d, ADA_TN), lambda j: (0, j)),
                  pl.BlockSpec((1, ADA_TN), lambda j: (0, j))],
        out_specs=pl.BlockSpec((m, ADA_TN), lambda j: (0, j)),
        out_shape=jax.ShapeDtypeStruct((m, n), F32),
        compiler_params=_cparams(1),
        name="adaln",
    )(c_all, w_ada, b_ada.reshape(1, n))


def _head_rmsnorm(a, g, n_heads):
    outs = []
    for h in range(n_heads):
        ah = a[:, h * HEAD_DIM:(h + 1) * HEAD_DIM]
        ms = jnp.mean(ah * ah, axis=-1, keepdims=True)
        outs.append(ah * lax.rsqrt(ms + EPS) * g)
    return outs


def _inproj_kernel(x_ref, sh_ref, sc_ref, g1_ref, w_ref, gq_ref, gk_ref,
                   q_ref, kf_ref, kb_ref, vf_ref, vb_ref, hn_ref, *, n_heads, n_sub):
    j = pl.program_id(1)
    tm, d = hn_ref.shape
    sub = tm // n_sub
    row_slices = [slice(s * sub, (s + 1) * sub) for s in range(n_sub)]
    head_cols = [slice(h * HEAD_DIM, (h + 1) * HEAD_DIM) for h in range(n_heads)]

    def normalize(rs):
        x = x_ref[...] if n_sub == 1 else x_ref[:, rs, :]
        ms = jnp.mean(x * x, axis=-1, keepdims=True)
        y = x * lax.rsqrt(ms + EPS) * g1_ref[...]
        hn = y * (1.0 + sc_ref[...]) + sh_ref[...]
        hn_ref[rs, :] = hn.reshape(sub, d).astype(BF16)

    def project(epilogue):
        accs = [jnp.dot(hn_ref[rs, :], w_ref[...], preferred_element_type=F32) for rs in row_slices]
        for rs, acc in zip(row_slices, accs):
            epilogue(rs, acc)

    def head_rows(rs, h):
        return pl.ds(rs.start * n_heads + h, sub, stride=n_heads)

    def q_epilogue(rs, acc):
        for h, qh in enumerate(_head_rmsnorm(acc, gq_ref[...], n_heads)):
            q_ref[rs, head_cols[h]] = (qh * SB_SCALE).astype(BF16)

    def k_epilogue(rs, acc):
        for h, kh in enumerate(_head_rmsnorm(acc, gk_ref[...], n_heads)):
            kf_ref[head_rows(rs, h), :] = kh
            kb_ref[rs, head_cols[h]] = kh.astype(BF16)

    def v_epilogue(rs, acc):
        for h in range(n_heads):
            vf_ref[head_rows(rs, h), :] = acc[:, head_cols[h]]
        vb_ref[rs, :] = acc.astype(BF16)

    @pl.when(j == 0)
    def _():
        for rs in row_slices:
            normalize(rs)
        project(q_epilogue)

    for t, epilogue in enumerate([k_epilogue, v_epilogue], start=1):
        pl.when(j == t)(functools.partial(project, epilogue))


def _inproj_call(x3, sh, sc, g1, w_qkv_bf, g_q, g_k, *, groups_per_block, mod_index):
    n_groups, rows, d = x3.shape
    g = groups_per_block
    tm = g * rows
    m = n_groups * rows
    n_blocks = n_groups // g
    width = w_qkv_bf.shape[1] // 3
    n_heads = width // HEAD_DIM
    row_spec = pl.BlockSpec((tm, width), lambda i, j: (i, 0))
    head_spec = pl.BlockSpec((tm * n_heads, HEAD_DIM), lambda i, j: (i, 0))
    mod_spec = pl.BlockSpec((g, 1, d), lambda i, j: (mod_index(i), 0, 0))
    head_out = jax.ShapeDtypeStruct((m * n_heads, HEAD_DIM), F32)
    bf_out = jax.ShapeDtypeStruct((m, width), BF16)
    return pl.pallas_call(
        functools.partial(_inproj_kernel, n_heads=n_heads, n_sub=INPROJ_SUB_BLOCKS if g == 1 else 1),
        grid=(n_blocks, 3),
        in_specs=[pl.BlockSpec((g, rows, d), lambda i, j: (i, 0, 0)),
                  mod_spec, mod_spec,
                  pl.BlockSpec((1, d), lambda i, j: (0, 0)),
                  pl.BlockSpec((d, width), lambda i, j: (0, j)),
                  pl.BlockSpec((1, HEAD_DIM), lambda i, j: (0, 0)),
                  pl.BlockSpec((1, HEAD_DIM), lambda i, j: (0, 0))],
        out_specs=[row_spec, head_spec, row_spec, head_spec, row_spec,
                   pl.BlockSpec((tm, d), lambda i, j: (i, 0))],
        out_shape=[bf_out, head_out, bf_out, head_out, bf_out, jax.ShapeDtypeStruct((m, d), BF16)],
        compiler_params=_cparams(2),
        name="inproj",
    )(x3, sh, sc, g1.reshape(1, d), w_qkv_bf, g_q.reshape(1, HEAD_DIM), g_k.reshape(1, HEAD_DIM))


def _sb_mass(z, suffix_mat, mask):
    sp = _softplus_logits(z)
    if mask is not None:
        sp = jnp.where(mask, sp, 0.0)
    hi, lo = _split_hi_lo(sp)
    return jnp.dot(jnp.concatenate([hi, lo], axis=1), suffix_mat, preferred_element_type=F32)


def _sb_weights(z, cums, later, mask):
    w = jnp.exp(z - cums - later)
    if mask is not None:
        w = jnp.where(mask, w, 0.0)
    return w, later + cums[:, 0:1]


def _attn_prompt_kernel(bias_ref, q_ref, k_ref, v_ref, o_ref):
    hg = pl.program_id(1)
    qi = pl.program_id(2)
    suffix_mat = _suffix_sum_matrix(ATTN_TK)
    heads = range(ATTN_HEADS)
    cols = [slice(hh * HEAD_DIM, (hh + 1) * HEAD_DIM) for hh in heads]
    biases = [bias_ref[hg * ATTN_HEADS + hh] for hh in heads]
    qs = [q_ref[:, c] for c in cols]

    def tile(j, carry, mask):
        start = pl.multiple_of(j * ATTN_TK, ATTN_TK)
        nt = (((1,), (1,)), ((), ()))
        zs = [lax.dot_general(qs[hh], k_ref[pl.ds(start, ATTN_TK), cols[hh]], nt,
                              preferred_element_type=F32) + biases[hh] for hh in heads]
        cums = [_sb_mass(zs[hh], suffix_mat, mask) for hh in heads]
        ws = [_sb_weights(zs[hh], cums[hh], carry[hh][1], mask) for hh in heads]
        return tuple(
            (carry[hh][0] + jnp.dot(ws[hh][0].astype(BF16), v_ref[pl.ds(start, ATTN_TK), cols[hh]],
                                    preferred_element_type=F32), ws[hh][1]) for hh in heads)

    row = lax.broadcasted_iota(jnp.int32, (ATTN_TQ, ATTN_TK), 0)
    col = lax.broadcasted_iota(jnp.int32, (ATTN_TQ, ATTN_TK), 1)
    carry = tuple((jnp.zeros((ATTN_TQ, HEAD_DIM), F32), jnp.zeros((ATTN_TQ, 1), F32)) for _ in heads)
    carry = tile(qi, carry, col < row)
    carry = lax.fori_loop(0, qi, lambda t, c: tile(qi - 1 - t, c, None), carry)
    for hh in heads:
        o_ref[:, cols[hh]] = carry[hh][0].astype(o_ref.dtype)


def _attn_prompt_call(q, k, v, b_sb, *, batch, seq):
    m, width = q.shape
    n_heads = width // HEAD_DIM
    nq = seq // ATTN_TQ
    gw = ATTN_HEADS * HEAD_DIM
    q_spec = pl.BlockSpec((ATTN_TQ, gw), lambda b, h, i: (b * nq + i, h))
    kv_spec = pl.BlockSpec((seq, gw), lambda b, h, i: (b, h))
    return pl.pallas_call(
        _attn_prompt_kernel,
        grid=(batch, n_heads // ATTN_HEADS, nq),
        in_specs=[pl.BlockSpec(memory_space=pltpu.SMEM), q_spec, kv_spec, kv_spec],
        out_specs=q_spec,
        out_shape=jax.ShapeDtypeStruct((m, width), BF16),
        compiler_params=_cparams(3),
        name="attn_prompt",
    )(b_sb, q, k, v)


def _attn_sample_kernel(pt_ref, bias_ref, q_ref, knew_ref, vnew_ref, *rest, n_heads, n_new):
    del pt_ref
    pages = PAGES_PER_STEP
    k_refs = rest[:pages]
    v_refs = rest[pages:2 * pages]
    o_ref = rest[2 * pages]
    kpad_ref, vpad_ref, acc_ref, later_ref = rest[2 * pages + 1:]
    g = pl.program_id(1)
    page_len = kpad_ref.shape[0] // n_heads
    suffix_mat = _suffix_sum_matrix(page_len)
    bias = jnp.concatenate([jnp.full((n_new, 1), bias_ref[h], F32) for h in range(n_heads)], axis=0)
    nt = (((1,), (1,)), ((), ()))

    def visit(k_pages, v_pages, mask):
        qh = [q_ref[0, :, h * HEAD_DIM:(h + 1) * HEAD_DIM].astype(BF16) for h in range(n_heads)]
        zs = []
        for kp in k_pages:
            z = [lax.dot_general(qh[h], kp[pl.ds(h, page_len, stride=n_heads), :].astype(BF16), nt,
                                 preferred_element_type=F32) for h in range(n_heads)]
            zs.append(jnp.concatenate(z, axis=0) + bias)
        cums = [_sb_mass(z, suffix_mat, mask) for z in zs]
        later = later_ref[...]
        ws = []
        for z, c in zip(zs, cums):
            w, later = _sb_weights(z, c, later, mask)
            ws.append(w)
        later_ref[...] = later
        acc = [acc_ref[h] for h in range(n_heads)]
        for w, vp in zip(ws, v_pages):
            for h in range(n_heads):
                wh = w[h * n_new:(h + 1) * n_new].astype(BF16)
                vh = vp[pl.ds(h, page_len, stride=n_heads), :].astype(BF16)
                acc[h] = acc[h] + jnp.dot(wh, vh, preferred_element_type=F32)
        for h in range(n_heads):
            acc_ref[h] = acc[h]

    @pl.when(g == 0)
    def _():
        acc_ref[...] = jnp.zeros_like(acc_ref)
        later_ref[...] = jnp.zeros_like(later_ref)
        kpad_ref[...] = jnp.zeros_like(kpad_ref)
        vpad_ref[...] = jnp.zeros_like(vpad_ref)
        kpad_ref[0:n_new * n_heads, :] = knew_ref[...]
        vpad_ref[0:n_new * n_heads, :] = vnew_ref[...]
        row = lax.broadcasted_iota(jnp.int32, (n_heads * n_new, page_len), 0) % n_new
        col = lax.broadcasted_iota(jnp.int32, (n_heads * n_new, page_len), 1)
        visit([kpad_ref], [vpad_ref], col < row)

    visit([r.at[0] for r in k_refs], [r.at[0] for r in v_refs], None)

    @pl.when(g == pl.num_programs(1) - 1)
    def _():
        for h in range(n_heads):
            o_ref[0, :, h * HEAD_DIM:(h + 1) * HEAD_DIM] = acc_ref[h].astype(o_ref.dtype)


def _attn_sample_call(q3, knew, vnew, cache_k, cache_v, page_table, b_sb):
    bsz, n_new, width = q3.shape
    n_heads = width // HEAD_DIM
    n_pages = page_table.shape[1]
    page_rows = cache_k.shape[1]
    pages = PAGES_PER_STEP
    steps = n_pages // pages

    def page_spec(p):
        def index(b, g, pt):
            return (pt[b, n_pages - 1 - (g * pages + p)], 0, 0)
        return pl.BlockSpec((1, page_rows, HEAD_DIM), index)

    new_spec = pl.BlockSpec((n_new * n_heads, HEAD_DIM), lambda b, g, pt: (b, 0))
    q_spec = pl.BlockSpec((1, n_new, width), lambda b, g, pt: (b, 0, 0))
    grid_spec = pltpu.PrefetchScalarGridSpec(
        num_scalar_prefetch=1,
        grid=(bsz, steps),
        in_specs=([pl.BlockSpec(memory_space=pltpu.SMEM), q_spec, new_spec, new_spec]
                  + [page_spec(p) for p in range(pages)] * 2),
        out_specs=q_spec,
        scratch_shapes=[pltpu.VMEM((page_rows, HEAD_DIM), F32),
                        pltpu.VMEM((page_rows, HEAD_DIM), F32),
                        pltpu.VMEM((n_heads, n_new, HEAD_DIM), F32),
                        pltpu.VMEM((n_heads * n_new, 1), F32)],
    )
    return pl.pallas_call(
        functools.partial(_attn_sample_kernel, n_heads=n_heads, n_new=n_new),
        grid_spec=grid_spec,
        out_shape=jax.ShapeDtypeStruct((bsz, n_new, width), F32),
        compiler_params=_cparams(2),
        name="attn_sample",
    )(page_table, b_sb, q3, knew, vnew, *([cache_k] * pages), *([cache_v] * pages))


def _segmented_linear_scan(a, b, seg_len):
    rows = lax.broadcasted_iota(jnp.int32, a.shape, 0) % seg_len
    d = 1
    while d < seg_len:
        keep = rows >= d
        a_prev = pltpu.roll(a, d, 0)
        b_prev = pltpu.roll(b, d, 0)
        b = jnp.where(keep, a * b_prev + b, b)
        a = jnp.where(keep, a * a_prev, a)
        d *= 2
    return a, b


def _mixer_kernel(hn_ref, attn_ref, x_ref, gt_ref, sh_ref, sc_ref, g2_ref,
                  wl_ref, cw_ref, cb_ref, wa_ref, ba_ref, wx_ref, bx_ref, lam_ref, prev_ref, h0_ref, wo_ref,
                  x1_ref, hn2_ref, hlast_ref, xtail_ref, xp_ref, hcar_ref,
                  *, groups, rows, blocks_per_seq, fresh, n_sub):
    i = pl.program_id(0)
    width = xp_ref.shape[-1]
    d = x_ref.shape[-1]
    first_block = (i % blocks_per_seq) == 0
    pad = SUBLANES
    sub_rows = rows // n_sub
    sub = groups * sub_rows
    row_slices = [slice(s * sub, (s + 1) * sub) for s in range(n_sub)]

    @pl.when(first_block)
    def _():
        if fresh:
            xp_ref[:, 0:pad, :] = jnp.zeros((groups, pad, width), F32)
            hcar_ref[...] = jnp.zeros_like(hcar_ref)
        else:
            xp_ref[:, 0:pad, :] = prev_ref[...]
            hcar_ref[...] = h0_ref[...]

    xgs = [jnp.dot(hn_ref[rs, :], wl_ref[...], preferred_element_type=F32) for rs in row_slices]
    mixes = [jnp.dot(attn_ref[rs, :].astype(BF16), wo_ref[0:width, :], preferred_element_type=F32)
             for rs in row_slices]
    for s in range(n_sub):
        xp_ref[:, pad + s * sub_rows:pad + (s + 1) * sub_rows, :] = (
            xgs[s][:, 0:width].reshape(groups, sub_rows, width))

    cw = cw_ref[...]
    bw = width // LRU_BLOCKS
    neg_c_softplus = (-LRU_C) * _softplus(-lam_ref[...])
    h_prev = hcar_ref[...] if rows == SUBLANES else hcar_ref[0]
    for s, rs in enumerate(row_slices):
        t0 = s * sub_rows
        xc = cb_ref[...] + cw[0:1] * xp_ref[:, pl.ds(pad - 3 + t0, sub_rows), :]
        for t in range(1, LRU_CONV_W):
            xc = xc + cw[t:t + 1] * xp_ref[:, pl.ds(pad - 3 + t0 + t, sub_rows), :]
        xc = xc.reshape(sub, width)

        xcb = xc.astype(BF16)
        r_parts, i_parts = [], []
        for n in range(LRU_BLOCKS):
            xb = xcb[:, n * bw:(n + 1) * bw]
            r_parts.append(jnp.dot(xb, wa_ref[n], preferred_element_type=F32))
            i_parts.append(jnp.dot(xb, wx_ref[n], preferred_element_type=F32))
        r = jax.nn.sigmoid(jnp.concatenate(r_parts, axis=1) + ba_ref[...])
        ig = jax.nn.sigmoid(jnp.concatenate(i_parts, axis=1) + bx_ref[...])
        log_a = r * neg_c_softplus
        a = jnp.exp(log_a)
        one_minus_a2 = -jnp.tanh(log_a) * (a * a + 1.0)
        mult = jnp.where(one_minus_a2 > 0.0, one_minus_a2 * lax.rsqrt(one_minus_a2), 0.0)
        if fresh and s == 0:
            t_row = lax.broadcasted_iota(jnp.int32, (sub, 1), 0)
            start_row = jnp.where(first_block, 0, -1)
            mult = jnp.where(t_row == start_row, 1.0, mult)
        b = mult * ig * xc

        a_cum, h = _segmented_linear_scan(a, b, SUBLANES)
        if rows == SUBLANES:
            h = h + a_cum * jnp.broadcast_to(h_prev, (groups, rows, width)).reshape(sub, width)
            h_prev = h.reshape(groups, rows, width)[:, rows - 1:rows, :]
        else:
            tiles = []
            for t in range(sub_rows // SUBLANES):
                ts = slice(t * SUBLANES, (t + 1) * SUBLANES)
                h_t = h[ts] + a_cum[ts] * h_prev
                tiles.append(h_t)
                h_prev = h_t[SUBLANES - 1:SUBLANES]
            h = jnp.concatenate(tiles, axis=0)
        lru = (h * _gelu_tanh(xgs[s][:, width:2 * width])).astype(BF16)

        mix = mixes[s] + jnp.dot(lru, wo_ref[width:2 * width, :], preferred_element_type=F32)
        x1 = x_ref[:, t0:t0 + sub_rows, :] + gt_ref[...] * mix.reshape(groups, sub_rows, d)
        x1_ref[:, t0:t0 + sub_rows, :] = x1
        ms = jnp.mean(x1 * x1, axis=-1, keepdims=True)
        y = x1 * lax.rsqrt(ms + EPS) * g2_ref[...]
        hn2 = y * (1.0 + sc_ref[...]) + sh_ref[...]
        hn2_ref[rs, :] = hn2.reshape(sub, d).astype(BF16)

    h_last = h_prev if rows == SUBLANES else h_prev[None]
    hcar_ref[...] = h_last
    hlast_ref[...] = h_last
    tail = xp_ref[:, rows:rows + pad, :]
    xtail_ref[...] = tail
    xp_ref[:, 0:pad, :] = tail


def _mixer_call(hn, attn, x3, gt, sh, sc, g2, w_lru_bf, cw, cb, wa_bf, ba, wx_bf, bx, lam, prev_pad, h0,
                w_out_bf, *, n_seq, groups_per_block, mod_index, fresh):
    n_groups, rows, d = x3.shape
    g = groups_per_block
    tm = g * rows
    m = n_groups * rows
    n_blocks = n_groups // g
    blocks_per_seq = n_blocks // (n_seq // g)
    width = attn.shape[1]
    bw = width // LRU_BLOCKS
    resident = pl.Buffered(1)
    x_spec = pl.BlockSpec((g, rows, d), lambda i: (i, 0, 0))
    mod_spec = pl.BlockSpec((g, 1, d), lambda i: (mod_index(i), 0, 0))
    vec_spec = pl.BlockSpec((1, width), lambda i: (0, 0))
    gate_spec = pl.BlockSpec((LRU_BLOCKS, bw, bw), lambda i: (0, 0, 0))
    seq_index = lambda i: (i // blocks_per_seq, 0, 0)
    return pl.pallas_call(
        functools.partial(_mixer_kernel, groups=g, rows=rows, blocks_per_seq=blocks_per_seq, fresh=fresh,
                          n_sub=MIXER_SUB_BLOCKS if g == 1 else 1),
        grid=(n_blocks,),
        in_specs=[pl.BlockSpec((tm, d), lambda i: (i, 0)),
                  pl.BlockSpec((tm, width), lambda i: (i, 0)),
                  x_spec, mod_spec, mod_spec, mod_spec,
                  pl.BlockSpec((1, d), lambda i: (0, 0)),
                  pl.BlockSpec((d, 2 * width), lambda i: (0, 0), pipeline_mode=resident),
                  pl.BlockSpec((LRU_CONV_W, width), lambda i: (0, 0)), vec_spec,
                  gate_spec, vec_spec, gate_spec, vec_spec, vec_spec,
                  pl.BlockSpec((g, SUBLANES, width), seq_index),
                  pl.BlockSpec((g, 1, width), seq_index),
                  pl.BlockSpec((d, d), lambda i: (0, 0), pipeline_mode=resident)],
        out_specs=[x_spec,
                   pl.BlockSpec((tm, d), lambda i: (i, 0)),
                   pl.BlockSpec((g, 1, width), seq_index),
                   pl.BlockSpec((g, SUBLANES, width), lambda i: (i, 0, 0))],
        out_shape=[jax.ShapeDtypeStruct((n_groups, rows, d), F32),
                   jax.ShapeDtypeStruct((m, d), BF16),
                   jax.ShapeDtypeStruct((n_seq, 1, width), F32),
                   jax.ShapeDtypeStruct((n_groups, SUBLANES, width), F32)],
        scratch_shapes=[pltpu.VMEM((g, rows + SUBLANES, width), F32),
                        pltpu.VMEM((g, 1, width), F32)],
        compiler_params=_cparams(1),
        name="mixer",
    )(hn, attn, x3, gt, sh, sc, g2.reshape(1, d), w_lru_bf, cw, cb.reshape(1, width), wa_bf,
      ba.reshape(1, width), wx_bf, bx.reshape(1, width), lam.reshape(1, width), prev_pad, h0, w_out_bf)


def _ffn_kernel(hn_ref, wg_ref, wv_ref, cwg_ref, cwv_ref, cbg_ref, cbv_ref, wd_ref,
                x1_ref, gt_ref, pg_ref, pv_ref,
                y_ref, tg_ref, tv_ref, acc_ref, cg_ref, cv_ref,
                *, groups, rows, blocks_per_seq, fresh):
    i = pl.program_id(0)
    j = pl.program_id(1)
    tm = groups * rows
    tf = wg_ref.shape[1]
    first_block = (i % blocks_per_seq) == 0
    pad = SUBLANES
    if not fresh and blocks_per_seq != 1:
        raise NotImplementedError("continuing sequences must fit one row block")

    @pl.when(j == 0)
    def _():
        acc_ref[...] = jnp.zeros_like(acc_ref)

    @pl.when(jnp.logical_and(i == 0, j == 0))
    def _():
        cg_ref[...] = jnp.zeros_like(cg_ref)
        cv_ref[...] = jnp.zeros_like(cv_ref)

    n_sub = FFN_SUB_BLOCKS if groups == 1 else 1
    sub = tm // n_sub
    row_slices = [slice(s * sub, (s + 1) * sub) for s in range(n_sub)]
    ups = []
    for rs in row_slices:
        hn = hn_ref[rs, :]
        ups.append((jnp.dot(hn, wg_ref[...], preferred_element_type=F32),
                    jnp.dot(hn, wv_ref[...], preferred_element_type=F32)))

    def boundary_rows(p_ref, carry_ref):
        if fresh:
            prev = jnp.where(first_block, 0.0, carry_ref[j])
            return prev[:, pad - 2:pad - 1, :], prev[:, pad - 1:pad, :]
        return p_ref[:, 0:1, :], p_ref[:, 1:2, :]

    def conv(u, p0, p1, cw_ref, cb_ref):
        u1 = pltpu.roll(u, 1, 0)
        u2 = pltpu.roll(u, 2, 0)
        if groups == 1:
            top = lax.broadcasted_iota(jnp.int32, (pad, tf), 0)
            u1_top = jnp.where(top == 0, p1, u1[0:pad])
            u2_top = jnp.where(top == 0, p0, jnp.where(top == 1, p1, u2[0:pad]))
            u1 = jnp.concatenate([u1_top, u1[pad:]], axis=0)
            u2 = jnp.concatenate([u2_top, u2[pad:]], axis=0)
        else:
            t_row = lax.broadcasted_iota(jnp.int32, (tm, tf), 0) % rows
            p1r = jnp.broadcast_to(p1, (groups, rows, tf)).reshape(tm, tf)
            p0r = jnp.broadcast_to(p0, (groups, rows, tf)).reshape(tm, tf)
            u1 = jnp.where(t_row == 0, p1r, u1)
            u2 = jnp.where(t_row == 0, p0r, jnp.where(t_row == 1, p1r, u2))
        cw = cw_ref[...]
        uc = cb_ref[...] + cw[0:1] * u2
        uc = uc + cw[1:2] * u1
        return uc + cw[2:3] * u

    pg = boundary_rows(pg_ref, cg_ref)
    pv = boundary_rows(pv_ref, cv_ref)
    if groups == 1:
        pg = (pg[0][0], pg[1][0])
        pv = (pv[0][0], pv[1][0])
    for s, rs in enumerate(row_slices):
        ug, uv = ups[s]
        gated = (_gelu_tanh(conv(ug, pg[0], pg[1], cwg_ref, cbg_ref))
                 * conv(uv, pv[0], pv[1], cwv_ref, cbv_ref)).astype(BF16)
        acc_ref[rs, :] += jnp.dot(gated, wd_ref[...], preferred_element_type=F32)
        pg = (ug[sub - 2:sub - 1], ug[sub - 1:sub])
        pv = (uv[sub - 2:sub - 1], uv[sub - 1:sub])

    ug3 = ups[-1][0].reshape(groups, rows // n_sub, tf)
    uv3 = ups[-1][1].reshape(groups, rows // n_sub, tf)
    last = rows // n_sub
    cg_ref[j] = ug3[:, last - pad:last, :]
    cv_ref[j] = uv3[:, last - pad:last, :]
    tg_ref[...] = ug3[:, last - 2:last, :]
    tv_ref[...] = uv3[:, last - 2:last, :]

    @pl.when(j == pl.num_programs(1) - 1)
    def _():
        y_ref[...] = x1_ref[...] + gt_ref[...] * acc_ref[...].reshape(y_ref.shape)


def _ffn_call(hn2, w_up_bf, cw, cb, w_down_bf, x1, gt, prev_g, prev_v, *,
              n_seq, groups_per_block, mod_index, fresh):
    n_groups, rows, d = x1.shape
    g = groups_per_block
    tm = g * rows
    n_blocks = n_groups // g
    blocks_per_seq = n_blocks // (n_seq // g)
    d_ff = w_down_bf.shape[0]
    tf = FFN_TF
    nj = d_ff // tf
    seq_index = lambda i: i // blocks_per_seq
    x_spec = pl.BlockSpec((g, rows, d), lambda i, j: (i, 0, 0))
    prev_spec = pl.BlockSpec((g, FFN_CONV_W - 1, tf), lambda i, j: (seq_index(i), 0, j))
    tail_spec = pl.BlockSpec((g, FFN_CONV_W - 1, tf), lambda i, j: (i, 0, j))
    tail_out = jax.ShapeDtypeStruct((n_groups, FFN_CONV_W - 1, d_ff), F32)
    y, tail_g, tail_v = pl.pallas_call(
        functools.partial(_ffn_kernel, groups=g, rows=rows, blocks_per_seq=blocks_per_seq, fresh=fresh),
        grid=(n_blocks, nj),
        in_specs=[pl.BlockSpec((tm, d), lambda i, j: (i, 0)),
                  pl.BlockSpec((d, tf), lambda i, j: (0, j)),
                  pl.BlockSpec((d, tf), lambda i, j: (0, j + nj)),
                  pl.BlockSpec((FFN_CONV_W, tf), lambda i, j: (0, j)),
                  pl.BlockSpec((FFN_CONV_W, tf), lambda i, j: (0, j + nj)),
                  pl.BlockSpec((1, tf), lambda i, j: (0, j)),
                  pl.BlockSpec((1, tf), lambda i, j: (0, j + nj)),
                  pl.BlockSpec((tf, d), lambda i, j: (j, 0)),
                  x_spec,
                  pl.BlockSpec((g, 1, d), lambda i, j: (mod_index(i), 0, 0)),
                  prev_spec, prev_spec],
        out_specs=[x_spec, tail_spec, tail_spec],
        out_shape=[jax.ShapeDtypeStruct((n_groups, rows, d), F32), tail_out, tail_out],
        scratch_shapes=[pltpu.VMEM((tm, d), F32),
                        pltpu.VMEM((nj, g, SUBLANES, tf), F32),
                        pltpu.VMEM((nj, g, SUBLANES, tf), F32)],
        compiler_params=_cparams(2),
        name="convffn",
    )(hn2, w_up_bf, w_up_bf, cw, cw, cb.reshape(1, 2 * d_ff), cb.reshape(1, 2 * d_ff), w_down_bf,
      x1, gt, prev_g, prev_v)
    last = lambda t: t.reshape(n_seq, n_groups // n_seq, FFN_CONV_W - 1, d_ff)[:, -1]
    return y, last(tail_g), last(tail_v)


def _run_group(x, mod, attn_fn, lru_prev_pad, lru_h0, ffn_prev, fresh, groups_per_block, rows_per_block, p):
    n_seq, t, d = x.shape
    rows = rows_per_block // groups_per_block
    x3 = x.reshape(n_seq * t // rows, rows, d)
    blocks_per_seq = max(1, t // rows_per_block)
    if groups_per_block == 1:
        mod_index = lambda i: i // blocks_per_seq
    else:
        mod_index = lambda i: i
    mods = [mod[:, n:n + 1, :] for n in range(N_MOD)]
    sh1, sc1, gt1, sh2, sc2, gt2 = mods

    q, kf, kb, vf, vb, hn = _inproj_call(
        x3, sh1, sc1, p["g_norm1"], p["w_qkv"], p["g_q"], p["g_k"],
        groups_per_block=groups_per_block, mod_index=mod_index)
    attn = attn_fn(q, kf, kb, vf, vb)
    x1, hn2, h_last, xl_tail = _mixer_call(
        hn, attn, x3, gt1, sh2, sc2, p["g_norm2"], p["w_lru_in"], p["w_lru_conv"], p["b_lru_conv"],
        p["w_lru_a"], p["b_lru_a"], p["w_lru_x"], p["b_lru_x"], p["lru_lambda"], lru_prev_pad, lru_h0,
        p["w_out"], n_seq=n_seq, groups_per_block=groups_per_block, mod_index=mod_index, fresh=fresh)
    d_ff = p["w_down"].shape[0]
    y, tail_g, tail_v = _ffn_call(
        hn2, p["w_up"], p["w_ffn_conv"], p["b_ffn_conv"], p["w_down"], x1, gt2,
        ffn_prev[:, :, :d_ff], ffn_prev[:, :, d_ff:],
        n_seq=n_seq, groups_per_block=groups_per_block, mod_index=mod_index, fresh=fresh)
    width = xl_tail.shape[-1]
    lru_conv_new = xl_tail.reshape(n_seq, -1, SUBLANES, width)[:, -1, SUBLANES - (LRU_CONV_W - 1):, :]
    ffn_conv_new = jnp.concatenate([tail_g, tail_v], axis=-1)
    return (y.reshape(n_seq, t, d), kf, vf, lru_conv_new, h_last.reshape(n_seq, width), ffn_conv_new)


def kernel(x_prompt, x_sample, c_prompt, c_sample, cache_k, cache_v, page_table, state_lru_conv, state_lru_h, state_ffn_conv, w_ada, b_ada, g_norm1, w_in, g_q, g_k, b_sb, w_lru_conv, b_lru_conv, w_lru_a, b_lru_a, w_lru_x, b_lru_x, lru_lambda, w_out, g_norm2, w_up, w_ffn_conv, b_ffn_conv, w_down):
    depth = w_ada.shape[0]
    assert depth == 1, "single-layer step"
    l = 0
    bsz, seq, d = x_prompt.shape
    dec_b, dec_t, _ = x_sample.shape
    page, n_heads = cache_k.shape[2], cache_k.shape[3]
    width = n_heads * HEAD_DIM
    lru_w = state_lru_h.shape[-1]
    d_ff2 = state_ffn_conv.shape[-1]

    p = dict(
        g_norm1=g_norm1[l], w_qkv=w_in[l][:, :3 * width].astype(BF16),
        w_lru_in=w_in[l][:, 3 * width:].astype(BF16), g_q=g_q[l], g_k=g_k[l],
        w_lru_conv=w_lru_conv[l], b_lru_conv=b_lru_conv[l],
        w_lru_a=w_lru_a[l].astype(BF16), b_lru_a=b_lru_a[l],
        w_lru_x=w_lru_x[l].astype(BF16), b_lru_x=b_lru_x[l], lru_lambda=lru_lambda[l],
        w_out=w_out[l].astype(BF16), g_norm2=g_norm2[l], w_up=w_up[l].astype(BF16),
        w_ffn_conv=w_ffn_conv[l], b_ffn_conv=b_ffn_conv[l], w_down=w_down[l].astype(BF16))

    n_c = bsz + dec_b
    pad_c = (-n_c) % SUBLANES
    c_all = jnp.concatenate([c_prompt, c_sample, jnp.zeros((pad_c, d), F32)], axis=0)
    mod = _ada_call(c_all, w_ada[l], b_ada[l])
    mod_p = mod[:bsz].reshape(bsz, N_MOD, d)
    mod_s = mod[bsz:n_c].reshape(dec_b, N_MOD, d)

    def prompt_attn(q, kf, kb, vf, vb):
        return _attn_prompt_call(q, kb, vb, b_sb[l], batch=bsz, seq=seq)

    zeros_prev = jnp.zeros((bsz, SUBLANES, lru_w), F32)
    zeros_h = jnp.zeros((bsz, 1, lru_w), F32)
    zeros_ffn = jnp.zeros((bsz, FFN_CONV_W - 1, d_ff2), F32)
    yp, kp, vp, lcp, lhp, fcp = _run_group(
        x_prompt, mod_p, prompt_attn, zeros_prev, zeros_h, zeros_ffn, True, 1, ROW_TILE, p)

    def sample_attn(q, kf, kb, vf, vb):
        q3 = q.astype(F32).reshape(dec_b, dec_t, width)
        pool = cache_k.shape[1]
        ck = cache_k[l].reshape(pool, page * n_heads, HEAD_DIM)
        cv = cache_v[l].reshape(pool, page * n_heads, HEAD_DIM)
        o = _attn_sample_call(q3, kf, vf, ck, cv, page_table, b_sb[l])
        return o.reshape(dec_b * dec_t, width)

    prev_pad = jnp.concatenate(
        [jnp.zeros((dec_b, SUBLANES - (LRU_CONV_W - 1), lru_w), F32), state_lru_conv[l]], axis=1)
    ys, ks, vs, lcs, lhs, fcs = _run_group(
        x_sample, mod_s, sample_attn, prev_pad, state_lru_h[l].reshape(dec_b, 1, lru_w),
        state_ffn_conv[l], False, dec_b, dec_b * dec_t, p)

    n_pages_p = seq // page
    return (yp, ys,
            kp.reshape(1, bsz, n_pages_p, page, n_heads, HEAD_DIM),
            vp.reshape(1, bsz, n_pages_p, page, n_heads, HEAD_DIM),
            ks.reshape(1, dec_b, dec_t, n_heads, HEAD_DIM),
            vs.reshape(1, dec_b, dec_t, n_heads, HEAD_DIM),
            lcp[None], lcs[None], lhp[None], lhs[None], fcp[None], fcs[None])
```

```python
import functools
import math

import jax
import jax.numpy as jnp
from jax import lax
from jax.experimental import pallas as pl
from jax.experimental.pallas import tpu as pltpu

F32 = jnp.float32
BF16 = jnp.bfloat16

EPS = 1e-6
LRU_C = 8.0
HEAD_DIM = 128
LRU_BLOCKS = 8
LRU_CONV_W = 4
FFN_CONV_W = 3
N_MOD = 6
SB_SCALE = 1.0 / math.sqrt(HEAD_DIM)

SUBLANES = 8
VMEM_LIMIT = 56 * 1024 * 1024

ROW_TILE = 512
INPROJ_SUB_BLOCKS = 2
MIXER_SUB_BLOCKS = 2
ATTN_TQ = 256
ATTN_TK = 256
ATTN_HEADS = 4
ADA_TN = 1024
FFN_TF = 512
FFN_SUB_BLOCKS = 2
PAGES_PER_STEP = 8


def _cparams(n_axes):
    return pltpu.CompilerParams(dimension_semantics=("arbitrary",) * n_axes,
                                vmem_limit_bytes=VMEM_LIMIT)


def _softplus(z):
    return jnp.maximum(z, 0.0) + jnp.log1p(jnp.exp(-jnp.abs(z)))


def _softplus_logits(z):
    return jnp.maximum(z, 0.0) + jnp.log(1.0 + jnp.exp(-jnp.abs(z)))


def _gelu_tanh(x):
    c = math.sqrt(2.0 / math.pi)
    return x * (0.5 * (1.0 + jnp.tanh(c * (x + 0.044715 * (x * x * x)))))


def _suffix_sum_matrix(n_keys):
    j = lax.broadcasted_iota(jnp.int32, (n_keys, n_keys), 0)
    s = lax.broadcasted_iota(jnp.int32, (n_keys, n_keys), 1)
    return jnp.where(j >= s, 1.0, 0.0).astype(BF16)


def _ada_kernel(c_ref, w_ref, b_ref, o_ref):
    c = c_ref[...]
    s = (c * jax.nn.sigmoid(c)).astype(BF16)
    o_ref[...] = jnp.dot(s, w_ref[...].astype(BF16), preferred_element_type=F32) + b_ref[...]


def _ada_call(c_all, w_ada, b_ada):
    m, d = c_all.shape
    n = w_ada.shape[1]
    return pl.pallas_call(
        _ada_kernel,
        grid=(n // ADA_TN,),
        in_specs=[pl.BlockSpec((m, d), lambda j: (0, 0)),
                  pl.BlockSpec((d, ADA_TN), lambda j: (0, j)),
                  pl.BlockSpec((1, ADA_TN), lambda j: (0, j))],
        out_specs=pl.BlockSpec((m, ADA_TN), lambda j: (0, j)),
        out_shape=jax.ShapeDtypeStruct((m, n), F32),
        compiler_params=_cparams(1),
        name="adaln",
    )(c_all, w_ada, b_ada.reshape(1, n))


def _head_rmsnorm(a, g, n_heads):
    outs = []
    for h in range(n_heads):
        ah = a[:, h * HEAD_DIM:(h + 1) * HEAD_DIM]
        ms = jnp.mean(ah * ah, axis=-1, keepdims=True)
        outs.append(ah * lax.rsqrt(ms + EPS) * g)
    return outs


def _inproj_kernel(x_ref, sh_ref, sc_ref, g1_ref, w_ref, gq_ref, gk_ref,
                   q_ref, kf_ref, kb_ref, vf_ref, vb_ref, hn_ref, *, n_heads, n_sub):
    j = pl.program_id(1)
    tm, d = hn_ref.shape
    sub = tm // n_sub
    row_slices = [slice(s * sub, (s + 1) * sub) for s in range(n_sub)]
    head_cols = [slice(h * HEAD_DIM, (h + 1) * HEAD_DIM) for h in range(n_heads)]

    def normalize(rs):
        x = x_ref[...] if n_sub == 1 else x_ref[:, rs, :]
        ms = jnp.mean(x * x, axis=-1, keepdims=True)
        y = x * lax.rsqrt(ms + EPS) * g1_ref[...]
        hn = y * (1.0 + sc_ref[...]) + sh_ref[...]
        hn_ref[rs, :] = hn.reshape(sub, d).astype(BF16)

    def project(epilogue):
        accs = [jnp.dot(hn_ref[rs, :], w_ref[...], preferred_element_type=F32) for rs in row_slices]
        for rs, acc in zip(row_slices, accs):
            epilogue(rs, acc)

    def head_rows(rs, h):
        return pl.ds(rs.start * n_heads + h, sub, stride=n_heads)

    def q_epilogue(rs, acc):
        for h, qh in enumerate(_head_rmsnorm(acc, gq_ref[...], n_heads)):
            q_ref[rs, head_cols[h]] = (qh * SB_SCALE).astype(BF16)

    def k_epilogue(rs, acc):
        for h, kh in enumerate(_head_rmsnorm(acc, gk_ref[...], n_heads)):
            kf_ref[head_rows(rs, h), :] = kh
            kb_ref[rs, head_cols[h]] = kh.astype(BF16)

    def v_epilogue(rs, acc):
        for h in range(n_heads):
            vf_ref[head_rows(rs, h), :] = acc[:, head_cols[h]]
        vb_ref[rs, :] = acc.astype(BF16)

    @pl.when(j == 0)
    def _():
        for rs in row_slices:
            normalize(rs)
        project(q_epilogue)

    for t, epilogue in enumerate([k_epilogue, v_epilogue], start=1):
        pl.when(j == t)(functools.partial(project, epilogue))


def _inproj_call(x3, sh, sc, g1, w_qkv_bf, g_q, g_k, *, groups_per_block, mod_index):
    n_groups, rows, d = x3.shape
    g = groups_per_block
    tm = g * rows
    m = n_groups * rows
    n_blocks = n_groups // g
    width = w_qkv_bf.shape[1] // 3
    n_heads = width // HEAD_DIM
    row_spec = pl.BlockSpec((tm, width), lambda i, j: (i, 0))
    head_spec = pl.BlockSpec((tm * n_heads, HEAD_DIM), lambda i, j: (i, 0))
    mod_spec = pl.BlockSpec((g, 1, d), lambda i, j: (mod_index(i), 0, 0))
    head_out = jax.ShapeDtypeStruct((m * n_heads, HEAD_DIM), F32)
    bf_out = jax.ShapeDtypeStruct((m, width), BF16)
    return pl.pallas_call(
        functools.partial(_inproj_kernel, n_heads=n_heads, n_sub=INPROJ_SUB_BLOCKS if g == 1 else 1),
        grid=(n_blocks, 3),
        in_specs=[pl.BlockSpec((g, rows, d), lambda i, j: (i, 0, 0)),
                  mod_spec, mod_spec,
                  pl.BlockSpec((1, d), lambda i, j: (0, 0)),
                  pl.BlockSpec((d, width), lambda i, j: (0, j)),
                  pl.BlockSpec((1, HEAD_DIM), lambda i, j: (0, 0)),
                  pl.BlockSpec((1, HEAD_DIM), lambda i, j: (0, 0))],
        out_specs=[row_spec, head_spec, row_spec, head_spec, row_spec,
                   pl.BlockSpec((tm, d), lambda i, j: (i, 0))],
        out_shape=[bf_out, head_out, bf_out, head_out, bf_out, jax.ShapeDtypeStruct((m, d), BF16)],
        compiler_params=_cparams(2),
        name="inproj",
    )(x3, sh, sc, g1.reshape(1, d), w_qkv_bf, g_q.reshape(1, HEAD_DIM), g_k.reshape(1, HEAD_DIM))


def _sb_mass(z, suffix_mat, mask):
    sp = _softplus_logits(z)
    if mask is not None:
        sp = jnp.where(mask, sp, 0.0)
    return jnp.dot(sp.astype(BF16), suffix_mat, preferred_element_type=F32)


def _sb_weights(z, cums, later, mask):
    w = jnp.exp(z - cums - later)
    if mask is not None:
        w = jnp.where(mask, w, 0.0)
    return w, later + cums[:, 0:1]


def _attn_prompt_kernel(bias_ref, q_ref, k_ref, v_ref, o_ref):
    hg = pl.program_id(1)
    qi = pl.program_id(2)
    suffix_mat = _suffix_sum_matrix(ATTN_TK)
    heads = range(ATTN_HEADS)
    cols = [slice(hh * HEAD_DIM, (hh + 1) * HEAD_DIM) for hh in heads]
    biases = [bias_ref[hg * ATTN_HEADS + hh] for hh in heads]
    qs = [q_ref[:, c] for c in cols]

    def tile(j, carry, mask):
        start = pl.multiple_of(j * ATTN_TK, ATTN_TK)
        nt = (((1,), (1,)), ((), ()))
        zs = [lax.dot_general(qs[hh], k_ref[pl.ds(start, ATTN_TK), cols[hh]], nt,
                              preferred_element_type=F32) + biases[hh] for hh in heads]
        cums = [_sb_mass(zs[hh], suffix_mat, mask) for hh in heads]
        ws = [_sb_weights(zs[hh], cums[hh], carry[hh][1], mask) for hh in heads]
        return tuple(
            (carry[hh][0] + jnp.dot(ws[hh][0].astype(BF16), v_ref[pl.ds(start, ATTN_TK), cols[hh]],
                                    preferred_element_type=F32), ws[hh][1]) for hh in heads)

    row = lax.broadcasted_iota(jnp.int32, (ATTN_TQ, ATTN_TK), 0)
    col = lax.broadcasted_iota(jnp.int32, (ATTN_TQ, ATTN_TK), 1)
    carry = tuple((jnp.zeros((ATTN_TQ, HEAD_DIM), F32), jnp.zeros((ATTN_TQ, 1), F32)) for _ in heads)
    carry = tile(qi, carry, col < row)
    carry = lax.fori_loop(0, qi, lambda t, c: tile(qi - 1 - t, c, None), carry)
    for hh in heads:
        o_ref[:, cols[hh]] = carry[hh][0].astype(o_ref.dtype)


def _attn_prompt_call(q, k, v, b_sb, *, batch, seq):
    m, width = q.shape
    n_heads = width // HEAD_DIM
    nq = seq // ATTN_TQ
    gw = ATTN_HEADS * HEAD_DIM
    q_spec = pl.BlockSpec((ATTN_TQ, gw), lambda b, h, i: (b * nq + i, h))
    kv_spec = pl.BlockSpec((seq, gw), lambda b, h, i: (b, h))
    return pl.pallas_call(
        _attn_prompt_kernel,
        grid=(batch, n_heads // ATTN_HEADS, nq),
        in_specs=[pl.BlockSpec(memory_space=pltpu.SMEM), q_spec, kv_spec, kv_spec],
        out_specs=q_spec,
        out_shape=jax.ShapeDtypeStruct((m, width), BF16),
        compiler_params=_cparams(3),
        name="attn_prompt",
    )(b_sb, q, k, v)


def _attn_sample_kernel(pt_ref, bias_ref, q_ref, knew_ref, vnew_ref, *rest, n_heads, n_new):
    del pt_ref
    pages = PAGES_PER_STEP
    k_refs = rest[:pages]
    v_refs = rest[pages:2 * pages]
    o_ref = rest[2 * pages]
    kpad_ref, vpad_ref, acc_ref, later_ref = rest[2 * pages + 1:]
    g = pl.program_id(1)
    page_len = kpad_ref.shape[0] // n_heads
    suffix_mat = _suffix_sum_matrix(page_len)
    bias = jnp.concatenate([jnp.full((n_new, 1), bias_ref[h], F32) for h in range(n_heads)], axis=0)
    nt = (((1,), (1,)), ((), ()))

    def visit(k_pages, v_pages, mask):
        qh = [q_ref[0, :, h * HEAD_DIM:(h + 1) * HEAD_DIM].astype(BF16) for h in range(n_heads)]
        zs = []
        for kp in k_pages:
            z = [lax.dot_general(qh[h], kp[pl.ds(h, page_len, stride=n_heads), :].astype(BF16), nt,
                                 preferred_element_type=F32) for h in range(n_heads)]
            zs.append(jnp.concatenate(z, axis=0) + bias)
        cums = [_sb_mass(z, suffix_mat, mask) for z in zs]
        later = later_ref[...]
        ws = []
        for z, c in zip(zs, cums):
            w, later = _sb_weights(z, c, later, mask)
            ws.append(w)
        later_ref[...] = later
        acc = [acc_ref[h] for h in range(n_heads)]
        for w, vp in zip(ws, v_pages):
            for h in range(n_heads):
                wh = w[h * n_new:(h + 1) * n_new].astype(BF16)
                vh = vp[pl.ds(h, page_len, stride=n_heads), :].astype(BF16)
                acc[h] = acc[h] + jnp.dot(wh, vh, preferred_element_type=F32)
        for h in range(n_heads):
            acc_ref[h] = acc[h]

    @pl.when(g == 0)
    def _():
        acc_ref[...] = jnp.zeros_like(acc_ref)
        later_ref[...] = jnp.zeros_like(later_ref)
        kpad_ref[...] = jnp.zeros_like(kpad_ref)
        vpad_ref[...] = jnp.zeros_like(vpad_ref)
        kpad_ref[0:n_new * n_heads, :] = knew_ref[...]
        vpad_ref[0:n_new * n_heads, :] = vnew_ref[...]
        row = lax.broadcasted_iota(jnp.int32, (n_heads * n_new, page_len), 0) % n_new
        col = lax.broadcasted_iota(jnp.int32, (n_heads * n_new, page_len), 1)
        visit([kpad_ref], [vpad_ref], col < row)

    visit([r.at[0] for r in k_refs], [r.at[0] for r in v_refs], None)

    @pl.when(g == pl.num_programs(1) - 1)
    def _():
        for h in range(n_heads):
            o_ref[0, :, h * HEAD_DIM:(h + 1) * HEAD_DIM] = acc_ref[h].astype(o_ref.dtype)


def _attn_sample_call(q3, knew, vnew, cache_k, cache_v, page_table, b_sb):
    bsz, n_new, width = q3.shape
    n_heads = width // HEAD_DIM
    n_pages = page_table.shape[1]
    page_rows = cache_k.shape[1]
    pages = PAGES_PER_STEP
    steps = n_pages // pages

    def page_spec(p):
        def index(b, g, pt):
            return (pt[b, n_pages - 1 - (g * pages + p)], 0, 0)
        return pl.BlockSpec((1, page_rows, HEAD_DIM), index)

    new_spec = pl.BlockSpec((n_new * n_heads, HEAD_DIM), lambda b, g, pt: (b, 0))
    q_spec = pl.BlockSpec((1, n_new, width), lambda b, g, pt: (b, 0, 0))
    grid_spec = pltpu.PrefetchScalarGridSpec(
        num_scalar_prefetch=1,
        grid=(bsz, steps),
        in_specs=([pl.BlockSpec(memory_space=pltpu.SMEM), q_spec, new_spec, new_spec]
                  + [page_spec(p) for p in range(pages)] * 2),
        out_specs=q_spec,
        scratch_shapes=[pltpu.VMEM((page_rows, HEAD_DIM), F32),
                        pltpu.VMEM((page_rows, HEAD_DIM), F32),
                        pltpu.VMEM((n_heads, n_new, HEAD_DIM), F32),
                        pltpu.VMEM((n_heads * n_new, 1), F32)],
    )
    return pl.pallas_call(
        functools.partial(_attn_sample_kernel, n_heads=n_heads, n_new=n_new),
        grid_spec=grid_spec,
        out_shape=jax.ShapeDtypeStruct((bsz, n_new, width), F32),
        compiler_params=_cparams(2),
        name="attn_sample",
    )(page_table, b_sb, q3, knew, vnew, *([cache_k] * pages), *([cache_v] * pages))


def _segmented_linear_scan(a, b, seg_len):
    rows = lax.broadcasted_iota(jnp.int32, a.shape, 0) % seg_len
    d = 1
    while d < seg_len:
        keep = rows >= d
        a_prev = pltpu.roll(a, d, 0)
        b_prev = pltpu.roll(b, d, 0)
        b = jnp.where(keep, a * b_prev + b, b)
        a = jnp.where(keep, a * a_prev, a)
        d *= 2
    return a, b


def _mixer_kernel(hn_ref, attn_ref, x_ref, gt_ref, sh_ref, sc_ref, g2_ref,
                  wl_ref, cw_ref, cb_ref, wa_ref, ba_ref, wx_ref, bx_ref, lam_ref, prev_ref, h0_ref, wo_ref,
                  x1_ref, hn2_ref, hlast_ref, xtail_ref, xp_ref, hcar_ref,
                  *, groups, rows, blocks_per_seq, fresh, n_sub):
    i = pl.program_id(0)
    width = xp_ref.shape[-1]
    d = x_ref.shape[-1]
    first_block = (i % blocks_per_seq) == 0
    pad = SUBLANES
    sub_rows = rows // n_sub
    sub = groups * sub_rows
    row_slices = [slice(s * sub, (s + 1) * sub) for s in range(n_sub)]

    @pl.when(first_block)
    def _():
        if fresh:
            xp_ref[:, 0:pad, :] = jnp.zeros((groups, pad, width), F32)
            hcar_ref[...] = jnp.zeros_like(hcar_ref)
        else:
            xp_ref[:, 0:pad, :] = prev_ref[...]
            hcar_ref[...] = h0_ref[...]

    xgs = [jnp.dot(hn_ref[rs, :], wl_ref[...], preferred_element_type=F32) for rs in row_slices]
    mixes = [jnp.dot(attn_ref[rs, :].astype(BF16), wo_ref[0:width, :], preferred_element_type=F32)
             for rs in row_slices]
    for s in range(n_sub):
        xp_ref[:, pad + s * sub_rows:pad + (s + 1) * sub_rows, :] = (
            xgs[s][:, 0:width].reshape(groups, sub_rows, width))

    cw = cw_ref[...]
    bw = width // LRU_BLOCKS
    neg_c_softplus = (-LRU_C) * _softplus(-lam_ref[...])
    h_prev = hcar_ref[...] if rows == SUBLANES else hcar_ref[0]
    for s, rs in enumerate(row_slices):
        t0 = s * sub_rows
        xc = cb_ref[...] + cw[0:1] * xp_ref[:, pl.ds(pad - 3 + t0, sub_rows), :]
        for t in range(1, LRU_CONV_W):
            xc = xc + cw[t:t + 1] * xp_ref[:, pl.ds(pad - 3 + t0 + t, sub_rows), :]
        xc = xc.reshape(sub, width)

        xcb = xc.astype(BF16)
        r_parts, i_parts = [], []
        for n in range(LRU_BLOCKS):
            xb = xcb[:, n * bw:(n + 1) * bw]
            r_parts.append(jnp.dot(xb, wa_ref[n], preferred_element_type=F32))
            i_parts.append(jnp.dot(xb, wx_ref[n], preferred_element_type=F32))
        r = jax.nn.sigmoid(jnp.concatenate(r_parts, axis=1) + ba_ref[...])
        ig = jax.nn.sigmoid(jnp.concatenate(i_parts, axis=1) + bx_ref[...])
        log_a = r * neg_c_softplus
        a = jnp.exp(log_a)
        one_minus_a2 = -jnp.tanh(log_a) * (a * a + 1.0)
        mult = jnp.where(one_minus_a2 > 0.0, one_minus_a2 * lax.rsqrt(one_minus_a2), 0.0)
        if fresh and s == 0:
            t_row = lax.broadcasted_iota(jnp.int32, (sub, 1), 0)
            start_row = jnp.where(first_block, 0, -1)
            mult = jnp.where(t_row == start_row, 1.0, mult)
        b = mult * ig * xc

        a_cum, h = _segmented_linear_scan(a, b, SUBLANES)
        if rows == SUBLANES:
            h = h + a_cum * jnp.broadcast_to(h_prev, (groups, rows, width)).reshape(sub, width)
            h_prev = h.reshape(groups, rows, width)[:, rows - 1:rows, :]
        else:
            tiles = []
            for t in range(sub_rows // SUBLANES):
                ts = slice(t * SUBLANES, (t + 1) * SUBLANES)
                h_t = h[ts] + a_cum[ts] * h_prev
                tiles.append(h_t)
                h_prev = h_t[SUBLANES - 1:SUBLANES]
            h = jnp.concatenate(tiles, axis=0)
        lru = (h * _gelu_tanh(xgs[s][:, width:2 * width])).astype(BF16)

        mix = mixes[s] + jnp.dot(lru, wo_ref[width:2 * width, :], preferred_element_type=F32)
        x1 = x_ref[:, t0:t0 + sub_rows, :] + gt_ref[...] * mix.reshape(groups, sub_rows, d)
        x1_ref[:, t0:t0 + sub_rows, :] = x1
        ms = jnp.mean(x1 * x1, axis=-1, keepdims=True)
        y = x1 * lax.rsqrt(ms + EPS) * g2_ref[...]
        hn2 = y * (1.0 + sc_ref[...]) + sh_ref[...]
        hn2_ref[rs, :] = hn2.reshape(sub, d).astype(BF16)

    h_last = h_prev if rows == SUBLANES else h_prev[None]
    hcar_ref[...] = h_last
    hlast_ref[...] = h_last
    tail = xp_ref[:, rows:rows + pad, :]
    xtail_ref[...] = tail
    xp_ref[:, 0:pad, :] = tail


def _mixer_call(hn, attn, x3, gt, sh, sc, g2, w_lru_bf, cw, cb, wa_bf, ba, wx_bf, bx, lam, prev_pad, h0,
                w_out_bf, *, n_seq, groups_per_block, mod_index, fresh):
    n_groups, rows, d = x3.shape
    g = groups_per_block
    tm = g * rows
    m = n_groups * rows
    n_blocks = n_groups // g
    blocks_per_seq = n_blocks // (n_seq // g)
    width = attn.shape[1]
    bw = width // LRU_BLOCKS
    resident = pl.Buffered(1)
    x_spec = pl.BlockSpec((g, rows, d), lambda i: (i, 0, 0))
    mod_spec = pl.BlockSpec((g, 1, d), lambda i: (mod_index(i), 0, 0))
    vec_spec = pl.BlockSpec((1, width), lambda i: (0, 0))
    gate_spec = pl.BlockSpec((LRU_BLOCKS, bw, bw), lambda i: (0, 0, 0))
    seq_index = lambda i: (i // blocks_per_seq, 0, 0)
    return pl.pallas_call(
        functools.partial(_mixer_kernel, groups=g, rows=rows, blocks_per_seq=blocks_per_seq, fresh=fresh,
                          n_sub=MIXER_SUB_BLOCKS if g == 1 else 1),
        grid=(n_blocks,),
        in_specs=[pl.BlockSpec((tm, d), lambda i: (i, 0)),
                  pl.BlockSpec((tm, width), lambda i: (i, 0)),
                  x_spec, mod_spec, mod_spec, mod_spec,
                  pl.BlockSpec((1, d), lambda i: (0, 0)),
                  pl.BlockSpec((d, 2 * width), lambda i: (0, 0), pipeline_mode=resident),
                  pl.BlockSpec((LRU_CONV_W, width), lambda i: (0, 0)), vec_spec,
                  gate_spec, vec_spec, gate_spec, vec_spec, vec_spec,
                  pl.BlockSpec((g, SUBLANES, width), seq_index),
                  pl.BlockSpec((g, 1, width), seq_index),
                  pl.BlockSpec((d, d), lambda i: (0, 0), pipeline_mode=resident)],
        out_specs=[x_spec,
                   pl.BlockSpec((tm, d), lambda i: (i, 0)),
                   pl.BlockSpec((g, 1, width), seq_index),
                   pl.BlockSpec((g, SUBLANES, width), lambda i: (i, 0, 0))],
        out_shape=[jax.ShapeDtypeStruct((n_groups, rows, d), F32),
                   jax.ShapeDtypeStruct((m, d), BF16),
                   jax.ShapeDtypeStruct((n_seq, 1, width), F32),
                   jax.ShapeDtypeStruct((n_groups, SUBLANES, width), F32)],
        scratch_shapes=[pltpu.VMEM((g, rows + SUBLANES, width), F32),
                        pltpu.VMEM((g, 1, width), F32)],
        compiler_params=_cparams(1),
        name="mixer",
    )(hn, attn, x3, gt, sh, sc, g2.reshape(1, d), w_lru_bf, cw, cb.reshape(1, width), wa_bf,
      ba.reshape(1, width), wx_bf, bx.reshape(1, width), lam.reshape(1, width), prev_pad, h0, w_out_bf)


def _ffn_kernel(hn_ref, wg_ref, wv_ref, cwg_ref, cwv_ref, cbg_ref, cbv_ref, wd_ref,
                x1_ref, gt_ref, pg_ref, pv_ref,
                y_ref, tg_ref, tv_ref, acc_ref, cg_ref, cv_ref,
                *, groups, rows, blocks_per_seq, fresh):
    i = pl.program_id(0)
    j = pl.program_id(1)
    tm = groups * rows
    tf = wg_ref.shape[1]
    first_block = (i % blocks_per_seq) == 0
    pad = SUBLANES
    if not fresh and blocks_per_seq != 1:
        raise NotImplementedError("continuing sequences must fit one row block")

    @pl.when(j == 0)
    def _():
        acc_ref[...] = jnp.zeros_like(acc_ref)

    @pl.when(jnp.logical_and(i == 0, j == 0))
    def _():
        cg_ref[...] = jnp.zeros_like(cg_ref)
        cv_ref[...] = jnp.zeros_like(cv_ref)

    n_sub = FFN_SUB_BLOCKS if groups == 1 else 1
    sub = tm // n_sub
    row_slices = [slice(s * sub, (s + 1) * sub) for s in range(n_sub)]
    ups = []
    for rs in row_slices:
        hn = hn_ref[rs, :]
        ups.append((jnp.dot(hn, wg_ref[...], preferred_element_type=F32),
                    jnp.dot(hn, wv_ref[...], preferred_element_type=F32)))

    def boundary_rows(p_ref, carry_ref):
        if fresh:
            prev = jnp.where(first_block, 0.0, carry_ref[j])
            return prev[:, pad - 2:pad - 1, :], prev[:, pad - 1:pad, :]
        return p_ref[:, 0:1, :], p_ref[:, 1:2, :]

    def conv(u, p0, p1, cw_ref, cb_ref):
        u1 = pltpu.roll(u, 1, 0)
        u2 = pltpu.roll(u, 2, 0)
        if groups == 1:
            top = lax.broadcasted_iota(jnp.int32, (pad, tf), 0)
            u1_top = jnp.where(top == 0, p1, u1[0:pad])
            u2_top = jnp.where(top == 0, p0, jnp.where(top == 1, p1, u2[0:pad]))
            u1 = jnp.concatenate([u1_top, u1[pad:]], axis=0)
            u2 = jnp.concatenate([u2_top, u2[pad:]], axis=0)
        else:
            t_row = lax.broadcasted_iota(jnp.int32, (tm, tf), 0) % rows
            p1r = jnp.broadcast_to(p1, (groups, rows, tf)).reshape(tm, tf)
            p0r = jnp.broadcast_to(p0, (groups, rows, tf)).reshape(tm, tf)
            u1 = jnp.where(t_row == 0, p1r, u1)
            u2 = jnp.where(t_row == 0, p0r, jnp.where(t_row == 1, p1r, u2))
        cw = cw_ref[...]
        uc = cb_ref[...] + cw[0:1] * u2
        uc = uc + cw[1:2] * u1
        return uc + cw[2:3] * u

    pg = boundary_rows(pg_ref, cg_ref)
    pv = boundary_rows(pv_ref, cv_ref)
    if groups == 1:
        pg = (pg[0][0], pg[1][0])
        pv = (pv[0][0], pv[1][0])
    for s, rs in enumerate(row_slices):
        ug, uv = ups[s]
        gated = (_gelu_tanh(conv(ug, pg[0], pg[1], cwg_ref, cbg_ref))
                 * conv(uv, pv[0], pv[1], cwv_ref, cbv_ref)).astype(BF16)
        acc_ref[rs, :] += jnp.dot(gated, wd_ref[...], preferred_element_type=F32)
        pg = (ug[sub - 2:sub - 1], ug[sub - 1:sub])
        pv = (uv[sub - 2:sub - 1], uv[sub - 1:sub])

    ug3 = ups[-1][0].reshape(groups, rows // n_sub, tf)
    uv3 = ups[-1][1].reshape(groups, rows // n_sub, tf)
    last = rows // n_sub
    cg_ref[j] = ug3[:, last - pad:last, :]
    cv_ref[j] = uv3[:, last - pad:last, :]
    tg_ref[...] = ug3[:, last - 2:last, :]
    tv_ref[...] = uv3[:, last - 2:last, :]

    @pl.when(j == pl.num_programs(1) - 1)
    def _():
        y_ref[...] = x1_ref[...] + gt_ref[...] * acc_ref[...].reshape(y_ref.shape)


def _ffn_call(hn2, w_up_bf, cw, cb, w_down_bf, x1, gt, prev_g, prev_v, *,
              n_seq, groups_per_block, mod_index, fresh):
    n_groups, rows, d = x1.shape
    g = groups_per_block
    tm = g * rows
    n_blocks = n_groups // g
    blocks_per_seq = n_blocks // (n_seq // g)
    d_ff = w_down_bf.shape[0]
    tf = FFN_TF
    nj = d_ff // tf
    seq_index = lambda i: i // blocks_per_seq
    x_spec = pl.BlockSpec((g, rows, d), lambda i, j: (i, 0, 0))
    prev_spec = pl.BlockSpec((g, FFN_CONV_W - 1, tf), lambda i, j: (seq_index(i), 0, j))
    tail_spec = pl.BlockSpec((g, FFN_CONV_W - 1, tf), lambda i, j: (i, 0, j))
    tail_out = jax.ShapeDtypeStruct((n_groups, FFN_CONV_W - 1, d_ff), F32)
    y, tail_g, tail_v = pl.pallas_call(
        functools.partial(_ffn_kernel, groups=g, rows=rows, blocks_per_seq=blocks_per_seq, fresh=fresh),
        grid=(n_blocks, nj),
        in_specs=[pl.BlockSpec((tm, d), lambda i, j: (i, 0)),
                  pl.BlockSpec((d, tf), lambda i, j: (0, j)),
                  pl.BlockSpec((d, tf), lambda i, j: (0, j + nj)),
                  pl.BlockSpec((FFN_CONV_W, tf), lambda i, j: (0, j)),
                  pl.BlockSpec((FFN_CONV_W, tf), lambda i, j: (0, j + nj)),
                  pl.BlockSpec((1, tf), lambda i, j: (0, j)),
                  pl.BlockSpec((1, tf), lambda i, j: (0, j + nj)),
                  pl.BlockSpec((tf, d), lambda i, j: (j, 0)),
                  x_spec,
                  pl.BlockSpec((g, 1, d), lambda i, j: (mod_index(i), 0, 0)),
                  prev_spec, prev_spec],
        out_specs=[x_spec, tail_spec, tail_spec],
        out_shape=[jax.ShapeDtypeStruct((n_groups, rows, d), F32), tail_out, tail_out],
        scratch_shapes=[pltpu.VMEM((tm, d), F32),
                        pltpu.VMEM((nj, g, SUBLANES, tf), F32),
                        pltpu.VMEM((nj, g, SUBLANES, tf), F32)],
        compiler_params=_cparams(2),
        name="convffn",
    )(hn2, w_up_bf, w_up_bf, cw, cw, cb.reshape(1, 2 * d_ff), cb.reshape(1, 2 * d_ff), w_down_bf,
      x1, gt, prev_g, prev_v)
    last = lambda t: t.reshape(n_seq, n_groups // n_seq, FFN_CONV_W - 1, d_ff)[:, -1]
    return y, last(tail_g), last(tail_v)


def _run_group(x, mod, attn_fn, lru_prev_pad, lru_h0, ffn_prev, fresh, groups_per_block, rows_per_block, p):
    n_seq, t, d = x.shape
    rows = rows_per_block // groups_per_block
    x3 = x.reshape(n_seq * t // rows, rows, d)
    blocks_per_seq = max(1, t // rows_per_block)
    if groups_per_block == 1:
        mod_index = lambda i: i // blocks_per_seq
    else:
        mod_index = lambda i: i
    mods = [mod[:, n:n + 1, :] for n in range(N_MOD)]
    sh1, sc1, gt1, sh2, sc2, gt2 = mods

    q, kf, kb, vf, vb, hn = _inproj_call(
        x3, sh1, sc1, p["g_norm1"], p["w_qkv"], p["g_q"], p["g_k"],
        groups_per_block=groups_per_block, mod_index=mod_index)
    attn = attn_fn(q, kf, kb, vf, vb)
    x1, hn2, h_last, xl_tail = _mixer_call(
        hn, attn, x3, gt1, sh2, sc2, p["g_norm2"], p["w_lru_in"], p["w_lru_conv"], p["b_lru_conv"],
        p["w_lru_a"], p["b_lru_a"], p["w_lru_x"], p["b_lru_x"], p["lru_lambda"], lru_prev_pad, lru_h0,
        p["w_out"], n_seq=n_seq, groups_per_block=groups_per_block, mod_index=mod_index, fresh=fresh)
    d_ff = p["w_down"].shape[0]
    y, tail_g, tail_v = _ffn_call(
        hn2, p["w_up"], p["w_ffn_conv"], p["b_ffn_conv"], p["w_down"], x1, gt2,
        ffn_prev[:, :, :d_ff], ffn_prev[:, :, d_ff:],
        n_seq=n_seq, groups_per_block=groups_per_block, mod_index=mod_index, fresh=fresh)
    width = xl_tail.shape[-1]
    lru_conv_new = xl_tail.reshape(n_seq, -1, SUBLANES, width)[:, -1, SUBLANES - (LRU_CONV_W - 1):, :]
    ffn_conv_new = jnp.concatenate([tail_g, tail_v], axis=-1)
    return (y.reshape(n_seq, t, d), kf, vf, lru_conv_new, h_last.reshape(n_seq, width), ffn_conv_new)


def kernel(x_prompt, x_sample, c_prompt, c_sample, cache_k, cache_v, page_table, state_lru_conv, state_lru_h, state_ffn_conv, w_ada, b_ada, g_norm1, w_in, g_q, g_k, b_sb, w_lru_conv, b_lru_conv, w_lru_a, b_lru_a, w_lru_x, b_lru_x, lru_lambda, w_out, g_norm2, w_up, w_ffn_conv, b_ffn_conv, w_down):
    depth = w_ada.shape[0]
    assert depth == 1, "single-layer step"
    l = 0
    bsz, seq, d = x_prompt.shape
    dec_b, dec_t, _ = x_sample.shape
    page, n_heads = cache_k.shape[2], cache_k.shape[3]
    width = n_heads * HEAD_DIM
    lru_w = state_lru_h.shape[-1]
    d_ff2 = state_ffn_conv.shape[-1]

    p = dict(
        g_norm1=g_norm1[l], w_qkv=w_in[l][:, :3 * width].astype(BF16),
        w_lru_in=w_in[l][:, 3 * width:].astype(BF16), g_q=g_q[l], g_k=g_k[l],
        w_lru_conv=w_lru_conv[l], b_lru_conv=b_lru_conv[l],
        w_lru_a=w_lru_a[l].astype(BF16), b_lru_a=b_lru_a[l],
        w_lru_x=w_lru_x[l].astype(BF16), b_lru_x=b_lru_x[l], lru_lambda=lru_lambda[l],
        w_out=w_out[l].astype(BF16), g_norm2=g_norm2[l], w_up=w_up[l].astype(BF16),
        w_ffn_conv=w_ffn_conv[l], b_ffn_conv=b_ffn_conv[l], w_down=w_down[l].astype(BF16))

    n_c = bsz + dec_b
    pad_c = (-n_c) % SUBLANES
    c_all = jnp.concatenate([c_prompt, c_sample, jnp.zeros((pad_c, d), F32)], axis=0)
    mod = _ada_call(c_all, w_ada[l], b_ada[l])
    mod_p = mod[:bsz].reshape(bsz, N_MOD, d)
    mod_s = mod[bsz:n_c].reshape(dec_b, N_MOD, d)

    def prompt_attn(q, kf, kb, vf, vb):
        return _attn_prompt_call(q, kb, vb, b_sb[l], batch=bsz, seq=seq)

    zeros_prev = jnp.zeros((bsz, SUBLANES, lru_w), F32)
    zeros_h = jnp.zeros((bsz, 1, lru_w), F32)
    zeros_ffn = jnp.zeros((bsz, FFN_CONV_W - 1, d_ff2), F32)
    yp, kp, vp, lcp, lhp, fcp = _run_group(
        x_prompt, mod_p, prompt_attn, zeros_prev, zeros_h, zeros_ffn, True, 1, ROW_TILE, p)

    def sample_attn(q, kf, kb, vf, vb):
        q3 = q.astype(F32).reshape(dec_b, dec_t, width)
        pool = cache_k.shape[1]
        ck = cache_k[l].reshape(pool, page * n_heads, HEAD_DIM)
        cv = cache_v[l].reshape(pool, page * n_heads, HEAD_DIM)
        o = _attn_sample_call(q3, kf, vf, ck, cv, page_table, b_sb[l])
        return o.reshape(dec_b * dec_t, width)

    prev_pad = jnp.concatenate(
        [jnp.zeros((dec_b, SUBLANES - (LRU_CONV_W - 1), lru_w), F32), state_lru_conv[l]], axis=1)
    ys, ks, vs, lcs, lhs, fcs = _run_group(
        x_sample, mod_s, sample_attn, prev_pad, state_lru_h[l].reshape(dec_b, 1, lru_w),
        state_ffn_conv[l], False, dec_b, dec_b * dec_t, p)

    n_pages_p = seq // page
    return (yp, ys,
            kp.reshape(1, bsz, n_pages_p, page, n_heads, HEAD_DIM),
            vp.reshape(1, bsz, n_pages_p, page, n_heads, HEAD_DIM),
            ks.reshape(1, dec_b, dec_t, n_heads, HEAD_DIM),
            vs.reshape(1, dec_b, dec_t, n_heads, HEAD_DIM),
            lcp[None], lcs[None], lhp[None], lhs[None], fcp[None], fcs[None])
```

```python
import functools
import math

import jax
import jax.numpy as jnp
from jax import lax
from jax.experimental import pallas as pl
from jax.experimental.pallas import tpu as pltpu

F32 = jnp.float32
BF16 = jnp.bfloat16

EPS = 1e-6
LRU_C = 8.0
HEAD_DIM = 128
LRU_BLOCKS = 8
LRU_CONV_W = 4
FFN_CONV_W = 3
N_MOD = 6
SB_SCALE = 1.0 / math.sqrt(HEAD_DIM)

SUBLANES = 8
VMEM_LIMIT = 56 * 1024 * 1024

ROW_TILE = 512
INPROJ_SUB_BLOCKS = 2
MIXER_SUB_BLOCKS = 2
ATTN_TQ = 256
ATTN_TK = 256
ATTN_HEADS = 4
ADA_TN = 1024
FFN_TF = 768
FFN_SUB_BLOCKS = 2
PAGES_PER_STEP = 8

def _cparams(n_axes):
    return pltpu.CompilerParams(dimension_semantics=("arbitrary",) * n_axes,
                                vmem_limit_bytes=VMEM_LIMIT)


def _softplus(z):
    return jnp.maximum(z, 0.0) + jnp.log1p(jnp.exp(-jnp.abs(z)))


def _softplus_logits(z):
    return jnp.maximum(z, 0.0) + jnp.log(1.0 + jnp.exp(-jnp.abs(z)))


def _gelu_tanh(x):
    c = math.sqrt(2.0 / math.pi)
    return x * (0.5 * (1.0 + jnp.tanh(c * (x + 0.044715 * (x * x * x)))))


def _suffix_sum_matrix(n_keys):
    j = lax.broadcasted_iota(jnp.int32, (n_keys, n_keys), 0)
    s = lax.broadcasted_iota(jnp.int32, (n_keys, n_keys), 1)
    return jnp.where(j >= s, 1.0, 0.0).astype(BF16)


def _ada_kernel(c_ref, w_ref, b_ref, o_ref):
    c = c_ref[...]
    s = (c * jax.nn.sigmoid(c)).astype(BF16)
    o_ref[...] = jnp.dot(s, w_ref[...].astype(BF16), preferred_element_type=F32) + b_ref[...]


def _ada_call(c_all, w_ada, b_ada):
    m, d = c_all.shape
    n = w_ada.shape[1]
    return pl.pallas_call(
        _ada_kernel,
        grid=(n // ADA_TN,),
        in_specs=[pl.BlockSpec((m, d), lambda j: (0, 0)),
                  pl.BlockSpec((d, ADA_TN), lambda j: (0, j)),
                  pl.BlockSpec((1, ADA_TN), lambda j: (0, j))],
        out_specs=pl.BlockSpec((m, ADA_TN), lambda j: (0, j)),
        out_shape=jax.ShapeDtypeStruct((m, n), F32),
        compiler_params=_cparams(1),
        name="adaln",
    )(c_all, w_ada, b_ada.reshape(1, n))


def _head_rmsnorm(a, g, n_heads):
    outs = []
    for h in range(n_heads):
        ah = a[:, h * HEAD_DIM:(h + 1) * HEAD_DIM]
        ms = jnp.mean(ah * ah, axis=-1, keepdims=True)
        outs.append(ah * lax.rsqrt(ms + EPS) * g)
    return outs


def _inproj_kernel(x_ref, sh_ref, sc_ref, g1_ref, w_ref, gq_ref, gk_ref,
                   q_ref, kf_ref, kb_ref, vf_ref, vb_ref, hn_ref, *, n_heads, n_sub):
    tm, d = hn_ref.shape
    width = n_heads * HEAD_DIM
    sub = tm // n_sub
    row_slices = [slice(s * sub, (s + 1) * sub) for s in range(n_sub)]
    head_cols = [slice(h * HEAD_DIM, (h + 1) * HEAD_DIM) for h in range(n_heads)]

    def normalize(rs):
        x = x_ref[...] if n_sub == 1 else x_ref[:, rs, :]
        ms = jnp.mean(x * x, axis=-1, keepdims=True)
        y = x * lax.rsqrt(ms + EPS) * g1_ref[...]
        hn = y * (1.0 + sc_ref[...]) + sh_ref[...]
        hn_ref[rs, :] = hn.reshape(sub, d).astype(BF16)

    def project(t, epilogue):
        w = w_ref[:, t * width:(t + 1) * width]
        accs = [jnp.dot(hn_ref[rs, :], w, preferred_element_type=F32) for rs in row_slices]
        for rs, acc in zip(row_slices, accs):
            epilogue(rs, acc)

    def head_rows(rs, h):
        return pl.ds(rs.start * n_heads + h, sub, stride=n_heads)

    def q_epilogue(rs, acc):
        for h, qh in enumerate(_head_rmsnorm(acc, gq_ref[...], n_heads)):
            q_ref[rs, head_cols[h]] = (qh * SB_SCALE).astype(BF16)

    def k_epilogue(rs, acc):
        for h, kh in enumerate(_head_rmsnorm(acc, gk_ref[...], n_heads)):
            kf_ref[head_rows(rs, h), :] = kh
            kb_ref[rs, head_cols[h]] = kh.astype(BF16)

    def v_epilogue(rs, acc):
        for h in range(n_heads):
            vf_ref[head_rows(rs, h), :] = acc[:, head_cols[h]]
        vb_ref[rs, :] = acc.astype(BF16)

    for rs in row_slices:
        normalize(rs)
    for t, epilogue in enumerate([q_epilogue, k_epilogue, v_epilogue]):
        project(t, epilogue)


def _inproj_call(x3, sh, sc, g1, w_qkv_bf, g_q, g_k, *, groups_per_block, mod_index):
    n_groups, rows, d = x3.shape
    g = groups_per_block
    tm = g * rows
    m = n_groups * rows
    n_blocks = n_groups // g
    width = w_qkv_bf.shape[1] // 3
    n_heads = width // HEAD_DIM
    row_spec = pl.BlockSpec((tm, width), lambda i: (i, 0))
    head_spec = pl.BlockSpec((tm * n_heads, HEAD_DIM), lambda i: (i, 0))
    mod_spec = pl.BlockSpec((g, 1, d), lambda i: (mod_index(i), 0, 0))
    head_out = jax.ShapeDtypeStruct((m * n_heads, HEAD_DIM), F32)
    bf_out = jax.ShapeDtypeStruct((m, width), BF16)
    return pl.pallas_call(
        functools.partial(_inproj_kernel, n_heads=n_heads, n_sub=INPROJ_SUB_BLOCKS if g == 1 else 1),
        grid=(n_blocks,),
        in_specs=[pl.BlockSpec((g, rows, d), lambda i: (i, 0, 0)),
                  mod_spec, mod_spec,
                  pl.BlockSpec((1, d), lambda i: (0, 0)),
                  pl.BlockSpec((d, 3 * width), lambda i: (0, 0), pipeline_mode=pl.Buffered(1)),
                  pl.BlockSpec((1, HEAD_DIM), lambda i: (0, 0)),
                  pl.BlockSpec((1, HEAD_DIM), lambda i: (0, 0))],
        out_specs=[row_spec, head_spec, row_spec, head_spec, row_spec,
                   pl.BlockSpec((tm, d), lambda i: (i, 0))],
        out_shape=[bf_out, head_out, bf_out, head_out, bf_out, jax.ShapeDtypeStruct((m, d), BF16)],
        compiler_params=_cparams(1),
        name="inproj",
    )(x3, sh, sc, g1.reshape(1, d), w_qkv_bf, g_q.reshape(1, HEAD_DIM), g_k.reshape(1, HEAD_DIM))


def _sb_mass(z, suffix_mat, mask):
    sp = _softplus_logits(z)
    if mask is not None:
        sp = jnp.where(mask, sp, 0.0)
    return jnp.dot(sp.astype(BF16), suffix_mat, preferred_element_type=F32)


def _sb_weights(z, cums, later, mask):
    w = jnp.exp(z - cums - later)
    if mask is not None:
        w = jnp.where(mask, w, 0.0)
    return w, later + cums[:, 0:1]


def _attn_prompt_kernel(bias_ref, q_ref, k_ref, v_ref, o_ref):
    hg = pl.program_id(1)
    qi = pl.program_id(2)
    suffix_mat = _suffix_sum_matrix(ATTN_TK)
    heads = range(ATTN_HEADS)
    cols = [slice(hh * HEAD_DIM, (hh + 1) * HEAD_DIM) for hh in heads]
    biases = [bias_ref[hg * ATTN_HEADS + hh] for hh in heads]
    qs = [q_ref[:, c] for c in cols]

    def tile(j, carry, mask):
        start = pl.multiple_of(j * ATTN_TK, ATTN_TK)
        nt = (((1,), (1,)), ((), ()))
        zs = [lax.dot_general(qs[hh], k_ref[pl.ds(start, ATTN_TK), cols[hh]], nt,
                              preferred_element_type=F32) + biases[hh] for hh in heads]
        cums = [_sb_mass(zs[hh], suffix_mat, mask) for hh in heads]
        ws = [_sb_weights(zs[hh], cums[hh], carry[hh][1], mask) for hh in heads]
        return tuple(
            (carry[hh][0] + jnp.dot(ws[hh][0].astype(BF16), v_ref[pl.ds(start, ATTN_TK), cols[hh]],
                                    preferred_element_type=F32), ws[hh][1]) for hh in heads)

    row = lax.broadcasted_iota(jnp.int32, (ATTN_TQ, ATTN_TK), 0)
    col = lax.broadcasted_iota(jnp.int32, (ATTN_TQ, ATTN_TK), 1)
    carry = tuple((jnp.zeros((ATTN_TQ, HEAD_DIM), F32), jnp.zeros((ATTN_TQ, 1), F32)) for _ in heads)
    carry = tile(qi, carry, col < row)
    carry = lax.fori_loop(0, qi, lambda t, c: tile(qi - 1 - t, c, None), carry)
    for hh in heads:
        o_ref[:, cols[hh]] = carry[hh][0].astype(o_ref.dtype)


def _attn_prompt_call(q, k, v, b_sb, *, batch, seq):
    m, width = q.shape
    n_heads = width // HEAD_DIM
    nq = seq // ATTN_TQ
    gw = ATTN_HEADS * HEAD_DIM
    q_spec = pl.BlockSpec((ATTN_TQ, gw), lambda b, h, i: (b * nq + i, h))
    kv_spec = pl.BlockSpec((seq, gw), lambda b, h, i: (b, h))
    return pl.pallas_call(
        _attn_prompt_kernel,
        grid=(batch, n_heads // ATTN_HEADS, nq),
        in_specs=[pl.BlockSpec(memory_space=pltpu.SMEM), q_spec, kv_spec, kv_spec],
        out_specs=q_spec,
        out_shape=jax.ShapeDtypeStruct((m, width), BF16),
        compiler_params=_cparams(3),
        name="attn_prompt",
    )(b_sb, q, k, v)


def _attn_sample_kernel(pt_ref, bias_ref, q_ref, knew_ref, vnew_ref, *rest, n_heads, n_new):
    del pt_ref
    pages = PAGES_PER_STEP
    k_refs = rest[:pages]
    v_refs = rest[pages:2 * pages]
    o_ref = rest[2 * pages]
    kpad_ref, vpad_ref, acc_ref, later_ref = rest[2 * pages + 1:]
    g = pl.program_id(1)
    page_len = kpad_ref.shape[0] // n_heads
    suffix_mat = _suffix_sum_matrix(page_len)
    bias = jnp.concatenate([jnp.full((n_new, 1), bias_ref[h], F32) for h in range(n_heads)], axis=0)
    nt = (((1,), (1,)), ((), ()))

    def visit(k_pages, v_pages, mask):
        qh = [q_ref[0, :, h * HEAD_DIM:(h + 1) * HEAD_DIM].astype(BF16) for h in range(n_heads)]
        zs = []
        for kp in k_pages:
            z = [lax.dot_general(qh[h], kp[pl.ds(h, page_len, stride=n_heads), :].astype(BF16), nt,
                                 preferred_element_type=F32) for h in range(n_heads)]
            zs.append(jnp.concatenate(z, axis=0) + bias)
        cums = [_sb_mass(z, suffix_mat, mask) for z in zs]
        later = later_ref[...]
        ws = []
        for z, c in zip(zs, cums):
            w, later = _sb_weights(z, c, later, mask)
            ws.append(w)
        later_ref[...] = later
        acc = [acc_ref[h] for h in range(n_heads)]
        for w, vp in zip(ws, v_pages):
            for h in range(n_heads):
                wh = w[h * n_new:(h + 1) * n_new].astype(BF16)
                vh = vp[pl.ds(h, page_len, stride=n_heads), :].astype(BF16)
                acc[h] = acc[h] + jnp.dot(wh, vh, preferred_element_type=F32)
        for h in range(n_heads):
            acc_ref[h] = acc[h]

    @pl.when(g == 0)
    def _():
        acc_ref[...] = jnp.zeros_like(acc_ref)
        later_ref[...] = jnp.zeros_like(later_ref)
        kpad_ref[...] = jnp.zeros_like(kpad_ref)
        vpad_ref[...] = jnp.zeros_like(vpad_ref)
        kpad_ref[0:n_new * n_heads, :] = knew_ref[...]
        vpad_ref[0:n_new * n_heads, :] = vnew_ref[...]
        row = lax.broadcasted_iota(jnp.int32, (n_heads * n_new, page_len), 0) % n_new
        col = lax.broadcasted_iota(jnp.int32, (n_heads * n_new, page_len), 1)
        visit([kpad_ref], [vpad_ref], col < row)

    visit([r.at[0] for r in k_refs], [r.at[0] for r in v_refs], None)

    @pl.when(g == pl.num_programs(1) - 1)
    def _():
        for h in range(n_heads):
            o_ref[0, :, h * HEAD_DIM:(h + 1) * HEAD_DIM] = acc_ref[h].astype(o_ref.dtype)


def _attn_sample_call(q3, knew, vnew, cache_k, cache_v, page_table, b_sb):
    bsz, n_new, width = q3.shape
    n_heads = width // HEAD_DIM
    n_pages = page_table.shape[1]
    page_rows = cache_k.shape[1]
    pages = PAGES_PER_STEP
    steps = n_pages // pages

    def page_spec(p):
        def index(b, g, pt):
            return (pt[b, n_pages - 1 - (g * pages + p)], 0, 0)
        return pl.BlockSpec((1, page_rows, HEAD_DIM), index)

    new_spec = pl.BlockSpec((n_new * n_heads, HEAD_DIM), lambda b, g, pt: (b, 0))
    q_spec = pl.BlockSpec((1, n_new, width), lambda b, g, pt: (b, 0, 0))
    grid_spec = pltpu.PrefetchScalarGridSpec(
        num_scalar_prefetch=1,
        grid=(bsz, steps),
        in_specs=([pl.BlockSpec(memory_space=pltpu.SMEM), q_spec, new_spec, new_spec]
                  + [page_spec(p) for p in range(pages)] * 2),
        out_specs=q_spec,
        scratch_shapes=[pltpu.VMEM((page_rows, HEAD_DIM), F32),
                        pltpu.VMEM((page_rows, HEAD_DIM), F32),
                        pltpu.VMEM((n_heads, n_new, HEAD_DIM), F32),
                        pltpu.VMEM((n_heads * n_new, 1), F32)],
    )
    return pl.pallas_call(
        functools.partial(_attn_sample_kernel, n_heads=n_heads, n_new=n_new),
        grid_spec=grid_spec,
        out_shape=jax.ShapeDtypeStruct((bsz, n_new, width), F32),
        compiler_params=_cparams(2),
        name="attn_sample",
    )(page_table, b_sb, q3, knew, vnew, *([cache_k] * pages), *([cache_v] * pages))


def _segmented_linear_scan(a, b, seg_len):
    rows = lax.broadcasted_iota(jnp.int32, a.shape, 0) % seg_len
    d = 1
    while d < seg_len:
        keep = rows >= d
        a_prev = pltpu.roll(a, d, 0)
        b_prev = pltpu.roll(b, d, 0)
        b = jnp.where(keep, a * b_prev + b, b)
        a = jnp.where(keep, a * a_prev, a)
        d *= 2
    return a, b


def _mixer_kernel(hn_ref, attn_ref, x_ref, gt_ref, sh_ref, sc_ref, g2_ref,
                  wl_ref, cw_ref, cb_ref, wa_ref, ba_ref, wx_ref, bx_ref, lam_ref, prev_ref, h0_ref, wo_ref,
                  x1_ref, hn2_ref, hlast_ref, xtail_ref, xp_ref, hcar_ref,
                  *, groups, rows, blocks_per_seq, fresh, n_sub):
    i = pl.program_id(0)
    width = xp_ref.shape[-1]
    d = x_ref.shape[-1]
    first_block = (i % blocks_per_seq) == 0
    pad = SUBLANES
    sub_rows = rows // n_sub
    sub = groups * sub_rows
    row_slices = [slice(s * sub, (s + 1) * sub) for s in range(n_sub)]

    @pl.when(first_block)
    def _():
        if fresh:
            xp_ref[:, 0:pad, :] = jnp.zeros((groups, pad, width), F32)
            hcar_ref[...] = jnp.zeros_like(hcar_ref)
        else:
            xp_ref[:, 0:pad, :] = prev_ref[...]
            hcar_ref[...] = h0_ref[...]

    xgs = [jnp.dot(hn_ref[rs, :], wl_ref[...], preferred_element_type=F32) for rs in row_slices]
    mixes = [jnp.dot(attn_ref[rs, :].astype(BF16), wo_ref[0:width, :], preferred_element_type=F32)
             for rs in row_slices]
    for s in range(n_sub):
        xp_ref[:, pad + s * sub_rows:pad + (s + 1) * sub_rows, :] = (
            xgs[s][:, 0:width].reshape(groups, sub_rows, width))

    cw = cw_ref[...]
    bw = width // LRU_BLOCKS
    neg_c_softplus = (-LRU_C) * _softplus(-lam_ref[...])
    h_prev = hcar_ref[...] if rows == SUBLANES else hcar_ref[0]
    for s, rs in enumerate(row_slices):
        t0 = s * sub_rows
        xc = cb_ref[...] + cw[0:1] * xp_ref[:, pl.ds(pad - 3 + t0, sub_rows), :]
        for t in range(1, LRU_CONV_W):
            xc = xc + cw[t:t + 1] * xp_ref[:, pl.ds(pad - 3 + t0 + t, sub_rows), :]
        xc = xc.reshape(sub, width)

        xcb = xc.astype(BF16)
        r_parts, i_parts = [], []
        for n in range(LRU_BLOCKS):
            xb = xcb[:, n * bw:(n + 1) * bw]
            r_parts.append(jnp.dot(xb, wa_ref[n], preferred_element_type=F32))
            i_parts.append(jnp.dot(xb, wx_ref[n], preferred_element_type=F32))
        r = jax.nn.sigmoid(jnp.concatenate(r_parts, axis=1) + ba_ref[...])
        ig = jax.nn.sigmoid(jnp.concatenate(i_parts, axis=1) + bx_ref[...])
        log_a = r * neg_c_softplus
        a = jnp.exp(log_a)
        one_minus_a2 = -jnp.tanh(log_a) * (a * a + 1.0)
        mult = jnp.where(one_minus_a2 > 0.0, one_minus_a2 * lax.rsqrt(one_minus_a2), 0.0)
        if fresh and s == 0:
            t_row = lax.broadcasted_iota(jnp.int32, (sub, 1), 0)
            start_row = jnp.where(first_block, 0, -1)
            mult = jnp.where(t_row == start_row, 1.0, mult)
        b = mult * ig * xc

        a_cum, h = _segmented_linear_scan(a, b, SUBLANES)
        if rows == SUBLANES:
            h = h + a_cum * jnp.broadcast_to(h_prev, (groups, rows, width)).reshape(sub, width)
            h_prev = h.reshape(groups, rows, width)[:, rows - 1:rows, :]
        else:
            tiles = []
            for t in range(sub_rows // SUBLANES):
                ts = slice(t * SUBLANES, (t + 1) * SUBLANES)
                h_t = h[ts] + a_cum[ts] * h_prev
                tiles.append(h_t)
                h_prev = h_t[SUBLANES - 1:SUBLANES]
            h = jnp.concatenate(tiles, axis=0)
        lru = (h * _gelu_tanh(xgs[s][:, width:2 * width])).astype(BF16)

        mix = mixes[s] + jnp.dot(lru, wo_ref[width:2 * width, :], preferred_element_type=F32)
        x1 = x_ref[:, t0:t0 + sub_rows, :] + gt_ref[...] * mix.reshape(groups, sub_rows, d)
        x1_ref[:, t0:t0 + sub_rows, :] = x1
        ms = jnp.mean(x1 * x1, axis=-1, keepdims=True)
        y = x1 * lax.rsqrt(ms + EPS) * g2_ref[...]
        hn2 = y * (1.0 + sc_ref[...]) + sh_ref[...]
        hn2_ref[rs, :] = hn2.reshape(sub, d).astype(BF16)

    h_last = h_prev if rows == SUBLANES else h_prev[None]
    hcar_ref[...] = h_last
    hlast_ref[...] = h_last
    tail = xp_ref[:, rows:rows + pad, :]
    xtail_ref[...] = tail
    xp_ref[:, 0:pad, :] = tail


def _mixer_call(hn, attn, x3, gt, sh, sc, g2, w_lru_bf, cw, cb, wa_bf, ba, wx_bf, bx, lam, prev_pad, h0,
                w_out_bf, *, n_seq, groups_per_block, mod_index, fresh):
    n_groups, rows, d = x3.shape
    g = groups_per_block
    tm = g * rows
    m = n_groups * rows
    n_blocks = n_groups // g
    blocks_per_seq = n_blocks // (n_seq // g)
    width = attn.shape[1]
    bw = width // LRU_BLOCKS
    resident = pl.Buffered(1)
    x_spec = pl.BlockSpec((g, rows, d), lambda i: (i, 0, 0))
    mod_spec = pl.BlockSpec((g, 1, d), lambda i: (mod_index(i), 0, 0))
    vec_spec = pl.BlockSpec((1, width), lambda i: (0, 0))
    gate_spec = pl.BlockSpec((LRU_BLOCKS, bw, bw), lambda i: (0, 0, 0))
    seq_index = lambda i: (i // blocks_per_seq, 0, 0)
    return pl.pallas_call(
        functools.partial(_mixer_kernel, groups=g, rows=rows, blocks_per_seq=blocks_per_seq, fresh=fresh,
                          n_sub=MIXER_SUB_BLOCKS if g == 1 else 1),
        grid=(n_blocks,),
        in_specs=[pl.BlockSpec((tm, d), lambda i: (i, 0)),
                  pl.BlockSpec((tm, width), lambda i: (i, 0)),
                  x_spec, mod_spec, mod_spec, mod_spec,
                  pl.BlockSpec((1, d), lambda i: (0, 0)),
                  pl.BlockSpec((d, 2 * width), lambda i: (0, 0), pipeline_mode=resident),
                  pl.BlockSpec((LRU_CONV_W, width), lambda i: (0, 0)), vec_spec,
                  gate_spec, vec_spec, gate_spec, vec_spec, vec_spec,
                  pl.BlockSpec((g, SUBLANES, width), seq_index),
                  pl.BlockSpec((g, 1, width), seq_index),
                  pl.BlockSpec((d, d), lambda i: (0, 0), pipeline_mode=resident)],
        out_specs=[x_spec,
                   pl.BlockSpec((tm, d), lambda i: (i, 0)),
                   pl.BlockSpec((g, 1, width), seq_index),
                   pl.BlockSpec((g, SUBLANES, width), lambda i: (i, 0, 0))],
        out_shape=[jax.ShapeDtypeStruct((n_groups, rows, d), F32),
                   jax.ShapeDtypeStruct((m, d), BF16),
                   jax.ShapeDtypeStruct((n_seq, 1, width), F32),
                   jax.ShapeDtypeStruct((n_groups, SUBLANES, width), F32)],
        scratch_shapes=[pltpu.VMEM((g, rows + SUBLANES, width), F32),
                        pltpu.VMEM((g, 1, width), F32)],
        compiler_params=_cparams(1),
        name="mixer",
    )(hn, attn, x3, gt, sh, sc, g2.reshape(1, d), w_lru_bf, cw, cb.reshape(1, width), wa_bf,
      ba.reshape(1, width), wx_bf, bx.reshape(1, width), lam.reshape(1, width), prev_pad, h0, w_out_bf)


def _ffn_kernel(hn_ref, wg_ref, wv_ref, cwg_ref, cwv_ref, cbg_ref, cbv_ref, wd_ref,
                x1_ref, gt_ref, pg_ref, pv_ref,
                y_ref, tg_ref, tv_ref, acc_ref, cg_ref, cv_ref,
                *, groups, rows, blocks_per_seq, fresh):
    i = pl.program_id(0)
    j = pl.program_id(1)
    tm = groups * rows
    tf = wg_ref.shape[1]
    first_block = (i % blocks_per_seq) == 0
    pad = SUBLANES
    if not fresh and blocks_per_seq != 1:
        raise NotImplementedError("continuing sequences must fit one row block")

    @pl.when(j == 0)
    def _():
        acc_ref[...] = jnp.zeros_like(acc_ref)

    @pl.when(jnp.logical_and(i == 0, j == 0))
    def _():
        cg_ref[...] = jnp.zeros_like(cg_ref)
        cv_ref[...] = jnp.zeros_like(cv_ref)

    n_sub = FFN_SUB_BLOCKS if groups == 1 else 1
    sub = tm // n_sub
    row_slices = [slice(s * sub, (s + 1) * sub) for s in range(n_sub)]
    ups = []
    for rs in row_slices:
        hn = hn_ref[rs, :]
        ups.append((jnp.dot(hn, wg_ref[...], preferred_element_type=F32),
                    jnp.dot(hn, wv_ref[...], preferred_element_type=F32)))

    def boundary_rows(p_ref, carry_ref):
        if fresh:
            prev = jnp.where(first_block, 0.0, carry_ref[j])
            return prev[:, pad - 2:pad - 1, :], prev[:, pad - 1:pad, :]
        return p_ref[:, 0:1, :], p_ref[:, 1:2, :]

    def conv(u, p0, p1, cw_ref, cb_ref):
        u1 = pltpu.roll(u, 1, 0)
        u2 = pltpu.roll(u, 2, 0)
        if groups == 1:
            top = lax.broadcasted_iota(jnp.int32, (pad, tf), 0)
            u1_top = jnp.where(top == 0, p1, u1[0:pad])
            u2_top = jnp.where(top == 0, p0, jnp.where(top == 1, p1, u2[0:pad]))
            u1 = jnp.concatenate([u1_top, u1[pad:]], axis=0)
            u2 = jnp.concatenate([u2_top, u2[pad:]], axis=0)
        else:
            t_row = lax.broadcasted_iota(jnp.int32, (tm, tf), 0) % rows
            p1r = jnp.broadcast_to(p1, (groups, rows, tf)).reshape(tm, tf)
            p0r = jnp.broadcast_to(p0, (groups, rows, tf)).reshape(tm, tf)
            u1 = jnp.where(t_row == 0, p1r, u1)
            u2 = jnp.where(t_row == 0, p0r, jnp.where(t_row == 1, p1r, u2))
        cw = cw_ref[...]
        uc = cb_ref[...] + cw[0:1] * u2
        uc = uc + cw[1:2] * u1
        return uc + cw[2:3] * u

    pg = boundary_rows(pg_ref, cg_ref)
    pv = boundary_rows(pv_ref, cv_ref)
    if groups == 1:
        pg = (pg[0][0], pg[1][0])
        pv = (pv[0][0], pv[1][0])
    for s, rs in enumerate(row_slices):
        ug, uv = ups[s]
        gated = (_gelu_tanh(conv(ug, pg[0], pg[1], cwg_ref, cbg_ref))
                 * conv(uv, pv[0], pv[1], cwv_ref, cbv_ref)).astype(BF16)
        acc_ref[rs, :] += jnp.dot(gated, wd_ref[...], preferred_element_type=F32)
        pg = (ug[sub - 2:sub - 1], ug[sub - 1:sub])
        pv = (uv[sub - 2:sub - 1], uv[sub - 1:sub])

    ug3 = ups[-1][0].reshape(groups, rows // n_sub, tf)
    uv3 = ups[-1][1].reshape(groups, rows // n_sub, tf)
    last = rows // n_sub
    cg_ref[j] = ug3[:, last - pad:last, :]
    cv_ref[j] = uv3[:, last - pad:last, :]
    tg_ref[...] = ug3[:, last - 2:last, :]
    tv_ref[...] = uv3[:, last - 2:last, :]

    @pl.when(j == pl.num_programs(1) - 1)
    def _():
        y_ref[...] = x1_ref[...] + gt_ref[...] * acc_ref[...].reshape(y_ref.shape)


def _ffn_call(hn2, w_up_bf, cw, cb, w_down_bf, x1, gt, prev_g, prev_v, *,
              n_seq, groups_per_block, mod_index, fresh):
    n_groups, rows, d = x1.shape
    g = groups_per_block
    tm = g * rows
    n_blocks = n_groups // g
    blocks_per_seq = n_blocks // (n_seq // g)
    d_ff = w_down_bf.shape[0]
    tf = FFN_TF
    nj = d_ff // tf
    seq_index = lambda i: i // blocks_per_seq
    x_spec = pl.BlockSpec((g, rows, d), lambda i, j: (i, 0, 0))
    prev_spec = pl.BlockSpec((g, FFN_CONV_W - 1, tf), lambda i, j: (seq_index(i), 0, j))
    tail_spec = pl.BlockSpec((g, FFN_CONV_W - 1, tf), lambda i, j: (i, 0, j))
    tail_out = jax.ShapeDtypeStruct((n_groups, FFN_CONV_W - 1, d_ff), F32)
    y, tail_g, tail_v = pl.pallas_call(
        functools.partial(_ffn_kernel, groups=g, rows=rows, blocks_per_seq=blocks_per_seq, fresh=fresh),
        grid=(n_blocks, nj),
        in_specs=[pl.BlockSpec((tm, d), lambda i, j: (i, 0)),
                  pl.BlockSpec((d, tf), lambda i, j: (0, j)),
                  pl.BlockSpec((d, tf), lambda i, j: (0, j + nj)),
                  pl.BlockSpec((FFN_CONV_W, tf), lambda i, j: (0, j)),
                  pl.BlockSpec((FFN_CONV_W, tf), lambda i, j: (0, j + nj)),
                  pl.BlockSpec((1, tf), lambda i, j: (0, j)),
                  pl.BlockSpec((1, tf), lambda i, j: (0, j + nj)),
                  pl.BlockSpec((tf, d), lambda i, j: (j, 0)),
                  x_spec,
                  pl.BlockSpec((g, 1, d), lambda i, j: (mod_index(i), 0, 0)),
                  prev_spec, prev_spec],
        out_specs=[x_spec, tail_spec, tail_spec],
        out_shape=[jax.ShapeDtypeStruct((n_groups, rows, d), F32), tail_out, tail_out],
        scratch_shapes=[pltpu.VMEM((tm, d), F32),
                        pltpu.VMEM((nj, g, SUBLANES, tf), F32),
                        pltpu.VMEM((nj, g, SUBLANES, tf), F32)],
        compiler_params=_cparams(2),
        name="convffn",
    )(hn2, w_up_bf, w_up_bf, cw, cw, cb.reshape(1, 2 * d_ff), cb.reshape(1, 2 * d_ff), w_down_bf,
      x1, gt, prev_g, prev_v)
    last = lambda t: t.reshape(n_seq, n_groups // n_seq, FFN_CONV_W - 1, d_ff)[:, -1]
    return y, last(tail_g), last(tail_v)


def _run_group(x, mod, attn_fn, lru_prev_pad, lru_h0, ffn_prev, fresh, groups_per_block, rows_per_block, p):
    n_seq, t, d = x.shape
    rows = rows_per_block // groups_per_block
    x3 = x.reshape(n_seq * t // rows, rows, d)
    blocks_per_seq = max(1, t // rows_per_block)
    if groups_per_block == 1:
        mod_index = lambda i: i // blocks_per_seq
    else:
        mod_index = lambda i: i
    mods = [mod[:, n:n + 1, :] for n in range(N_MOD)]
    sh1, sc1, gt1, sh2, sc2, gt2 = mods

    q, kf, kb, vf, vb, hn = _inproj_call(
        x3, sh1, sc1, p["g_norm1"], p["w_qkv"], p["g_q"], p["g_k"],
        groups_per_block=groups_per_block, mod_index=mod_index)
    attn = attn_fn(q, kf, kb, vf, vb)
    x1, hn2, h_last, xl_tail = _mixer_call(
        hn, attn, x3, gt1, sh2, sc2, p["g_norm2"], p["w_lru_in"], p["w_lru_conv"], p["b_lru_conv"],
        p["w_lru_a"], p["b_lru_a"], p["w_lru_x"], p["b_lru_x"], p["lru_lambda"], lru_prev_pad, lru_h0,
        p["w_out"], n_seq=n_seq, groups_per_block=groups_per_block, mod_index=mod_index, fresh=fresh)
    d_ff = p["w_down"].shape[0]
    y, tail_g, tail_v = _ffn_call(
        hn2, p["w_up"], p["w_ffn_conv"], p["b_ffn_conv"], p["w_down"], x1, gt2,
        ffn_prev[:, :, :d_ff], ffn_prev[:, :, d_ff:],
        n_seq=n_seq, groups_per_block=groups_per_block, mod_index=mod_index, fresh=fresh)
    width = xl_tail.shape[-1]
    lru_conv_new = xl_tail.reshape(n_seq, -1, SUBLANES, width)[:, -1, SUBLANES - (LRU_CONV_W - 1):, :]
    ffn_conv_new = jnp.concatenate([tail_g, tail_v], axis=-1)
    return (y.reshape(n_seq, t, d), kf, vf, lru_conv_new, h_last.reshape(n_seq, width), ffn_conv_new)


def kernel(x_prompt, x_sample, c_prompt, c_sample, cache_k, cache_v, page_table, state_lru_conv, state_lru_h, state_ffn_conv, w_ada, b_ada, g_norm1, w_in, g_q, g_k, b_sb, w_lru_conv, b_lru_conv, w_lru_a, b_lru_a, w_lru_x, b_lru_x, lru_lambda, w_out, g_norm2, w_up, w_ffn_conv, b_ffn_conv, w_down):
    depth = w_ada.shape[0]
    assert depth == 1, "single-layer step"
    l = 0
    bsz, seq, d = x_prompt.shape
    dec_b, dec_t, _ = x_sample.shape
    page, n_heads = cache_k.shape[2], cache_k.shape[3]
    width = n_heads * HEAD_DIM
    lru_w = state_lru_h.shape[-1]
    d_ff2 = state_ffn_conv.shape[-1]

    p = dict(
        g_norm1=g_norm1[l], w_qkv=w_in[l][:, :3 * width].astype(BF16),
        w_lru_in=w_in[l][:, 3 * width:].astype(BF16), g_q=g_q[l], g_k=g_k[l],
        w_lru_conv=w_lru_conv[l], b_lru_conv=b_lru_conv[l],
        w_lru_a=w_lru_a[l].astype(BF16), b_lru_a=b_lru_a[l],
        w_lru_x=w_lru_x[l].astype(BF16), b_lru_x=b_lru_x[l], lru_lambda=lru_lambda[l],
        w_out=w_out[l].astype(BF16), g_norm2=g_norm2[l], w_up=w_up[l].astype(BF16),
        w_ffn_conv=w_ffn_conv[l], b_ffn_conv=b_ffn_conv[l], w_down=w_down[l].astype(BF16))

    n_c = bsz + dec_b
    pad_c = (-n_c) % SUBLANES
    c_all = jnp.concatenate([c_prompt, c_sample, jnp.zeros((pad_c, d), F32)], axis=0)
    mod = _ada_call(c_all, w_ada[l], b_ada[l])
    mod_p = mod[:bsz].reshape(bsz, N_MOD, d)
    mod_s = mod[bsz:n_c].reshape(dec_b, N_MOD, d)

    def prompt_attn(q, kf, kb, vf, vb):
        return _attn_prompt_call(q, kb, vb, b_sb[l], batch=bsz, seq=seq)

    zeros_prev = jnp.zeros((bsz, SUBLANES, lru_w), F32)
    zeros_h = jnp.zeros((bsz, 1, lru_w), F32)
    zeros_ffn = jnp.zeros((bsz, FFN_CONV_W - 1, d_ff2), F32)
    yp, kp, vp, lcp, lhp, fcp = _run_group(
        x_prompt, mod_p, prompt_attn, zeros_prev, zeros_h, zeros_ffn, True, 1, ROW_TILE, p)

    def sample_attn(q, kf, kb, vf, vb):
        q3 = q.astype(F32).reshape(dec_b, dec_t, width)
        pool = cache_k.shape[1]
        ck = cache_k[l].reshape(pool, page * n_heads, HEAD_DIM)
        cv = cache_v[l].reshape(pool, page * n_heads, HEAD_DIM)
        o = _attn_sample_call(q3, kf, vf, ck, cv, page_table, b_sb[l])
        return o.reshape(dec_b * dec_t, width)

    prev_pad = jnp.concatenate(
        [jnp.zeros((dec_b, SUBLANES - (LRU_CONV_W - 1), lru_w), F32), state_lru_conv[l]], axis=1)
    ys, ks, vs, lcs, lhs, fcs = _run_group(
        x_sample, mod_s, sample_attn, prev_pad, state_lru_h[l].reshape(dec_b, 1, lru_w),
        state_ffn_conv[l], False, dec_b, dec_b * dec_t, p)

    n_pages_p = seq // page
    return (yp, ys,
            kp.reshape(1, bsz, n_pages_p, page, n_heads, HEAD_DIM),
            vp.reshape(1, bsz, n_pages_p, page, n_heads, HEAD_DIM),
            ks.reshape(1, dec_b, dec_t, n_heads, HEAD_DIM),
            vs.reshape(1, dec_b, dec_t, n_heads, HEAD_DIM),
            lcp[None], lcs[None], lhp[None], lhs[None], fcp[None], fcs[None])
```

```python
import functools
import math

import jax
import jax.numpy as jnp
from jax import lax
from jax.experimental import pallas as pl
from jax.experimental.pallas import tpu as pltpu

F32 = jnp.float32
BF16 = jnp.bfloat16

EPS = 1e-6
LRU_C = 8.0
HEAD_DIM = 128
LRU_BLOCKS = 8
LRU_CONV_W = 4
FFN_CONV_W = 3
N_MOD = 6
SB_SCALE = 1.0 / math.sqrt(HEAD_DIM)

SUBLANES = 8
VMEM_LIMIT = 56 * 1024 * 1024

ROW_TILE = 512
INPROJ_SUB_BLOCKS = 2
MIXER_SUB_BLOCKS = 2
ATTN_TQ = 256
ATTN_TK = 256
ATTN_HEADS = 8
ADA_TN = 1024
FFN_TF = 1024
FFN_TF_SAMPLE = 256
FFN_SUB_BLOCKS = 2
PAGES_PER_STEP = 8

def _cparams(n_axes):
    return pltpu.CompilerParams(dimension_semantics=("arbitrary",) * n_axes,
                                vmem_limit_bytes=VMEM_LIMIT)


def _softplus(z):
    return jnp.maximum(z, 0.0) + jnp.log1p(jnp.exp(-jnp.abs(z)))


def _softplus_logits(z):
    return jnp.maximum(z, 0.0) + jnp.log(1.0 + jnp.exp(-jnp.abs(z)))


def _gelu_tanh(x):
    c = math.sqrt(2.0 / math.pi)
    return x * (0.5 * (1.0 + jnp.tanh(c * (x + 0.044715 * (x * x * x)))))


def _suffix_sum_matrix(n_keys):
    j = lax.broadcasted_iota(jnp.int32, (n_keys, n_keys), 0)
    s = lax.broadcasted_iota(jnp.int32, (n_keys, n_keys), 1)
    return jnp.where(j >= s, 1.0, 0.0).astype(BF16)


def _ada_kernel(c_ref, w_ref, b_ref, o_ref):
    c = c_ref[...]
    s = (c * jax.nn.sigmoid(c)).astype(BF16)
    o_ref[...] = jnp.dot(s, w_ref[...].astype(BF16), preferred_element_type=F32) + b_ref[...]


def _ada_call(c_all, w_ada, b_ada):
    m, d = c_all.shape
    n = w_ada.shape[1]
    return pl.pallas_call(
        _ada_kernel,
        grid=(n // ADA_TN,),
        in_specs=[pl.BlockSpec((m, d), lambda j: (0, 0)),
                  pl.BlockSpec((d, ADA_TN), lambda j: (0, j)),
                  pl.BlockSpec((1, ADA_TN), lambda j: (0, j))],
        out_specs=pl.BlockSpec((m, ADA_TN), lambda j: (0, j)),
        out_shape=jax.ShapeDtypeStruct((m, n), F32),
        compiler_params=_cparams(1),
        name="adaln",
    )(c_all, w_ada, b_ada.reshape(1, n))


def _head_rmsnorm(a, g, n_heads):
    outs = []
    for h in range(n_heads):
        ah = a[:, h * HEAD_DIM:(h + 1) * HEAD_DIM]
        ms = jnp.mean(ah * ah, axis=-1, keepdims=True)
        outs.append(ah * lax.rsqrt(ms + EPS) * g)
    return outs


def _inproj_kernel(x_ref, sh_ref, sc_ref, g1_ref, w_ref, gq_ref, gk_ref,
                   q_ref, kf_ref, kb_ref, vf_ref, vb_ref, hn_ref, *, n_heads, n_sub):
    tm, d = hn_ref.shape
    width = n_heads * HEAD_DIM
    sub = tm // n_sub
    row_slices = [slice(s * sub, (s + 1) * sub) for s in range(n_sub)]
    head_cols = [slice(h * HEAD_DIM, (h + 1) * HEAD_DIM) for h in range(n_heads)]

    def normalize(rs):
        x = x_ref[...] if n_sub == 1 else x_ref[:, rs, :]
        ms = jnp.mean(x * x, axis=-1, keepdims=True)
        y = x * lax.rsqrt(ms + EPS) * g1_ref[...]
        hn = y * (1.0 + sc_ref[...]) + sh_ref[...]
        hn_ref[rs, :] = hn.reshape(sub, d).astype(BF16)

    def project(t, epilogue):
        w = w_ref[:, t * width:(t + 1) * width]
        accs = [jnp.dot(hn_ref[rs, :], w, preferred_element_type=F32) for rs in row_slices]
        for rs, acc in zip(row_slices, accs):
            epilogue(rs, acc)

    def head_rows(rs, h):
        return pl.ds(rs.start * n_heads + h, sub, stride=n_heads)

    def q_epilogue(rs, acc):
        for h, qh in enumerate(_head_rmsnorm(acc, gq_ref[...], n_heads)):
            q_ref[rs, head_cols[h]] = (qh * SB_SCALE).astype(BF16)

    def k_epilogue(rs, acc):
        for h, kh in enumerate(_head_rmsnorm(acc, gk_ref[...], n_heads)):
            kf_ref[head_rows(rs, h), :] = kh
            kb_ref[rs, head_cols[h]] = kh.astype(BF16)

    def v_epilogue(rs, acc):
        for h in range(n_heads):
            vf_ref[head_rows(rs, h), :] = acc[:, head_cols[h]]
        vb_ref[rs, :] = acc.astype(BF16)

    for rs in row_slices:
        normalize(rs)
    for t, epilogue in enumerate([q_epilogue, k_epilogue, v_epilogue]):
        project(t, epilogue)


def _inproj_call(x3, sh, sc, g1, w_qkv_bf, g_q, g_k, *, groups_per_block, mod_index):
    n_groups, rows, d = x3.shape
    g = groups_per_block
    tm = g * rows
    m = n_groups * rows
    n_blocks = n_groups // g
    width = w_qkv_bf.shape[1] // 3
    n_heads = width // HEAD_DIM
    row_spec = pl.BlockSpec((tm, width), lambda i: (i, 0))
    head_spec = pl.BlockSpec((tm * n_heads, HEAD_DIM), lambda i: (i, 0))
    mod_spec = pl.BlockSpec((g, 1, d), lambda i: (mod_index(i), 0, 0))
    head_out = jax.ShapeDtypeStruct((m * n_heads, HEAD_DIM), F32)
    bf_out = jax.ShapeDtypeStruct((m, width), BF16)
    return pl.pallas_call(
        functools.partial(_inproj_kernel, n_heads=n_heads, n_sub=INPROJ_SUB_BLOCKS if g == 1 else 1),
        grid=(n_blocks,),
        in_specs=[pl.BlockSpec((g, rows, d), lambda i: (i, 0, 0)),
                  mod_spec, mod_spec,
                  pl.BlockSpec((1, d), lambda i: (0, 0)),
                  pl.BlockSpec((d, 3 * width), lambda i: (0, 0), pipeline_mode=pl.Buffered(1)),
                  pl.BlockSpec((1, HEAD_DIM), lambda i: (0, 0)),
                  pl.BlockSpec((1, HEAD_DIM), lambda i: (0, 0))],
        out_specs=[row_spec, head_spec, row_spec, head_spec, row_spec,
                   pl.BlockSpec((tm, d), lambda i: (i, 0))],
        out_shape=[bf_out, head_out, bf_out, head_out, bf_out, jax.ShapeDtypeStruct((m, d), BF16)],
        compiler_params=_cparams(1),
        name="inproj",
    )(x3, sh, sc, g1.reshape(1, d), w_qkv_bf, g_q.reshape(1, HEAD_DIM), g_k.reshape(1, HEAD_DIM))


def _sb_mass(z, suffix_mat, mask):
    sp = _softplus_logits(z)
    if mask is not None:
        sp = jnp.where(mask, sp, 0.0)
    return jnp.dot(sp.astype(BF16), suffix_mat, preferred_element_type=F32)


def _sb_weights(z, cums, later, mask):
    w = jnp.exp(z - cums - later)
    if mask is not None:
        w = jnp.where(mask, w, 0.0)
    return w, later + cums[:, 0:1]


def _attn_prompt_kernel(bias_ref, q_ref, k_ref, v_ref, o_ref):
    hg = pl.program_id(1)
    qi = pl.program_id(2)
    suffix_mat = _suffix_sum_matrix(ATTN_TK)
    heads = range(ATTN_HEADS)
    cols = [slice(hh * HEAD_DIM, (hh + 1) * HEAD_DIM) for hh in heads]
    biases = [bias_ref[hg * ATTN_HEADS + hh] for hh in heads]
    qs = [q_ref[:, c] for c in cols]

    def tile(j, carry, mask):
        start = pl.multiple_of(j * ATTN_TK, ATTN_TK)
        nt = (((1,), (1,)), ((), ()))
        zs = [lax.dot_general(qs[hh], k_ref[pl.ds(start, ATTN_TK), cols[hh]], nt,
                              preferred_element_type=F32) + biases[hh] for hh in heads]
        cums = [_sb_mass(zs[hh], suffix_mat, mask) for hh in heads]
        ws = [_sb_weights(zs[hh], cums[hh], carry[hh][1], mask) for hh in heads]
        return tuple(
            (carry[hh][0] + jnp.dot(ws[hh][0].astype(BF16), v_ref[pl.ds(start, ATTN_TK), cols[hh]],
                                    preferred_element_type=F32), ws[hh][1]) for hh in heads)

    row = lax.broadcasted_iota(jnp.int32, (ATTN_TQ, ATTN_TK), 0)
    col = lax.broadcasted_iota(jnp.int32, (ATTN_TQ, ATTN_TK), 1)
    carry = tuple((jnp.zeros((ATTN_TQ, HEAD_DIM), F32), jnp.zeros((ATTN_TQ, 1), F32)) for _ in heads)
    carry = tile(qi, carry, col < row)
    carry = lax.fori_loop(0, qi, lambda t, c: tile(qi - 1 - t, c, None), carry)
    for hh in heads:
        o_ref[:, cols[hh]] = carry[hh][0].astype(o_ref.dtype)


def _attn_prompt_call(q, k, v, b_sb, *, batch, seq):
    m, width = q.shape
    n_heads = width // HEAD_DIM
    nq = seq // ATTN_TQ
    gw = ATTN_HEADS * HEAD_DIM
    q_spec = pl.BlockSpec((ATTN_TQ, gw), lambda b, h, i: (b * nq + i, h))
    kv_spec = pl.BlockSpec((seq, gw), lambda b, h, i: (b, h))
    return pl.pallas_call(
        _attn_prompt_kernel,
        grid=(batch, n_heads // ATTN_HEADS, nq),
        in_specs=[pl.BlockSpec(memory_space=pltpu.SMEM), q_spec, kv_spec, kv_spec],
        out_specs=q_spec,
        out_shape=jax.ShapeDtypeStruct((m, width), BF16),
        compiler_params=_cparams(3),
        name="attn_prompt",
    )(b_sb, q, k, v)


def _attn_sample_kernel(pt_ref, bias_ref, q_ref, knew_ref, vnew_ref, *rest, n_heads, n_new):
    del pt_ref
    pages = PAGES_PER_STEP
    k_refs = rest[:pages]
    v_refs = rest[pages:2 * pages]
    o_ref = rest[2 * pages]
    kpad_ref, vpad_ref, acc_ref, later_ref = rest[2 * pages + 1:]
    g = pl.program_id(1)
    page_len = kpad_ref.shape[0] // n_heads
    suffix_mat = _suffix_sum_matrix(page_len)
    bias = jnp.concatenate([jnp.full((n_new, 1), bias_ref[h], F32) for h in range(n_heads)], axis=0)
    nt = (((1,), (1,)), ((), ()))

    def visit(k_pages, v_pages, mask):
        qh = [q_ref[0, :, h * HEAD_DIM:(h + 1) * HEAD_DIM].astype(BF16) for h in range(n_heads)]
        zs = []
        for kp in k_pages:
            z = [lax.dot_general(qh[h], kp[pl.ds(h, page_len, stride=n_heads), :].astype(BF16), nt,
                                 preferred_element_type=F32) for h in range(n_heads)]
            zs.append(jnp.concatenate(z, axis=0) + bias)
        cums = [_sb_mass(z, suffix_mat, mask) for z in zs]
        later = later_ref[...]
        ws = []
        for z, c in zip(zs, cums):
            w, later = _sb_weights(z, c, later, mask)
            ws.append(w)
        later_ref[...] = later
        acc = [acc_ref[h] for h in range(n_heads)]
        for w, vp in zip(ws, v_pages):
            for h in range(n_heads):
                wh = w[h * n_new:(h + 1) * n_new].astype(BF16)
                vh = vp[pl.ds(h, page_len, stride=n_heads), :].astype(BF16)
                acc[h] = acc[h] + jnp.dot(wh, vh, preferred_element_type=F32)
        for h in range(n_heads):
            acc_ref[h] = acc[h]

    @pl.when(g == 0)
    def _():
        acc_ref[...] = jnp.zeros_like(acc_ref)
        later_ref[...] = jnp.zeros_like(later_ref)
        kpad_ref[...] = jnp.zeros_like(kpad_ref)
        vpad_ref[...] = jnp.zeros_like(vpad_ref)
        kpad_ref[0:n_new * n_heads, :] = knew_ref[...]
        vpad_ref[0:n_new * n_heads, :] = vnew_ref[...]
        row = lax.broadcasted_iota(jnp.int32, (n_heads * n_new, page_len), 0) % n_new
        col = lax.broadcasted_iota(jnp.int32, (n_heads * n_new, page_len), 1)
        visit([kpad_ref], [vpad_ref], col < row)

    visit([r.at[0] for r in k_refs], [r.at[0] for r in v_refs], None)

    @pl.when(g == pl.num_programs(1) - 1)
    def _():
        for h in range(n_heads):
            o_ref[0, :, h * HEAD_DIM:(h + 1) * HEAD_DIM] = acc_ref[h].astype(o_ref.dtype)


def _attn_sample_call(q3, knew, vnew, cache_k, cache_v, page_table, b_sb):
    bsz, n_new, width = q3.shape
    n_heads = width // HEAD_DIM
    n_pages = page_table.shape[1]
    page_rows = cache_k.shape[1]
    pages = PAGES_PER_STEP
    steps = n_pages // pages

    def page_spec(p):
        def index(b, g, pt):
            return (pt[b, n_pages - 1 - (g * pages + p)], 0, 0)
        return pl.BlockSpec((1, page_rows, HEAD_DIM), index)

    new_spec = pl.BlockSpec((n_new * n_heads, HEAD_DIM), lambda b, g, pt: (b, 0))
    q_spec = pl.BlockSpec((1, n_new, width), lambda b, g, pt: (b, 0, 0))
    grid_spec = pltpu.PrefetchScalarGridSpec(
        num_scalar_prefetch=1,
        grid=(bsz, steps),
        in_specs=([pl.BlockSpec(memory_space=pltpu.SMEM), q_spec, new_spec, new_spec]
                  + [page_spec(p) for p in range(pages)] * 2),
        out_specs=q_spec,
        scratch_shapes=[pltpu.VMEM((page_rows, HEAD_DIM), F32),
                        pltpu.VMEM((page_rows, HEAD_DIM), F32),
                        pltpu.VMEM((n_heads, n_new, HEAD_DIM), F32),
                        pltpu.VMEM((n_heads * n_new, 1), F32)],
    )
    return pl.pallas_call(
        functools.partial(_attn_sample_kernel, n_heads=n_heads, n_new=n_new),
        grid_spec=grid_spec,
        out_shape=jax.ShapeDtypeStruct((bsz, n_new, width), F32),
        compiler_params=_cparams(2),
        name="attn_sample",
    )(page_table, b_sb, q3, knew, vnew, *([cache_k] * pages), *([cache_v] * pages))


def _segmented_linear_scan(a, b, seg_len):
    rows = lax.broadcasted_iota(jnp.int32, a.shape, 0) % seg_len
    d = 1
    while d < seg_len:
        keep = rows >= d
        a_prev = pltpu.roll(a, d, 0)
        b_prev = pltpu.roll(b, d, 0)
        b = jnp.where(keep, a * b_prev + b, b)
        a = jnp.where(keep, a * a_prev, a)
        d *= 2
    return a, b


def _mixer_kernel(hn_ref, attn_ref, x_ref, gt_ref, sh_ref, sc_ref, g2_ref,
                  wl_ref, cw_ref, cb_ref, wa_ref, ba_ref, wx_ref, bx_ref, lam_ref, prev_ref, h0_ref, wo_ref,
                  x1_ref, hn2_ref, hlast_ref, xtail_ref, xp_ref, hcar_ref,
                  *, groups, rows, blocks_per_seq, fresh, n_sub):
    i = pl.program_id(0)
    width = xp_ref.shape[-1]
    d = x_ref.shape[-1]
    first_block = (i % blocks_per_seq) == 0
    pad = SUBLANES
    sub_rows = rows // n_sub
    sub = groups * sub_rows
    row_slices = [slice(s * sub, (s + 1) * sub) for s in range(n_sub)]

    @pl.when(first_block)
    def _():
        if fresh:
            xp_ref[:, 0:pad, :] = jnp.zeros((groups, pad, width), F32)
            hcar_ref[...] = jnp.zeros_like(hcar_ref)
        else:
            xp_ref[:, 0:pad, :] = prev_ref[...]
            hcar_ref[...] = h0_ref[...]

    xgs = [jnp.dot(hn_ref[rs, :], wl_ref[...], preferred_element_type=F32) for rs in row_slices]
    mixes = [jnp.dot(attn_ref[rs, :].astype(BF16), wo_ref[0:width, :], preferred_element_type=F32)
             for rs in row_slices]
    for s in range(n_sub):
        xp_ref[:, pad + s * sub_rows:pad + (s + 1) * sub_rows, :] = (
            xgs[s][:, 0:width].reshape(groups, sub_rows, width))

    cw = cw_ref[...]
    bw = width // LRU_BLOCKS
    neg_c_softplus = (-LRU_C) * _softplus(-lam_ref[...])
    h_prev = hcar_ref[...] if rows == SUBLANES else hcar_ref[0]
    for s, rs in enumerate(row_slices):
        t0 = s * sub_rows
        xc = cb_ref[...] + cw[0:1] * xp_ref[:, pl.ds(pad - 3 + t0, sub_rows), :]
        for t in range(1, LRU_CONV_W):
            xc = xc + cw[t:t + 1] * xp_ref[:, pl.ds(pad - 3 + t0 + t, sub_rows), :]
        xc = xc.reshape(sub, width)

        xcb = xc.astype(BF16)
        r_parts, i_parts = [], []
        for n in range(LRU_BLOCKS):
            xb = xcb[:, n * bw:(n + 1) * bw]
            r_parts.append(jnp.dot(xb, wa_ref[n], preferred_element_type=F32))
            i_parts.append(jnp.dot(xb, wx_ref[n], preferred_element_type=F32))
        r = jax.nn.sigmoid(jnp.concatenate(r_parts, axis=1) + ba_ref[...])
        ig = jax.nn.sigmoid(jnp.concatenate(i_parts, axis=1) + bx_ref[...])
        log_a = r * neg_c_softplus
        a = jnp.exp(log_a)
        one_minus_a2 = -jnp.tanh(log_a) * (a * a + 1.0)
        mult = jnp.where(one_minus_a2 > 0.0, one_minus_a2 * lax.rsqrt(one_minus_a2), 0.0)
        if fresh and s == 0:
            t_row = lax.broadcasted_iota(jnp.int32, (sub, 1), 0)
            start_row = jnp.where(first_block, 0, -1)
            mult = jnp.where(t_row == start_row, 1.0, mult)
        b = mult * ig * xc

        a_cum, h = _segmented_linear_scan(a, b, SUBLANES)
        if rows == SUBLANES:
            h = h + a_cum * jnp.broadcast_to(h_prev, (groups, rows, width)).reshape(sub, width)
            h_prev = h.reshape(groups, rows, width)[:, rows - 1:rows, :]
        else:
            tiles = []
            for t in range(sub_rows // SUBLANES):
                ts = slice(t * SUBLANES, (t + 1) * SUBLANES)
                h_t = h[ts] + a_cum[ts] * h_prev
                tiles.append(h_t)
                h_prev = h_t[SUBLANES - 1:SUBLANES]
            h = jnp.concatenate(tiles, axis=0)
        lru = (h * _gelu_tanh(xgs[s][:, width:2 * width])).astype(BF16)

        mix = mixes[s] + jnp.dot(lru, wo_ref[width:2 * width, :], preferred_element_type=F32)
        x1 = x_ref[:, t0:t0 + sub_rows, :] + gt_ref[...] * mix.reshape(groups, sub_rows, d)
        x1_ref[:, t0:t0 + sub_rows, :] = x1
        ms = jnp.mean(x1 * x1, axis=-1, keepdims=True)
        y = x1 * lax.rsqrt(ms + EPS) * g2_ref[...]
        hn2 = y * (1.0 + sc_ref[...]) + sh_ref[...]
        hn2_ref[rs, :] = hn2.reshape(sub, d).astype(BF16)

    h_last = h_prev if rows == SUBLANES else h_prev[None]
    hcar_ref[...] = h_last
    hlast_ref[...] = h_last
    tail = xp_ref[:, rows:rows + pad, :]
    xtail_ref[...] = tail
    xp_ref[:, 0:pad, :] = tail


def _mixer_call(hn, attn, x3, gt, sh, sc, g2, w_lru_bf, cw, cb, wa_bf, ba, wx_bf, bx, lam, prev_pad, h0,
                w_out_bf, *, n_seq, groups_per_block, mod_index, fresh):
    n_groups, rows, d = x3.shape
    g = groups_per_block
    tm = g * rows
    m = n_groups * rows
    n_blocks = n_groups // g
    blocks_per_seq = n_blocks // (n_seq // g)
    width = attn.shape[1]
    bw = width // LRU_BLOCKS
    resident = pl.Buffered(1)
    x_spec = pl.BlockSpec((g, rows, d), lambda i: (i, 0, 0))
    mod_spec = pl.BlockSpec((g, 1, d), lambda i: (mod_index(i), 0, 0))
    vec_spec = pl.BlockSpec((1, width), lambda i: (0, 0))
    gate_spec = pl.BlockSpec((LRU_BLOCKS, bw, bw), lambda i: (0, 0, 0))
    seq_index = lambda i: (i // blocks_per_seq, 0, 0)
    return pl.pallas_call(
        functools.partial(_mixer_kernel, groups=g, rows=rows, blocks_per_seq=blocks_per_seq, fresh=fresh,
                          n_sub=MIXER_SUB_BLOCKS if g == 1 else 1),
        grid=(n_blocks,),
        in_specs=[pl.BlockSpec((tm, d), lambda i: (i, 0)),
                  pl.BlockSpec((tm, width), lambda i: (i, 0)),
                  x_spec, mod_spec, mod_spec, mod_spec,
                  pl.BlockSpec((1, d), lambda i: (0, 0)),
                  pl.BlockSpec((d, 2 * width), lambda i: (0, 0), pipeline_mode=resident),
                  pl.BlockSpec((LRU_CONV_W, width), lambda i: (0, 0)), vec_spec,
                  gate_spec, vec_spec, gate_spec, vec_spec, vec_spec,
                  pl.BlockSpec((g, SUBLANES, width), seq_index),
                  pl.BlockSpec((g, 1, width), seq_index),
                  pl.BlockSpec((d, d), lambda i: (0, 0), pipeline_mode=resident)],
        out_specs=[x_spec,
                   pl.BlockSpec((tm, d), lambda i: (i, 0)),
                   pl.BlockSpec((g, 1, width), seq_index),
                   pl.BlockSpec((g, SUBLANES, width), lambda i: (i, 0, 0))],
        out_shape=[jax.ShapeDtypeStruct((n_groups, rows, d), F32),
                   jax.ShapeDtypeStruct((m, d), BF16),
                   jax.ShapeDtypeStruct((n_seq, 1, width), F32),
                   jax.ShapeDtypeStruct((n_groups, SUBLANES, width), F32)],
        scratch_shapes=[pltpu.VMEM((g, rows + SUBLANES, width), F32),
                        pltpu.VMEM((g, 1, width), F32)],
        compiler_params=_cparams(1),
        name="mixer",
    )(hn, attn, x3, gt, sh, sc, g2.reshape(1, d), w_lru_bf, cw, cb.reshape(1, width), wa_bf,
      ba.reshape(1, width), wx_bf, bx.reshape(1, width), lam.reshape(1, width), prev_pad, h0, w_out_bf)


def _ffn_kernel(hn_ref, wg_ref, wv_ref, cwg_ref, cwv_ref, cbg_ref, cbv_ref, wd_ref,
                x1_ref, gt_ref, pg_ref, pv_ref, *rest,
                groups, rows, blocks_per_seq, fresh, emit_bf16_weights):
    if emit_bf16_weights:
        y_ref, tg_ref, tv_ref, wg_out_ref, wv_out_ref, wd_out_ref, cg_ref, cv_ref = rest
        for src, dst in ((wg_ref, wg_out_ref), (wv_ref, wv_out_ref), (wd_ref, wd_out_ref)):
            dst[...] = src[...].astype(BF16)
        wg_ref, wv_ref, wd_ref = wg_out_ref, wv_out_ref, wd_out_ref
    else:
        y_ref, tg_ref, tv_ref, cg_ref, cv_ref = rest
    i = pl.program_id(0)
    j = pl.program_id(1)
    tm = groups * rows
    tf = wg_ref.shape[1]
    first_block = (i % blocks_per_seq) == 0
    pad = SUBLANES
    if not fresh and blocks_per_seq != 1:
        raise NotImplementedError("continuing sequences must fit one row block")

    @pl.when(j == 0)
    def _():
        y_ref[...] = jnp.zeros_like(y_ref)

    @pl.when(jnp.logical_and(i == 0, j == 0))
    def _():
        cg_ref[...] = jnp.zeros_like(cg_ref)
        cv_ref[...] = jnp.zeros_like(cv_ref)

    n_sub = FFN_SUB_BLOCKS if groups == 1 else 1
    sub = tm // n_sub
    row_slices = [slice(s * sub, (s + 1) * sub) for s in range(n_sub)]
    ups = []
    for rs in row_slices:
        hn = hn_ref[rs, :]
        ups.append((jnp.dot(hn, wg_ref[...], preferred_element_type=F32),
                    jnp.dot(hn, wv_ref[...], preferred_element_type=F32)))

    def boundary_rows(p_ref, carry_ref):
        if fresh:
            prev = jnp.where(first_block, 0.0, carry_ref[j])
            return prev[:, pad - 2:pad - 1, :], prev[:, pad - 1:pad, :]
        return p_ref[:, 0:1, :], p_ref[:, 1:2, :]

    def conv(u, p0, p1, cw_ref, cb_ref):
        u1 = pltpu.roll(u, 1, 0)
        u2 = pltpu.roll(u, 2, 0)
        if groups == 1:
            top = lax.broadcasted_iota(jnp.int32, (pad, tf), 0)
            u1_top = jnp.where(top == 0, p1, u1[0:pad])
            u2_top = jnp.where(top == 0, p0, jnp.where(top == 1, p1, u2[0:pad]))
            u1 = jnp.concatenate([u1_top, u1[pad:]], axis=0)
            u2 = jnp.concatenate([u2_top, u2[pad:]], axis=0)
        else:
            t_row = lax.broadcasted_iota(jnp.int32, (tm, tf), 0) % rows
            p1r = jnp.broadcast_to(p1, (groups, rows, tf)).reshape(tm, tf)
            p0r = jnp.broadcast_to(p0, (groups, rows, tf)).reshape(tm, tf)
            u1 = jnp.where(t_row == 0, p1r, u1)
            u2 = jnp.where(t_row == 0, p0r, jnp.where(t_row == 1, p1r, u2))
        cw = cw_ref[...]
        uc = cb_ref[...] + cw[0:1] * u2
        uc = uc + cw[1:2] * u1
        return uc + cw[2:3] * u

    pg = boundary_rows(pg_ref, cg_ref)
    pv = boundary_rows(pv_ref, cv_ref)
    if groups == 1:
        pg = (pg[0][0], pg[1][0])
        pv = (pv[0][0], pv[1][0])
    for s, rs in enumerate(row_slices):
        ug, uv = ups[s]
        gated = (_gelu_tanh(conv(ug, pg[0], pg[1], cwg_ref, cbg_ref))
                 * conv(uv, pv[0], pv[1], cwv_ref, cbv_ref)).astype(BF16)
        y_ref[rs, :] += jnp.dot(gated, wd_ref[...], preferred_element_type=F32)
        pg = (ug[sub - 2:sub - 1], ug[sub - 1:sub])
        pv = (uv[sub - 2:sub - 1], uv[sub - 1:sub])

    ug3 = ups[-1][0].reshape(groups, rows // n_sub, tf)
    uv3 = ups[-1][1].reshape(groups, rows // n_sub, tf)
    last = rows // n_sub
    cg_ref[j] = ug3[:, last - pad:last, :]
    cv_ref[j] = uv3[:, last - pad:last, :]
    tg_ref[...] = ug3[:, last - 2:last, :]
    tv_ref[...] = uv3[:, last - 2:last, :]

    @pl.when(j == pl.num_programs(1) - 1)
    def _():
        down = y_ref[...].reshape(groups, rows, y_ref.shape[-1])
        y_ref[...] = x1_ref[...] + (gt_ref[...] * down).reshape(y_ref.shape)


def _ffn_call(hn2, w_gate, w_value, w_down, cw, cb, x1, gt, prev_g, prev_v, *,
              n_seq, groups_per_block, mod_index, fresh, tf):
    n_groups, rows, d = x1.shape
    g = groups_per_block
    tm = g * rows
    n_blocks = n_groups // g
    blocks_per_seq = n_blocks // (n_seq // g)
    d_ff = w_down.shape[0]
    nj = d_ff // tf
    emit = w_down.dtype != BF16
    assert not emit or n_blocks == 1, "f32 weight tiles must be visited exactly once"
    value_offset = nj if w_value.shape[1] == 2 * d_ff else 0
    seq_index = lambda i: i // blocks_per_seq
    x_spec = pl.BlockSpec((tm, d), lambda i, j: (i, 0))
    up_spec = pl.BlockSpec((d, tf), lambda i, j: (0, j))
    down_spec = pl.BlockSpec((tf, d), lambda i, j: (j, 0))
    prev_spec = pl.BlockSpec((g, FFN_CONV_W - 1, tf), lambda i, j: (seq_index(i), 0, j))
    tail_spec = pl.BlockSpec((g, FFN_CONV_W - 1, tf), lambda i, j: (i, 0, j))
    tail_out = jax.ShapeDtypeStruct((n_groups, FFN_CONV_W - 1, d_ff), F32)
    out_specs = [x_spec, tail_spec, tail_spec]
    out_shape = [jax.ShapeDtypeStruct((n_groups * rows, d), F32), tail_out, tail_out]
    if emit:
        out_specs += [up_spec, up_spec, down_spec]
        out_shape += [jax.ShapeDtypeStruct((d, d_ff), BF16), jax.ShapeDtypeStruct((d, d_ff), BF16),
                      jax.ShapeDtypeStruct((d_ff, d), BF16)]
    outs = pl.pallas_call(
        functools.partial(_ffn_kernel, groups=g, rows=rows, blocks_per_seq=blocks_per_seq, fresh=fresh,
                          emit_bf16_weights=emit),
        grid=(n_blocks, nj),
        in_specs=[pl.BlockSpec((tm, d), lambda i, j: (i, 0)),
                  up_spec,
                  pl.BlockSpec((d, tf), lambda i, j: (0, j + value_offset)),
                  pl.BlockSpec((FFN_CONV_W, tf), lambda i, j: (0, j)),
                  pl.BlockSpec((FFN_CONV_W, tf), lambda i, j: (0, j + nj)),
                  pl.BlockSpec((1, tf), lambda i, j: (0, j)),
                  pl.BlockSpec((1, tf), lambda i, j: (0, j + nj)),
                  down_spec,
                  x_spec,
                  pl.BlockSpec((g, 1, d), lambda i, j: (mod_index(i), 0, 0)),
                  prev_spec, prev_spec],
        out_specs=out_specs,
        out_shape=out_shape,
        scratch_shapes=[pltpu.VMEM((nj, g, SUBLANES, tf), F32),
                        pltpu.VMEM((nj, g, SUBLANES, tf), F32)],
        compiler_params=_cparams(2),
        name="convffn",
    )(hn2, w_gate, w_value, cw, cw, cb.reshape(1, 2 * d_ff), cb.reshape(1, 2 * d_ff), w_down,
      x1.reshape(n_groups * rows, d), gt, prev_g, prev_v)
    y, tail_g, tail_v = outs[:3]
    last = lambda t: t.reshape(n_seq, n_groups // n_seq, FFN_CONV_W - 1, d_ff)[:, -1]
    bf16_weights = tuple(outs[3:]) if emit else (w_gate, w_value, w_down)
    return y, last(tail_g), last(tail_v), bf16_weights


def _run_group(x, mod, attn_fn, lru_prev_pad, lru_h0, ffn_prev, fresh, groups_per_block, rows_per_block, p,
               ffn_weights, ffn_tf):
    n_seq, t, d = x.shape
    rows = rows_per_block // groups_per_block
    x3 = x.reshape(n_seq * t // rows, rows, d)
    blocks_per_seq = max(1, t // rows_per_block)
    if groups_per_block == 1:
        mod_index = lambda i: i // blocks_per_seq
    else:
        mod_index = lambda i: i
    mods = [mod[:, n:n + 1, :] for n in range(N_MOD)]
    sh1, sc1, gt1, sh2, sc2, gt2 = mods

    q, kf, kb, vf, vb, hn = _inproj_call(
        x3, sh1, sc1, p["g_norm1"], p["w_qkv"], p["g_q"], p["g_k"],
        groups_per_block=groups_per_block, mod_index=mod_index)
    attn = attn_fn(q, kf, kb, vf, vb)
    x1, hn2, h_last, xl_tail = _mixer_call(
        hn, attn, x3, gt1, sh2, sc2, p["g_norm2"], p["w_lru_in"], p["w_lru_conv"], p["b_lru_conv"],
        p["w_lru_a"], p["b_lru_a"], p["w_lru_x"], p["b_lru_x"], p["lru_lambda"], lru_prev_pad, lru_h0,
        p["w_out"], n_seq=n_seq, groups_per_block=groups_per_block, mod_index=mod_index, fresh=fresh)
    w_gate, w_value, w_down = ffn_weights
    d_ff = w_down.shape[0]
    y, tail_g, tail_v, ffn_bf16 = _ffn_call(
        hn2, w_gate, w_value, w_down, p["w_ffn_conv"], p["b_ffn_conv"], x1, gt2,
        ffn_prev[:, :, :d_ff], ffn_prev[:, :, d_ff:],
        n_seq=n_seq, groups_per_block=groups_per_block, mod_index=mod_index, fresh=fresh, tf=ffn_tf)
    width = xl_tail.shape[-1]
    lru_conv_new = xl_tail.reshape(n_seq, -1, SUBLANES, width)[:, -1, SUBLANES - (LRU_CONV_W - 1):, :]
    ffn_conv_new = jnp.concatenate([tail_g, tail_v], axis=-1)
    return (y.reshape(n_seq, t, d), kf, vf, lru_conv_new, h_last.reshape(n_seq, width), ffn_conv_new), ffn_bf16


def kernel(x_prompt, x_sample, c_prompt, c_sample, cache_k, cache_v, page_table, state_lru_conv, state_lru_h, state_ffn_conv, w_ada, b_ada, g_norm1, w_in, g_q, g_k, b_sb, w_lru_conv, b_lru_conv, w_lru_a, b_lru_a, w_lru_x, b_lru_x, lru_lambda, w_out, g_norm2, w_up, w_ffn_conv, b_ffn_conv, w_down):
    depth = w_ada.shape[0]
    assert depth == 1, "single-layer step"
    l = 0
    bsz, seq, d = x_prompt.shape
    dec_b, dec_t, _ = x_sample.shape
    page, n_heads = cache_k.shape[2], cache_k.shape[3]
    width = n_heads * HEAD_DIM
    lru_w = state_lru_h.shape[-1]
    d_ff2 = state_ffn_conv.shape[-1]

    p = dict(
        g_norm1=g_norm1[l], w_qkv=w_in[l][:, :3 * width].astype(BF16),
        w_lru_in=w_in[l][:, 3 * width:].astype(BF16), g_q=g_q[l], g_k=g_k[l],
        w_lru_conv=w_lru_conv[l], b_lru_conv=b_lru_conv[l],
        w_lru_a=w_lru_a[l].astype(BF16), b_lru_a=b_lru_a[l],
        w_lru_x=w_lru_x[l].astype(BF16), b_lru_x=b_lru_x[l], lru_lambda=lru_lambda[l],
        w_out=w_out[l].astype(BF16), g_norm2=g_norm2[l],
        w_ffn_conv=w_ffn_conv[l], b_ffn_conv=b_ffn_conv[l])

    n_c = bsz + dec_b
    pad_c = (-n_c) % SUBLANES
    c_all = jnp.concatenate([c_prompt, c_sample, jnp.zeros((pad_c, d), F32)], axis=0)
    mod = _ada_call(c_all, w_ada[l], b_ada[l])
    mod_p = mod[:bsz].reshape(bsz, N_MOD, d)
    mod_s = mod[bsz:n_c].reshape(dec_b, N_MOD, d)

    def sample_attn(q, kf, kb, vf, vb):
        q3 = q.astype(F32).reshape(dec_b, dec_t, width)
        pool = cache_k.shape[1]
        ck = cache_k[l].reshape(pool, page * n_heads, HEAD_DIM)
        cv = cache_v[l].reshape(pool, page * n_heads, HEAD_DIM)
        o = _attn_sample_call(q3, kf, vf, ck, cv, page_table, b_sb[l])
        return o.reshape(dec_b * dec_t, width)

    prev_pad = jnp.concatenate(
        [jnp.zeros((dec_b, SUBLANES - (LRU_CONV_W - 1), lru_w), F32), state_lru_conv[l]], axis=1)
    (ys, ks, vs, lcs, lhs, fcs), ffn_bf16 = _run_group(
        x_sample, mod_s, sample_attn, prev_pad, state_lru_h[l].reshape(dec_b, 1, lru_w),
        state_ffn_conv[l], False, dec_b, dec_b * dec_t, p, (w_up[l], w_up[l], w_down[l]), FFN_TF_SAMPLE)

    def prompt_attn(q, kf, kb, vf, vb):
        return _attn_prompt_call(q, kb, vb, b_sb[l], batch=bsz, seq=seq)

    zeros_prev = jnp.zeros((bsz, SUBLANES, lru_w), F32)
    zeros_h = jnp.zeros((bsz, 1, lru_w), F32)
    zeros_ffn = jnp.zeros((bsz, FFN_CONV_W - 1, d_ff2), F32)
    (yp, kp, vp, lcp, lhp, fcp), _ = _run_group(
        x_prompt, mod_p, prompt_attn, zeros_prev, zeros_h, zeros_ffn, True, 1, ROW_TILE, p, ffn_bf16, FFN_TF)

    n_pages_p = seq // page
    return (yp, ys,
            kp.reshape(1, bsz, n_pages_p, page, n_heads, HEAD_DIM),
            vp.reshape(1, bsz, n_pages_p, page, n_heads, HEAD_DIM),
            ks.reshape(1, dec_b, dec_t, n_heads, HEAD_DIM),
            vs.reshape(1, dec_b, dec_t, n_heads, HEAD_DIM),
            lcp[None], lcs[None], lhp[None], lhs[None], fcp[None], fcs[None])
```

```python
import functools
import math

import jax
import jax.numpy as jnp
from jax import lax
from jax.experimental import pallas as pl
from jax.experimental.pallas import tpu as pltpu

F32 = jnp.float32
BF16 = jnp.bfloat16

EPS = 1e-6
LRU_C = 8.0
HEAD_DIM = 128
LRU_BLOCKS = 8
LRU_CONV_W = 4
FFN_CONV_W = 3
N_MOD = 6
SB_SCALE = 1.0 / math.sqrt(HEAD_DIM)

SUBLANES = 8
VMEM_LIMIT = 56 * 1024 * 1024

ROW_TILE = 512
INPROJ_SUB_BLOCKS = 2
MIXER_SUB_BLOCKS = 2
ATTN_TQ = 256
ATTN_TK = 256
ATTN_HEADS = 8
ADA_TN = 1024
FFN_TF = 1024
FFN_TF_SAMPLE = 256
FFN_SUB_BLOCKS = 2
PAGES_PER_STEP = 8

def _cparams(n_axes):
    return pltpu.CompilerParams(dimension_semantics=("arbitrary",) * n_axes,
                                vmem_limit_bytes=VMEM_LIMIT)


def _softplus(z):
    return jnp.maximum(z, 0.0) + jnp.log1p(jnp.exp(-jnp.abs(z)))


def _softplus_logits(z):
    return jnp.maximum(z, 0.0) + jnp.log(1.0 + jnp.exp(-jnp.abs(z)))


def _gelu_tanh(x):
    c = math.sqrt(2.0 / math.pi)
    return x * (0.5 * (1.0 + jnp.tanh(c * (x + 0.044715 * (x * x * x)))))


def _suffix_sum_matrix(n_keys):
    j = lax.broadcasted_iota(jnp.int32, (n_keys, n_keys), 0)
    s = lax.broadcasted_iota(jnp.int32, (n_keys, n_keys), 1)
    return jnp.where(j >= s, 1.0, 0.0).astype(BF16)


def _ada_kernel(c_ref, w_ref, b_ref, o_ref):
    c = c_ref[...]
    s = (c * jax.nn.sigmoid(c)).astype(BF16)
    o_ref[...] = jnp.dot(s, w_ref[...].astype(BF16), preferred_element_type=F32) + b_ref[...]


def _ada_call(c_all, w_ada, b_ada):
    m, d = c_all.shape
    n = w_ada.shape[1]
    return pl.pallas_call(
        _ada_kernel,
        grid=(n // ADA_TN,),
        in_specs=[pl.BlockSpec((m, d), lambda j: (0, 0)),
                  pl.BlockSpec((d, ADA_TN), lambda j: (0, j)),
                  pl.BlockSpec((1, ADA_TN), lambda j: (0, j))],
        out_specs=pl.BlockSpec((m, ADA_TN), lambda j: (0, j)),
        out_shape=jax.ShapeDtypeStruct((m, n), F32),
        compiler_params=_cparams(1),
        name="adaln",
    )(c_all, w_ada, b_ada.reshape(1, n))


def _head_rmsnorm(a, g, n_heads):
    outs = []
    for h in range(n_heads):
        ah = a[:, h * HEAD_DIM:(h + 1) * HEAD_DIM]
        ms = jnp.mean(ah * ah, axis=-1, keepdims=True)
        outs.append(ah * lax.rsqrt(ms + EPS) * g)
    return outs


def _inproj_kernel(x_ref, sh_ref, sc_ref, g1_ref, w_ref, gq_ref, gk_ref,
                   q_ref, kf_ref, kb_ref, vf_ref, vb_ref, hn_ref, *, n_heads, n_sub):
    tm, d = hn_ref.shape
    width = n_heads * HEAD_DIM
    sub = tm // n_sub
    row_slices = [slice(s * sub, (s + 1) * sub) for s in range(n_sub)]
    head_cols = [slice(h * HEAD_DIM, (h + 1) * HEAD_DIM) for h in range(n_heads)]

    def normalize(rs):
        x = x_ref[...] if n_sub == 1 else x_ref[:, rs, :]
        ms = jnp.mean(x * x, axis=-1, keepdims=True)
        y = x * lax.rsqrt(ms + EPS) * g1_ref[...]
        hn = y * (1.0 + sc_ref[...]) + sh_ref[...]
        hn_ref[rs, :] = hn.reshape(sub, d).astype(BF16)

    def project(t, epilogue):
        w = w_ref[:, t * width:(t + 1) * width]
        accs = [jnp.dot(hn_ref[rs, :], w, preferred_element_type=F32) for rs in row_slices]
        for rs, acc in zip(row_slices, accs):
            epilogue(rs, acc)

    def head_rows(rs, h):
        return pl.ds(rs.start * n_heads + h, sub, stride=n_heads)

    def q_epilogue(rs, acc):
        for h, qh in enumerate(_head_rmsnorm(acc, gq_ref[...], n_heads)):
            q_ref[rs, head_cols[h]] = (qh * SB_SCALE).astype(BF16)

    def k_epilogue(rs, acc):
        for h, kh in enumerate(_head_rmsnorm(acc, gk_ref[...], n_heads)):
            kf_ref[head_rows(rs, h), :] = kh
            kb_ref[rs, head_cols[h]] = kh.astype(BF16)

    def v_epilogue(rs, acc):
        for h in range(n_heads):
            vf_ref[head_rows(rs, h), :] = acc[:, head_cols[h]]
        vb_ref[rs, :] = acc.astype(BF16)

    for rs in row_slices:
        normalize(rs)
    for t, epilogue in enumerate([q_epilogue, k_epilogue, v_epilogue]):
        project(t, epilogue)


def _inproj_call(x3, sh, sc, g1, w_qkv_bf, g_q, g_k, *, groups_per_block, mod_index):
    n_groups, rows, d = x3.shape
    g = groups_per_block
    tm = g * rows
    m = n_groups * rows
    n_blocks = n_groups // g
    width = w_qkv_bf.shape[1] // 3
    n_heads = width // HEAD_DIM
    row_spec = pl.BlockSpec((tm, width), lambda i: (i, 0))
    head_spec = pl.BlockSpec((tm * n_heads, HEAD_DIM), lambda i: (i, 0))
    mod_spec = pl.BlockSpec((g, 1, d), lambda i: (mod_index(i), 0, 0))
    head_out = jax.ShapeDtypeStruct((m * n_heads, HEAD_DIM), F32)
    bf_out = jax.ShapeDtypeStruct((m, width), BF16)
    return pl.pallas_call(
        functools.partial(_inproj_kernel, n_heads=n_heads, n_sub=INPROJ_SUB_BLOCKS if g == 1 else 1),
        grid=(n_blocks,),
        in_specs=[pl.BlockSpec((g, rows, d), lambda i: (i, 0, 0)),
                  mod_spec, mod_spec,
                  pl.BlockSpec((1, d), lambda i: (0, 0)),
                  pl.BlockSpec((d, 3 * width), lambda i: (0, 0), pipeline_mode=pl.Buffered(1)),
                  pl.BlockSpec((1, HEAD_DIM), lambda i: (0, 0)),
                  pl.BlockSpec((1, HEAD_DIM), lambda i: (0, 0))],
        out_specs=[row_spec, head_spec, row_spec, head_spec, row_spec,
                   pl.BlockSpec((tm, d), lambda i: (i, 0))],
        out_shape=[bf_out, head_out, bf_out, head_out, bf_out, jax.ShapeDtypeStruct((m, d), BF16)],
        compiler_params=_cparams(1),
        name="inproj",
    )(x3, sh, sc, g1.reshape(1, d), w_qkv_bf, g_q.reshape(1, HEAD_DIM), g_k.reshape(1, HEAD_DIM))


def _sb_mass(z, suffix_mat, mask):
    sp = _softplus_logits(z)
    if mask is not None:
        sp = jnp.where(mask, sp, 0.0)
    return jnp.dot(sp.astype(BF16), suffix_mat, preferred_element_type=F32)


def _sb_weights(z, cums, later, mask):
    w = jnp.exp(z - cums - later)
    if mask is not None:
        w = jnp.where(mask, w, 0.0)
    return w, later + cums[:, 0:1]


def _attn_prompt_kernel(bias_ref, q_ref, k_ref, v_ref, o_ref):
    hg = pl.program_id(1)
    qi = pl.program_id(2)
    suffix_mat = _suffix_sum_matrix(ATTN_TK)
    heads = range(ATTN_HEADS)
    cols = [slice(hh * HEAD_DIM, (hh + 1) * HEAD_DIM) for hh in heads]
    biases = [bias_ref[hg * ATTN_HEADS + hh] for hh in heads]
    qs = [q_ref[:, c] for c in cols]

    def tile(j, carry, mask):
        start = pl.multiple_of(j * ATTN_TK, ATTN_TK)
        nt = (((1,), (1,)), ((), ()))
        zs = [lax.dot_general(qs[hh], k_ref[pl.ds(start, ATTN_TK), cols[hh]], nt,
                              preferred_element_type=F32) + biases[hh] for hh in heads]
        cums = [_sb_mass(zs[hh], suffix_mat, mask) for hh in heads]
        ws = [_sb_weights(zs[hh], cums[hh], carry[hh][1], mask) for hh in heads]
        return tuple(
            (carry[hh][0] + jnp.dot(ws[hh][0].astype(BF16), v_ref[pl.ds(start, ATTN_TK), cols[hh]],
                                    preferred_element_type=F32), ws[hh][1]) for hh in heads)

    row = lax.broadcasted_iota(jnp.int32, (ATTN_TQ, ATTN_TK), 0)
    col = lax.broadcasted_iota(jnp.int32, (ATTN_TQ, ATTN_TK), 1)
    carry = tuple((jnp.zeros((ATTN_TQ, HEAD_DIM), F32), jnp.zeros((ATTN_TQ, 1), F32)) for _ in heads)
    carry = tile(qi, carry, col < row)
    carry = lax.fori_loop(0, qi, lambda t, c: tile(qi - 1 - t, c, None), carry)
    for hh in heads:
        o_ref[:, cols[hh]] = carry[hh][0].astype(o_ref.dtype)


def _attn_prompt_call(q, k, v, b_sb, *, batch, seq):
    m, width = q.shape
    n_heads = width // HEAD_DIM
    nq = seq // ATTN_TQ
    gw = ATTN_HEADS * HEAD_DIM
    q_spec = pl.BlockSpec((ATTN_TQ, gw), lambda b, h, i: (b * nq + i, h))
    kv_spec = pl.BlockSpec((seq, gw), lambda b, h, i: (b, h))
    return pl.pallas_call(
        _attn_prompt_kernel,
        grid=(batch, n_heads // ATTN_HEADS, nq),
        in_specs=[pl.BlockSpec(memory_space=pltpu.SMEM), q_spec, kv_spec, kv_spec],
        out_specs=q_spec,
        out_shape=jax.ShapeDtypeStruct((m, width), BF16),
        compiler_params=_cparams(3),
        name="attn_prompt",
    )(b_sb, q, k, v)


def _attn_sample_kernel(pt_ref, bias_ref, q_ref, knew_ref, vnew_ref, *rest, n_heads, n_new):
    del pt_ref
    pages = PAGES_PER_STEP
    k_refs = rest[:pages]
    v_refs = rest[pages:2 * pages]
    o_ref = rest[2 * pages]
    kpad_ref, vpad_ref, acc_ref, later_ref = rest[2 * pages + 1:]
    g = pl.program_id(1)
    page_len = kpad_ref.shape[0] // n_heads
    suffix_mat = _suffix_sum_matrix(page_len)
    bias = jnp.concatenate([jnp.full((n_new, 1), bias_ref[h], F32) for h in range(n_heads)], axis=0)
    nt = (((1,), (1,)), ((), ()))

    def visit(k_pages, v_pages, mask):
        qh = [q_ref[0, :, h * HEAD_DIM:(h + 1) * HEAD_DIM].astype(BF16) for h in range(n_heads)]
        zs = []
        for kp in k_pages:
            z = [lax.dot_general(qh[h], kp[pl.ds(h, page_len, stride=n_heads), :].astype(BF16), nt,
                                 preferred_element_type=F32) for h in range(n_heads)]
            zs.append(jnp.concatenate(z, axis=0) + bias)
        cums = [_sb_mass(z, suffix_mat, mask) for z in zs]
        later = later_ref[...]
        ws = []
        for z, c in zip(zs, cums):
            w, later = _sb_weights(z, c, later, mask)
            ws.append(w)
        later_ref[...] = later
        acc = [acc_ref[h] for h in range(n_heads)]
        for w, vp in zip(ws, v_pages):
            for h in range(n_heads):
                wh = w[h * n_new:(h + 1) * n_new].astype(BF16)
                vh = vp[pl.ds(h, page_len, stride=n_heads), :].astype(BF16)
                acc[h] = acc[h] + jnp.dot(wh, vh, preferred_element_type=F32)
        for h in range(n_heads):
            acc_ref[h] = acc[h]

    @pl.when(g == 0)
    def _():
        acc_ref[...] = jnp.zeros_like(acc_ref)
        later_ref[...] = jnp.zeros_like(later_ref)
        kpad_ref[...] = jnp.zeros_like(kpad_ref)
        vpad_ref[...] = jnp.zeros_like(vpad_ref)
        kpad_ref[0:n_new * n_heads, :] = knew_ref[...]
        vpad_ref[0:n_new * n_heads, :] = vnew_ref[...]
        row = lax.broadcasted_iota(jnp.int32, (n_heads * n_new, page_len), 0) % n_new
        col = lax.broadcasted_iota(jnp.int32, (n_heads * n_new, page_len), 1)
        visit([kpad_ref], [vpad_ref], col < row)

    visit([r.at[0] for r in k_refs], [r.at[0] for r in v_refs], None)

    @pl.when(g == pl.num_programs(1) - 1)
    def _():
        for h in range(n_heads):
            o_ref[0, :, h * HEAD_DIM:(h + 1) * HEAD_DIM] = acc_ref[h].astype(o_ref.dtype)


def _attn_sample_call(q3, knew, vnew, cache_k, cache_v, page_table, b_sb):
    bsz, n_new, width = q3.shape
    n_heads = width // HEAD_DIM
    n_pages = page_table.shape[1]
    page_rows = cache_k.shape[1]
    pages = PAGES_PER_STEP
    steps = n_pages // pages

    def page_spec(p):
        def index(b, g, pt):
            return (pt[b, n_pages - 1 - (g * pages + p)], 0, 0)
        return pl.BlockSpec((1, page_rows, HEAD_DIM), index)

    new_spec = pl.BlockSpec((n_new * n_heads, HEAD_DIM), lambda b, g, pt: (b, 0))
    q_spec = pl.BlockSpec((1, n_new, width), lambda b, g, pt: (b, 0, 0))
    grid_spec = pltpu.PrefetchScalarGridSpec(
        num_scalar_prefetch=1,
        grid=(bsz, steps),
        in_specs=([pl.BlockSpec(memory_space=pltpu.SMEM), q_spec, new_spec, new_spec]
                  + [page_spec(p) for p in range(pages)] * 2),
        out_specs=q_spec,
        scratch_shapes=[pltpu.VMEM((page_rows, HEAD_DIM), F32),
                        pltpu.VMEM((page_rows, HEAD_DIM), F32),
                        pltpu.VMEM((n_heads, n_new, HEAD_DIM), F32),
                        pltpu.VMEM((n_heads * n_new, 1), F32)],
    )
    return pl.pallas_call(
        functools.partial(_attn_sample_kernel, n_heads=n_heads, n_new=n_new),
        grid_spec=grid_spec,
        out_shape=jax.ShapeDtypeStruct((bsz, n_new, width), F32),
        compiler_params=_cparams(2),
        name="attn_sample",
    )(page_table, b_sb, q3, knew, vnew, *([cache_k] * pages), *([cache_v] * pages))


def _attn_fused_kernel(pt_ref, sb_ref, sqi_ref, sj_ref, bias_ref, qs_ref, knew_ref, vnew_ref, *rest,
                       n_heads, n_new, sample_steps, steps_per_seq):
    del pt_ref, sb_ref
    pages = PAGES_PER_STEP
    k_refs = rest[:pages]
    v_refs = rest[pages:2 * pages]
    qp_ref, kp_ref, vp_ref, os_ref, op_ref = rest[2 * pages:2 * pages + 5]
    kpad_ref, vpad_ref, acc_ref, later_ref, o_scr, later_scr = rest[2 * pages + 5:]
    s = pl.program_id(0)
    active = s < sample_steps
    g = s % steps_per_seq
    qi = sqi_ref[s]
    j = sj_ref[s]
    page_len = kpad_ref.shape[0] // n_heads
    page_suffix = _suffix_sum_matrix(page_len)
    tile_suffix = _suffix_sum_matrix(ATTN_TK)
    heads = range(n_heads)
    cols = [slice(h * HEAD_DIM, (h + 1) * HEAD_DIM) for h in heads]
    bias_rows = jnp.concatenate([jnp.full((n_new, 1), bias_ref[h], F32) for h in heads], axis=0)
    nt = (((1,), (1,)), ((), ()))

    def sample_logits(k_pages):
        qh = [qs_ref[0, :, cols[h]].astype(BF16) for h in heads]
        zs = []
        for kp in k_pages:
            z = [lax.dot_general(qh[h], kp[pl.ds(h, page_len, stride=n_heads), :].astype(BF16), nt,
                                 preferred_element_type=F32) for h in heads]
            zs.append(jnp.concatenate(z, axis=0) + bias_rows)
        return zs

    def sample_weights(zs, cums, mask):
        later = later_ref[...]
        ws = []
        for z, c in zip(zs, cums):
            w, later = _sb_weights(z, c, later, mask)
            ws.append(w)
        return ws, later

    def sample_values(ws, v_pages):
        acc = [acc_ref[h] for h in heads]
        for w, vp in zip(ws, v_pages):
            for h in heads:
                wh = w[h * n_new:(h + 1) * n_new].astype(BF16)
                vh = vp[pl.ds(h, page_len, stride=n_heads), :].astype(BF16)
                acc[h] = acc[h] + jnp.dot(wh, vh, preferred_element_type=F32)
        return acc

    @pl.when(jnp.logical_and(active, g == 0))
    def _():
        acc_ref[...] = jnp.zeros_like(acc_ref)
        later_ref[...] = jnp.zeros_like(later_ref)
        kpad_ref[...] = jnp.zeros_like(kpad_ref)
        vpad_ref[...] = jnp.zeros_like(vpad_ref)
        kpad_ref[0:n_new * n_heads, :] = knew_ref[...]
        vpad_ref[0:n_new * n_heads, :] = vnew_ref[...]
        row = lax.broadcasted_iota(jnp.int32, (n_heads * n_new, page_len), 0) % n_new
        col = lax.broadcasted_iota(jnp.int32, (n_heads * n_new, page_len), 1)
        mask = col < row
        zs = sample_logits([kpad_ref])
        ws, later = sample_weights(zs, [_sb_mass(zs[0], page_suffix, mask)], mask)
        later_ref[...] = later
        for h, a in enumerate(sample_values(ws, [vpad_ref])):
            acc_ref[h] = a

    def step(diagonal):
        k_pages = [r.at[0] for r in k_refs]
        v_pages = [r.at[0] for r in v_refs]
        start = pl.multiple_of(j * ATTN_TK, ATTN_TK)
        if diagonal:
            row = lax.broadcasted_iota(jnp.int32, (ATTN_TQ, ATTN_TK), 0)
            col = lax.broadcasted_iota(jnp.int32, (ATTN_TQ, ATTN_TK), 1)
            mask = col < row
        else:
            mask = None
        zs_s = sample_logits(k_pages)
        zs_p = [lax.dot_general(qp_ref[:, cols[h]], kp_ref[pl.ds(start, ATTN_TK), cols[h]], nt,
                                preferred_element_type=F32) + bias_ref[h] for h in heads]
        cums_s = [_sb_mass(z, page_suffix, None) for z in zs_s]
        cums_p = [_sb_mass(z, tile_suffix, mask) for z in zs_p]
        ws_s, later_s = sample_weights(zs_s, cums_s, None)
        later_ref[...] = jnp.where(active, later_s, later_ref[...])
        ws_p = []
        for h in heads:
            later_p = jnp.zeros((ATTN_TQ, 1), F32) if diagonal else later_scr[h]
            w, later_p = _sb_weights(zs_p[h], cums_p[h], later_p, mask)
            later_scr[h] = later_p
            ws_p.append(w)
        for h, a in enumerate(sample_values(ws_s, v_pages)):
            acc_ref[h] = jnp.where(active, a, acc_ref[h])
        for h in heads:
            pv = jnp.dot(ws_p[h].astype(BF16), vp_ref[pl.ds(start, ATTN_TK), cols[h]],
                         preferred_element_type=F32)
            o_scr[:, cols[h]] = pv if diagonal else o_scr[:, cols[h]] + pv

    pl.when(j == qi)(functools.partial(step, True))
    pl.when(j != qi)(functools.partial(step, False))

    @pl.when(jnp.logical_and(active, g == steps_per_seq - 1))
    def _():
        for h in heads:
            os_ref[0, :, cols[h]] = acc_ref[h].astype(os_ref.dtype)

    @pl.when(j == 0)
    def _():
        op_ref[...] = o_scr[...].astype(op_ref.dtype)


def _attn_fused_call(q_s3, knew, vnew, cache_k, cache_v, page_table, q_p, k_p, v_p, b_sb, *, batch, seq):
    bsz, n_new, width = q_s3.shape
    n_heads = width // HEAD_DIM
    n_pages = page_table.shape[1]
    page_rows = cache_k.shape[1]
    pages = PAGES_PER_STEP
    steps_per_seq = n_pages // pages
    sample_steps = bsz * steps_per_seq
    nq = seq // ATTN_TQ
    sched = [(b, qi, j) for b in range(batch) for qi in range(nq) for j in range(qi, -1, -1)]
    n_steps = max(len(sched), sample_steps)
    assert len(sched) == n_steps, "prompt schedule must cover every grid step"
    sb, sqi, sj = (jnp.asarray([t[n] for t in sched], jnp.int32) for n in range(3))

    def sample_pos(s):
        s = jnp.minimum(s, sample_steps - 1)
        return s // steps_per_seq, s % steps_per_seq

    def page_spec(p):
        def index(s, pt, sb, sqi, sj):
            b, g = sample_pos(s)
            return (pt[b, n_pages - 1 - (g * pages + p)], 0, 0)
        return pl.BlockSpec((1, page_rows, HEAD_DIM), index)

    new_spec = pl.BlockSpec((n_new * n_heads, HEAD_DIM), lambda s, pt, sb, sqi, sj: (sample_pos(s)[0], 0))
    qs_spec = pl.BlockSpec((1, n_new, width), lambda s, pt, sb, sqi, sj: (sample_pos(s)[0], 0, 0))
    qp_spec = pl.BlockSpec((ATTN_TQ, width), lambda s, pt, sb, sqi, sj: (sb[s] * nq + sqi[s], 0))
    kv_spec = pl.BlockSpec((seq, width), lambda s, pt, sb, sqi, sj: (sb[s], 0), pipeline_mode=pl.Buffered(1))
    grid_spec = pltpu.PrefetchScalarGridSpec(
        num_scalar_prefetch=4,
        grid=(n_steps,),
        in_specs=([pl.BlockSpec(memory_space=pltpu.SMEM), qs_spec, new_spec, new_spec]
                  + [page_spec(p) for p in range(pages)] * 2
                  + [qp_spec, kv_spec, kv_spec]),
        out_specs=[qs_spec, qp_spec],
        scratch_shapes=[pltpu.VMEM((page_rows, HEAD_DIM), F32),
                        pltpu.VMEM((page_rows, HEAD_DIM), F32),
                        pltpu.VMEM((n_heads, n_new, HEAD_DIM), F32),
                        pltpu.VMEM((n_heads * n_new, 1), F32),
                        pltpu.VMEM((ATTN_TQ, width), F32),
                        pltpu.VMEM((n_heads, ATTN_TQ, 1), F32)],
    )
    return pl.pallas_call(
        functools.partial(_attn_fused_kernel, n_heads=n_heads, n_new=n_new, sample_steps=sample_steps,
                          steps_per_seq=steps_per_seq),
        grid_spec=grid_spec,
        out_shape=[jax.ShapeDtypeStruct((bsz, n_new, width), F32),
                   jax.ShapeDtypeStruct(q_p.shape, BF16)],
        compiler_params=_cparams(1),
        name="attn_fused",
    )(page_table, sb, sqi, sj, b_sb, q_s3, knew, vnew, *([cache_k] * pages), *([cache_v] * pages),
      q_p, k_p, v_p)


def _segmented_linear_scan(a, b, seg_len):
    rows = lax.broadcasted_iota(jnp.int32, a.shape, 0) % seg_len
    d = 1
    while d < seg_len:
        keep = rows >= d
        a_prev = pltpu.roll(a, d, 0)
        b_prev = pltpu.roll(b, d, 0)
        b = jnp.where(keep, a * b_prev + b, b)
        a = jnp.where(keep, a * a_prev, a)
        d *= 2
    return a, b


def _mixer_kernel(hn_ref, attn_ref, x_ref, gt_ref, sh_ref, sc_ref, g2_ref,
                  wl_ref, cw_ref, cb_ref, wa_ref, ba_ref, wx_ref, bx_ref, lam_ref, prev_ref, h0_ref, wo_ref,
                  x1_ref, hn2_ref, hlast_ref, xtail_ref, xp_ref, hcar_ref,
                  *, groups, rows, blocks_per_seq, fresh, n_sub):
    i = pl.program_id(0)
    width = xp_ref.shape[-1]
    d = x_ref.shape[-1]
    first_block = (i % blocks_per_seq) == 0
    pad = SUBLANES
    sub_rows = rows // n_sub
    sub = groups * sub_rows
    row_slices = [slice(s * sub, (s + 1) * sub) for s in range(n_sub)]

    @pl.when(first_block)
    def _():
        if fresh:
            xp_ref[:, 0:pad, :] = jnp.zeros((groups, pad, width), F32)
            hcar_ref[...] = jnp.zeros_like(hcar_ref)
        else:
            xp_ref[:, 0:pad, :] = prev_ref[...]
            hcar_ref[...] = h0_ref[...]

    xgs = [jnp.dot(hn_ref[rs, :], wl_ref[...], preferred_element_type=F32) for rs in row_slices]
    mixes = [jnp.dot(attn_ref[rs, :].astype(BF16), wo_ref[0:width, :], preferred_element_type=F32)
             for rs in row_slices]
    for s in range(n_sub):
        xp_ref[:, pad + s * sub_rows:pad + (s + 1) * sub_rows, :] = (
            xgs[s][:, 0:width].reshape(groups, sub_rows, width))

    cw = cw_ref[...]
    bw = width // LRU_BLOCKS
    neg_c_softplus = (-LRU_C) * _softplus(-lam_ref[...])
    h_prev = hcar_ref[...] if rows == SUBLANES else hcar_ref[0]
    for s, rs in enumerate(row_slices):
        t0 = s * sub_rows
        xc = cb_ref[...] + cw[0:1] * xp_ref[:, pl.ds(pad - 3 + t0, sub_rows), :]
        for t in range(1, LRU_CONV_W):
            xc = xc + cw[t:t + 1] * xp_ref[:, pl.ds(pad - 3 + t0 + t, sub_rows), :]
        xc = xc.reshape(sub, width)

        xcb = xc.astype(BF16)
        r_parts, i_parts = [], []
        for n in range(LRU_BLOCKS):
            xb = xcb[:, n * bw:(n + 1) * bw]
            r_parts.append(jnp.dot(xb, wa_ref[n], preferred_element_type=F32))
            i_parts.append(jnp.dot(xb, wx_ref[n], preferred_element_type=F32))
        r = jax.nn.sigmoid(jnp.concatenate(r_parts, axis=1) + ba_ref[...])
        ig = jax.nn.sigmoid(jnp.concatenate(i_parts, axis=1) + bx_ref[...])
        log_a = r * neg_c_softplus
        a = jnp.exp(log_a)
        one_minus_a2 = -jnp.tanh(log_a) * (a * a + 1.0)
        mult = jnp.where(one_minus_a2 > 0.0, one_minus_a2 * lax.rsqrt(one_minus_a2), 0.0)
        if fresh and s == 0:
            t_row = lax.broadcasted_iota(jnp.int32, (sub, 1), 0)
            start_row = jnp.where(first_block, 0, -1)
            mult = jnp.where(t_row == start_row, 1.0, mult)
        b = mult * ig * xc

        a_cum, h = _segmented_linear_scan(a, b, SUBLANES)
        if rows == SUBLANES:
            h = h + a_cum * jnp.broadcast_to(h_prev, (groups, rows, width)).reshape(sub, width)
            h_prev = h.reshape(groups, rows, width)[:, rows - 1:rows, :]
        else:
            tiles = []
            for t in range(sub_rows // SUBLANES):
                ts = slice(t * SUBLANES, (t + 1) * SUBLANES)
                h_t = h[ts] + a_cum[ts] * h_prev
                tiles.append(h_t)
                h_prev = h_t[SUBLANES - 1:SUBLANES]
            h = jnp.concatenate(tiles, axis=0)
        lru = (h * _gelu_tanh(xgs[s][:, width:2 * width])).astype(BF16)

        mix = mixes[s] + jnp.dot(lru, wo_ref[width:2 * width, :], preferred_element_type=F32)
        x1 = x_ref[:, t0:t0 + sub_rows, :] + gt_ref[...] * mix.reshape(groups, sub_rows, d)
        x1_ref[:, t0:t0 + sub_rows, :] = x1
        ms = jnp.mean(x1 * x1, axis=-1, keepdims=True)
        y = x1 * lax.rsqrt(ms + EPS) * g2_ref[...]
        hn2 = y * (1.0 + sc_ref[...]) + sh_ref[...]
        hn2_ref[rs, :] = hn2.reshape(sub, d).astype(BF16)

    h_last = h_prev if rows == SUBLANES else h_prev[None]
    hcar_ref[...] = h_last
    hlast_ref[...] = h_last
    tail = xp_ref[:, rows:rows + pad, :]
    xtail_ref[...] = tail
    xp_ref[:, 0:pad, :] = tail


def _mixer_call(hn, attn, x3, gt, sh, sc, g2, w_lru_bf, cw, cb, wa_bf, ba, wx_bf, bx, lam, prev_pad, h0,
                w_out_bf, *, n_seq, groups_per_block, mod_index, fresh):
    n_groups, rows, d = x3.shape
    g = groups_per_block
    tm = g * rows
    m = n_groups * rows
    n_blocks = n_groups // g
    blocks_per_seq = n_blocks // (n_seq // g)
    width = attn.shape[1]
    bw = width // LRU_BLOCKS
    resident = pl.Buffered(1)
    x_spec = pl.BlockSpec((g, rows, d), lambda i: (i, 0, 0))
    mod_spec = pl.BlockSpec((g, 1, d), lambda i: (mod_index(i), 0, 0))
    vec_spec = pl.BlockSpec((1, width), lambda i: (0, 0))
    gate_spec = pl.BlockSpec((LRU_BLOCKS, bw, bw), lambda i: (0, 0, 0))
    seq_index = lambda i: (i // blocks_per_seq, 0, 0)
    return pl.pallas_call(
        functools.partial(_mixer_kernel, groups=g, rows=rows, blocks_per_seq=blocks_per_seq, fresh=fresh,
                          n_sub=MIXER_SUB_BLOCKS if g == 1 else 1),
        grid=(n_blocks,),
        in_specs=[pl.BlockSpec((tm, d), lambda i: (i, 0)),
                  pl.BlockSpec((tm, width), lambda i: (i, 0)),
                  x_spec, mod_spec, mod_spec, mod_spec,
                  pl.BlockSpec((1, d), lambda i: (0, 0)),
                  pl.BlockSpec((d, 2 * width), lambda i: (0, 0), pipeline_mode=resident),
                  pl.BlockSpec((LRU_CONV_W, width), lambda i: (0, 0)), vec_spec,
                  gate_spec, vec_spec, gate_spec, vec_spec, vec_spec,
                  pl.BlockSpec((g, SUBLANES, width), seq_index),
                  pl.BlockSpec((g, 1, width), seq_index),
                  pl.BlockSpec((d, d), lambda i: (0, 0), pipeline_mode=resident)],
        out_specs=[x_spec,
                   pl.BlockSpec((tm, d), lambda i: (i, 0)),
                   pl.BlockSpec((g, 1, width), seq_index),
                   pl.BlockSpec((g, SUBLANES, width), lambda i: (i, 0, 0))],
        out_shape=[jax.ShapeDtypeStruct((n_groups, rows, d), F32),
                   jax.ShapeDtypeStruct((m, d), BF16),
                   jax.ShapeDtypeStruct((n_seq, 1, width), F32),
                   jax.ShapeDtypeStruct((n_groups, SUBLANES, width), F32)],
        scratch_shapes=[pltpu.VMEM((g, rows + SUBLANES, width), F32),
                        pltpu.VMEM((g, 1, width), F32)],
        compiler_params=_cparams(1),
        name="mixer",
    )(hn, attn, x3, gt, sh, sc, g2.reshape(1, d), w_lru_bf, cw, cb.reshape(1, width), wa_bf,
      ba.reshape(1, width), wx_bf, bx.reshape(1, width), lam.reshape(1, width), prev_pad, h0, w_out_bf)


def _ffn_kernel(hn_ref, wg_ref, wv_ref, cwg_ref, cwv_ref, cbg_ref, cbv_ref, wd_ref,
                x1_ref, gt_ref, pg_ref, pv_ref, *rest,
                groups, rows, blocks_per_seq, fresh, emit_bf16_weights):
    if emit_bf16_weights:
        y_ref, tg_ref, tv_ref, wg_out_ref, wv_out_ref, wd_out_ref, cg_ref, cv_ref = rest
        for src, dst in ((wg_ref, wg_out_ref), (wv_ref, wv_out_ref), (wd_ref, wd_out_ref)):
            dst[...] = src[...].astype(BF16)
        wg_ref, wv_ref, wd_ref = wg_out_ref, wv_out_ref, wd_out_ref
    else:
        y_ref, tg_ref, tv_ref, cg_ref, cv_ref = rest
    i = pl.program_id(0)
    j = pl.program_id(1)
    tm = groups * rows
    tf = wg_ref.shape[1]
    first_block = (i % blocks_per_seq) == 0
    pad = SUBLANES
    if not fresh and blocks_per_seq != 1:
        raise NotImplementedError("continuing sequences must fit one row block")

    @pl.when(j == 0)
    def _():
        y_ref[...] = jnp.zeros_like(y_ref)

    @pl.when(jnp.logical_and(i == 0, j == 0))
    def _():
        cg_ref[...] = jnp.zeros_like(cg_ref)
        cv_ref[...] = jnp.zeros_like(cv_ref)

    n_sub = FFN_SUB_BLOCKS if groups == 1 else 1
    sub = tm // n_sub
    row_slices = [slice(s * sub, (s + 1) * sub) for s in range(n_sub)]
    ups = []
    for rs in row_slices:
        hn = hn_ref[rs, :]
        ups.append((jnp.dot(hn, wg_ref[...], preferred_element_type=F32),
                    jnp.dot(hn, wv_ref[...], preferred_element_type=F32)))

    def boundary_rows(p_ref, carry_ref):
        if fresh:
            prev = jnp.where(first_block, 0.0, carry_ref[j])
            return prev[:, pad - 2:pad - 1, :], prev[:, pad - 1:pad, :]
        return p_ref[:, 0:1, :], p_ref[:, 1:2, :]

    def conv(u, p0, p1, cw_ref, cb_ref):
        u1 = pltpu.roll(u, 1, 0)
        u2 = pltpu.roll(u, 2, 0)
        if groups == 1:
            top = lax.broadcasted_iota(jnp.int32, (pad, tf), 0)
            u1_top = jnp.where(top == 0, p1, u1[0:pad])
            u2_top = jnp.where(top == 0, p0, jnp.where(top == 1, p1, u2[0:pad]))
            u1 = jnp.concatenate([u1_top, u1[pad:]], axis=0)
            u2 = jnp.concatenate([u2_top, u2[pad:]], axis=0)
        else:
            t_row = lax.broadcasted_iota(jnp.int32, (tm, tf), 0) % rows
            p1r = jnp.broadcast_to(p1, (groups, rows, tf)).reshape(tm, tf)
            p0r = jnp.broadcast_to(p0, (groups, rows, tf)).reshape(tm, tf)
            u1 = jnp.where(t_row == 0, p1r, u1)
            u2 = jnp.where(t_row == 0, p0r, jnp.where(t_row == 1, p1r, u2))
        cw = cw_ref[...]
        uc = cb_ref[...] + cw[0:1] * u2
        uc = uc + cw[1:2] * u1
        return uc + cw[2:3] * u

    pg = boundary_rows(pg_ref, cg_ref)
    pv = boundary_rows(pv_ref, cv_ref)
    if groups == 1:
        pg = (pg[0][0], pg[1][0])
        pv = (pv[0][0], pv[1][0])
    for s, rs in enumerate(row_slices):
        ug, uv = ups[s]
        gated = (_gelu_tanh(conv(ug, pg[0], pg[1], cwg_ref, cbg_ref))
                 * conv(uv, pv[0], pv[1], cwv_ref, cbv_ref)).astype(BF16)
        y_ref[rs, :] += jnp.dot(gated, wd_ref[...], preferred_element_type=F32)
        pg = (ug[sub - 2:sub - 1], ug[sub - 1:sub])
        pv = (uv[sub - 2:sub - 1], uv[sub - 1:sub])

    ug3 = ups[-1][0].reshape(groups, rows // n_sub, tf)
    uv3 = ups[-1][1].reshape(groups, rows // n_sub, tf)
    last = rows // n_sub
    cg_ref[j] = ug3[:, last - pad:last, :]
    cv_ref[j] = uv3[:, last - pad:last, :]
    tg_ref[...] = ug3[:, last - 2:last, :]
    tv_ref[...] = uv3[:, last - 2:last, :]

    @pl.when(j == pl.num_programs(1) - 1)
    def _():
        down = y_ref[...].reshape(groups, rows, y_ref.shape[-1])
        y_ref[...] = x1_ref[...] + (gt_ref[...] * down).reshape(y_ref.shape)


def _ffn_call(hn2, w_gate, w_value, w_down, cw, cb, x1, gt, prev_g, prev_v, *,
              n_seq, groups_per_block, mod_index, fresh, tf):
    n_groups, rows, d = x1.shape
    g = groups_per_block
    tm = g * rows
    n_blocks = n_groups // g
    blocks_per_seq = n_blocks // (n_seq // g)
    d_ff = w_down.shape[0]
    nj = d_ff // tf
    emit = w_down.dtype != BF16
    assert not emit or n_blocks == 1, "f32 weight tiles must be visited exactly once"
    value_offset = nj if w_value.shape[1] == 2 * d_ff else 0
    seq_index = lambda i: i // blocks_per_seq
    x_spec = pl.BlockSpec((tm, d), lambda i, j: (i, 0))
    up_spec = pl.BlockSpec((d, tf), lambda i, j: (0, j))
    down_spec = pl.BlockSpec((tf, d), lambda i, j: (j, 0))
    prev_spec = pl.BlockSpec((g, FFN_CONV_W - 1, tf), lambda i, j: (seq_index(i), 0, j))
    tail_spec = pl.BlockSpec((g, FFN_CONV_W - 1, tf), lambda i, j: (i, 0, j))
    tail_out = jax.ShapeDtypeStruct((n_groups, FFN_CONV_W - 1, d_ff), F32)
    out_specs = [x_spec, tail_spec, tail_spec]
    out_shape = [jax.ShapeDtypeStruct((n_groups * rows, d), F32), tail_out, tail_out]
    if emit:
        out_specs += [up_spec, up_spec, down_spec]
        out_shape += [jax.ShapeDtypeStruct((d, d_ff), BF16), jax.ShapeDtypeStruct((d, d_ff), BF16),
                      jax.ShapeDtypeStruct((d_ff, d), BF16)]
    outs = pl.pallas_call(
        functools.partial(_ffn_kernel, groups=g, rows=rows, blocks_per_seq=blocks_per_seq, fresh=fresh,
                          emit_bf16_weights=emit),
        grid=(n_blocks, nj),
        in_specs=[pl.BlockSpec((tm, d), lambda i, j: (i, 0)),
                  up_spec,
                  pl.BlockSpec((d, tf), lambda i, j: (0, j + value_offset)),
                  pl.BlockSpec((FFN_CONV_W, tf), lambda i, j: (0, j)),
                  pl.BlockSpec((FFN_CONV_W, tf), lambda i, j: (0, j + nj)),
                  pl.BlockSpec((1, tf), lambda i, j: (0, j)),
                  pl.BlockSpec((1, tf), lambda i, j: (0, j + nj)),
                  down_spec,
                  x_spec,
                  pl.BlockSpec((g, 1, d), lambda i, j: (mod_index(i), 0, 0)),
                  prev_spec, prev_spec],
        out_specs=out_specs,
        out_shape=out_shape,
        scratch_shapes=[pltpu.VMEM((nj, g, SUBLANES, tf), F32),
                        pltpu.VMEM((nj, g, SUBLANES, tf), F32)],
        compiler_params=_cparams(2),
        name="convffn",
    )(hn2, w_gate, w_value, cw, cw, cb.reshape(1, 2 * d_ff), cb.reshape(1, 2 * d_ff), w_down,
      x1.reshape(n_groups * rows, d), gt, prev_g, prev_v)
    y, tail_g, tail_v = outs[:3]
    last = lambda t: t.reshape(n_seq, n_groups // n_seq, FFN_CONV_W - 1, d_ff)[:, -1]
    bf16_weights = tuple(outs[3:]) if emit else (w_gate, w_value, w_down)
    return y, last(tail_g), last(tail_v), bf16_weights


def _group_front(x, mod, groups_per_block, rows_per_block, p):
    n_seq, t, d = x.shape
    rows = rows_per_block // groups_per_block
    x3 = x.reshape(n_seq * t // rows, rows, d)
    blocks_per_seq = max(1, t // rows_per_block)
    if groups_per_block == 1:
        mod_index = lambda i: i // blocks_per_seq
    else:
        mod_index = lambda i: i
    mods = [mod[:, n:n + 1, :] for n in range(N_MOD)]
    q, kf, kb, vf, vb, hn = _inproj_call(
        x3, mods[0], mods[1], p["g_norm1"], p["w_qkv"], p["g_q"], p["g_k"],
        groups_per_block=groups_per_block, mod_index=mod_index)
    return dict(x3=x3, mods=mods, mod_index=mod_index, groups_per_block=groups_per_block,
                shape=(n_seq, t, d), q=q, kf=kf, kb=kb, vf=vf, vb=vb, hn=hn)


def _group_back(ctx, attn, lru_prev_pad, lru_h0, ffn_prev, fresh, p, ffn_weights, ffn_tf):
    n_seq, t, d = ctx["shape"]
    g, mod_index = ctx["groups_per_block"], ctx["mod_index"]
    _, _, gt1, sh2, sc2, gt2 = ctx["mods"]
    x1, hn2, h_last, xl_tail = _mixer_call(
        ctx["hn"], attn, ctx["x3"], gt1, sh2, sc2, p["g_norm2"], p["w_lru_in"], p["w_lru_conv"],
        p["b_lru_conv"], p["w_lru_a"], p["b_lru_a"], p["w_lru_x"], p["b_lru_x"], p["lru_lambda"],
        lru_prev_pad, lru_h0, p["w_out"], n_seq=n_seq, groups_per_block=g, mod_index=mod_index, fresh=fresh)
    w_gate, w_value, w_down = ffn_weights
    d_ff = w_down.shape[0]
    y, tail_g, tail_v, ffn_bf16 = _ffn_call(
        hn2, w_gate, w_value, w_down, p["w_ffn_conv"], p["b_ffn_conv"], x1, gt2,
        ffn_prev[:, :, :d_ff], ffn_prev[:, :, d_ff:],
        n_seq=n_seq, groups_per_block=g, mod_index=mod_index, fresh=fresh, tf=ffn_tf)
    width = xl_tail.shape[-1]
    lru_conv_new = xl_tail.reshape(n_seq, -1, SUBLANES, width)[:, -1, SUBLANES - (LRU_CONV_W - 1):, :]
    ffn_conv_new = jnp.concatenate([tail_g, tail_v], axis=-1)
    outs = (y.reshape(n_seq, t, d), ctx["kf"], ctx["vf"], lru_conv_new, h_last.reshape(n_seq, width),
            ffn_conv_new)
    return outs, ffn_bf16


def kernel(x_prompt, x_sample, c_prompt, c_sample, cache_k, cache_v, page_table, state_lru_conv, state_lru_h, state_ffn_conv, w_ada, b_ada, g_norm1, w_in, g_q, g_k, b_sb, w_lru_conv, b_lru_conv, w_lru_a, b_lru_a, w_lru_x, b_lru_x, lru_lambda, w_out, g_norm2, w_up, w_ffn_conv, b_ffn_conv, w_down):
    depth = w_ada.shape[0]
    assert depth == 1, "single-layer step"
    l = 0
    bsz, seq, d = x_prompt.shape
    dec_b, dec_t, _ = x_sample.shape
    page, n_heads = cache_k.shape[2], cache_k.shape[3]
    width = n_heads * HEAD_DIM
    lru_w = state_lru_h.shape[-1]
    d_ff2 = state_ffn_conv.shape[-1]

    p = dict(
        g_norm1=g_norm1[l], w_qkv=w_in[l][:, :3 * width].astype(BF16),
        w_lru_in=w_in[l][:, 3 * width:].astype(BF16), g_q=g_q[l], g_k=g_k[l],
        w_lru_conv=w_lru_conv[l], b_lru_conv=b_lru_conv[l],
        w_lru_a=w_lru_a[l].astype(BF16), b_lru_a=b_lru_a[l],
        w_lru_x=w_lru_x[l].astype(BF16), b_lru_x=b_lru_x[l], lru_lambda=lru_lambda[l],
        w_out=w_out[l].astype(BF16), g_norm2=g_norm2[l],
        w_ffn_conv=w_ffn_conv[l], b_ffn_conv=b_ffn_conv[l])

    n_c = bsz + dec_b
    pad_c = (-n_c) % SUBLANES
    c_all = jnp.concatenate([c_prompt, c_sample, jnp.zeros((pad_c, d), F32)], axis=0)
    mod = _ada_call(c_all, w_ada[l], b_ada[l])
    mod_p = mod[:bsz].reshape(bsz, N_MOD, d)
    mod_s = mod[bsz:n_c].reshape(dec_b, N_MOD, d)

    front_p = _group_front(x_prompt, mod_p, 1, ROW_TILE, p)
    front_s = _group_front(x_sample, mod_s, dec_b, dec_b * dec_t, p)

    pool = cache_k.shape[1]
    attn_s, attn_p = _attn_fused_call(
        front_s["q"].astype(F32).reshape(dec_b, dec_t, width), front_s["kf"], front_s["vf"],
        cache_k[l].reshape(pool, page * n_heads, HEAD_DIM), cache_v[l].reshape(pool, page * n_heads, HEAD_DIM),
        page_table, front_p["q"], front_p["kb"], front_p["vb"], b_sb[l], batch=bsz, seq=seq)

    prev_pad = jnp.concatenate(
        [jnp.zeros((dec_b, SUBLANES - (LRU_CONV_W - 1), lru_w), F32), state_lru_conv[l]], axis=1)
    (ys, ks, vs, lcs, lhs, fcs), ffn_bf16 = _group_back(
        front_s, attn_s.reshape(dec_b * dec_t, width), prev_pad, state_lru_h[l].reshape(dec_b, 1, lru_w),
        state_ffn_conv[l], False, p, (w_up[l], w_up[l], w_down[l]), FFN_TF_SAMPLE)

    zeros_prev = jnp.zeros((bsz, SUBLANES, lru_w), F32)
    zeros_h = jnp.zeros((bsz, 1, lru_w), F32)
    zeros_ffn = jnp.zeros((bsz, FFN_CONV_W - 1, d_ff2), F32)
    (yp, kp, vp, lcp, lhp, fcp), _ = _group_back(
        front_p, attn_p, zeros_prev, zeros_h, zeros_ffn, True, p, ffn_bf16, FFN_TF)

    n_pages_p = seq // page
    return (yp, ys,
            kp.reshape(1, bsz, n_pages_p, page, n_heads, HEAD_DIM),
            vp.reshape(1, bsz, n_pages_p, page, n_heads, HEAD_DIM),
            ks.reshape(1, dec_b, dec_t, n_heads, HEAD_DIM),
            vs.reshape(1, dec_b, dec_t, n_heads, HEAD_DIM),
            lcp[None], lcs[None], lhp[None], lhs[None], fcp[None], fcs[None])
```

```python
import functools
import math

import jax
import jax.numpy as jnp
from jax import lax
from jax.experimental import pallas as pl
from jax.experimental.pallas import tpu as pltpu

F32 = jnp.float32
BF16 = jnp.bfloat16

EPS = 1e-6
LRU_C = 8.0
HEAD_DIM = 128
LRU_BLOCKS = 8
LRU_CONV_W = 4
FFN_CONV_W = 3
N_MOD = 6
SB_SCALE = 1.0 / math.sqrt(HEAD_DIM)

SUBLANES = 8
VMEM_LIMIT = 56 * 1024 * 1024

ROW_TILE = 512
INPROJ_SUB_BLOCKS = 2
MIXER_SUB_BLOCKS = 2
ATTN_TQ = 256
ATTN_TK = 256
ADA_TN = 1024
FFN_TF = 1024
FFN_TF_SAMPLE = 256
FFN_SUB_BLOCKS = 2
PAGES_PER_STEP = 8


def _cparams(n_axes):
    return pltpu.CompilerParams(dimension_semantics=("arbitrary",) * n_axes,
                                vmem_limit_bytes=VMEM_LIMIT)


def _softplus(z):
    return jnp.maximum(z, 0.0) + jnp.log1p(jnp.exp(-jnp.abs(z)))


def _softplus_logits(z):
    return jnp.maximum(z, 0.0) + jnp.log(1.0 + jnp.exp(-jnp.abs(z)))


def _gelu_tanh(x):
    c = math.sqrt(2.0 / math.pi)
    return x * (0.5 * (1.0 + jnp.tanh(c * (x + 0.044715 * (x * x * x)))))


def _suffix_sum_matrix(n_keys):
    j = lax.broadcasted_iota(jnp.int32, (n_keys, n_keys), 0)
    s = lax.broadcasted_iota(jnp.int32, (n_keys, n_keys), 1)
    return jnp.where(j >= s, 1.0, 0.0).astype(BF16)


def _ada_kernel(c_ref, w_ref, b_ref, o_ref):
    c = c_ref[...]
    s = (c * jax.nn.sigmoid(c)).astype(BF16)
    o_ref[...] = jnp.dot(s, w_ref[...].astype(BF16), preferred_element_type=F32) + b_ref[...]


def _ada_call(c_all, w_ada, b_ada):
    m, d = c_all.shape
    n = w_ada.shape[1]
    return pl.pallas_call(
        _ada_kernel,
        grid=(n // ADA_TN,),
        in_specs=[pl.BlockSpec((m, d), lambda j: (0, 0)),
                  pl.BlockSpec((d, ADA_TN), lambda j: (0, j)),
                  pl.BlockSpec((1, ADA_TN), lambda j: (0, j))],
        out_specs=pl.BlockSpec((m, ADA_TN), lambda j: (0, j)),
        out_shape=jax.ShapeDtypeStruct((m, n), F32),
        compiler_params=_cparams(1),
        name="adaln",
    )(c_all, w_ada, b_ada.reshape(1, n))


def _head_rmsnorm(a, g, n_heads):
    outs = []
    for h in range(n_heads):
        ah = a[:, h * HEAD_DIM:(h + 1) * HEAD_DIM]
        ms = jnp.mean(ah * ah, axis=-1, keepdims=True)
        outs.append(ah * lax.rsqrt(ms + EPS) * g)
    return outs


def _inproj_kernel(x_ref, sh_ref, sc_ref, g1_ref, wq_ref, wk_ref, wv_ref, gq_ref, gk_ref,
                   q_ref, kf_ref, kb_ref, vf_ref, vb_ref, hn_ref, *, n_heads, n_sub):
    tm, d = hn_ref.shape
    sub = tm // n_sub
    row_slices = [slice(s * sub, (s + 1) * sub) for s in range(n_sub)]
    head_cols = [slice(h * HEAD_DIM, (h + 1) * HEAD_DIM) for h in range(n_heads)]

    def normalize(rs):
        x = x_ref[...] if n_sub == 1 else x_ref[:, rs, :]
        ms = jnp.mean(x * x, axis=-1, keepdims=True)
        y = x * lax.rsqrt(ms + EPS) * g1_ref[...]
        hn = y * (1.0 + sc_ref[...]) + sh_ref[...]
        hn_ref[rs, :] = hn.reshape(sub, d).astype(BF16)

    def project(t, epilogue):
        w_ref = (wq_ref, wk_ref, wv_ref)[t]
        accs = [jnp.dot(hn_ref[rs, :], w_ref[...], preferred_element_type=F32) for rs in row_slices]
        for rs, acc in zip(row_slices, accs):
            epilogue(rs, acc)

    def head_rows(rs, h):
        return pl.ds(rs.start * n_heads + h, sub, stride=n_heads)

    def q_epilogue(rs, acc):
        for h, qh in enumerate(_head_rmsnorm(acc, gq_ref[...], n_heads)):
            q_ref[rs, head_cols[h]] = (qh * SB_SCALE).astype(BF16)

    def k_epilogue(rs, acc):
        for h, kh in enumerate(_head_rmsnorm(acc, gk_ref[...], n_heads)):
            kf_ref[head_rows(rs, h), :] = kh
            kb_ref[rs, head_cols[h]] = kh.astype(BF16)

    def v_epilogue(rs, acc):
        for h in range(n_heads):
            vf_ref[head_rows(rs, h), :] = acc[:, head_cols[h]]
        vb_ref[rs, :] = acc.astype(BF16)

    for rs in row_slices:
        normalize(rs)
    for t, epilogue in enumerate([q_epilogue, k_epilogue, v_epilogue]):
        project(t, epilogue)


def _inproj_call(x3, sh, sc, g1, w_in_bf, g_q, g_k, *, width, groups_per_block, mod_index):
    n_groups, rows, d = x3.shape
    g = groups_per_block
    tm = g * rows
    m = n_groups * rows
    n_blocks = n_groups // g
    n_heads = width // HEAD_DIM
    row_spec = pl.BlockSpec((tm, width), lambda i: (i, 0))
    head_spec = pl.BlockSpec((tm * n_heads, HEAD_DIM), lambda i: (i, 0))
    mod_spec = pl.BlockSpec((g, 1, d), lambda i: (mod_index(i), 0, 0))
    head_out = jax.ShapeDtypeStruct((m * n_heads, HEAD_DIM), F32)
    bf_out = jax.ShapeDtypeStruct((m, width), BF16)
    return pl.pallas_call(
        functools.partial(_inproj_kernel, n_heads=n_heads, n_sub=INPROJ_SUB_BLOCKS if g == 1 else 1),
        grid=(n_blocks,),
        in_specs=[pl.BlockSpec((g, rows, d), lambda i: (i, 0, 0)),
                  mod_spec, mod_spec,
                  pl.BlockSpec((1, d), lambda i: (0, 0)),
                  *[pl.BlockSpec((d, width), lambda i, t=t: (0, t), pipeline_mode=pl.Buffered(1))
                    for t in range(3)],
                  pl.BlockSpec((1, HEAD_DIM), lambda i: (0, 0)),
                  pl.BlockSpec((1, HEAD_DIM), lambda i: (0, 0))],
        out_specs=[row_spec, head_spec, row_spec, head_spec, row_spec,
                   pl.BlockSpec((tm, d), lambda i: (i, 0))],
        out_shape=[bf_out, head_out, bf_out, head_out, bf_out, jax.ShapeDtypeStruct((m, d), BF16)],
        compiler_params=_cparams(1),
        name="inproj",
    )(x3, sh, sc, g1.reshape(1, d), w_in_bf, w_in_bf, w_in_bf, g_q.reshape(1, HEAD_DIM),
      g_k.reshape(1, HEAD_DIM))


def _sb_mass(zs, suffix_mat, mask):
    cums = []
    for z in zs:
        sp = _softplus_logits(z)
        if mask is not None:
            sp = jnp.where(mask, sp, 0.0)
        cums.append(jnp.dot(sp.astype(BF16), suffix_mat, preferred_element_type=F32))
    return cums


def _sb_weights(z, cums, later, mask):
    w = jnp.exp(z - cums - later)
    if mask is not None:
        w = jnp.where(mask, w, 0.0)
    return w, later + cums[:, 0:1]


def _attn_fused_kernel(pt_ref, sb_ref, sqi_ref, sj_ref, bias_ref, qs_ref, knew_ref, vnew_ref, *rest,
                       n_heads, n_new, sample_steps, steps_per_seq):
    del pt_ref, sb_ref
    pages = PAGES_PER_STEP
    k_refs = rest[:pages]
    v_refs = rest[pages:2 * pages]
    qp_ref, kp_ref, vp_ref, os_ref, op_ref = rest[2 * pages:2 * pages + 5]
    kpad_ref, vpad_ref, acc_ref, later_ref, o_scr, later_scr = rest[2 * pages + 5:]
    s = pl.program_id(0)
    active = s < sample_steps
    g = s % steps_per_seq
    qi = sqi_ref[s]
    j = sj_ref[s]
    page_len = kpad_ref.shape[0] // n_heads
    page_suffix = _suffix_sum_matrix(page_len)
    tile_suffix = _suffix_sum_matrix(ATTN_TK)
    heads = range(n_heads)
    cols = [slice(h * HEAD_DIM, (h + 1) * HEAD_DIM) for h in heads]
    bias_rows = jnp.concatenate([jnp.full((n_new, 1), bias_ref[h], F32) for h in heads], axis=0)
    nt = (((1,), (1,)), ((), ()))

    def sample_logits(k_pages):
        qh = [qs_ref[0, :, cols[h]].astype(BF16) for h in heads]
        zs = []
        for kp in k_pages:
            z = [lax.dot_general(qh[h], kp[pl.ds(h, page_len, stride=n_heads), :].astype(BF16), nt,
                                 preferred_element_type=F32) for h in heads]
            zs.append(jnp.concatenate(z, axis=0) + bias_rows)
        return zs

    def sample_weights(zs, cums, mask):
        later = later_ref[...]
        ws = []
        for z, c in zip(zs, cums):
            w, later = _sb_weights(z, c, later, mask)
            ws.append(w)
        return ws, later

    def sample_values(ws, v_pages):
        acc = [acc_ref[h] for h in heads]
        for w, vp in zip(ws, v_pages):
            for h in heads:
                wh = w[h * n_new:(h + 1) * n_new].astype(BF16)
                vh = vp[pl.ds(h, page_len, stride=n_heads), :].astype(BF16)
                acc[h] = acc[h] + jnp.dot(wh, vh, preferred_element_type=F32)
        return acc

    @pl.when(jnp.logical_and(active, g == 0))
    def _():
        acc_ref[...] = jnp.zeros_like(acc_ref)
        later_ref[...] = jnp.zeros_like(later_ref)
        kpad_ref[...] = jnp.zeros_like(kpad_ref)
        vpad_ref[...] = jnp.zeros_like(vpad_ref)
        kpad_ref[0:n_new * n_heads, :] = knew_ref[...]
        vpad_ref[0:n_new * n_heads, :] = vnew_ref[...]
        row = lax.broadcasted_iota(jnp.int32, (n_heads * n_new, page_len), 0) % n_new
        col = lax.broadcasted_iota(jnp.int32, (n_heads * n_new, page_len), 1)
        mask = col < row
        zs = sample_logits([kpad_ref])
        ws, later = sample_weights(zs, _sb_mass(zs, page_suffix, mask), mask)
        later_ref[...] = later
        for h, a in enumerate(sample_values(ws, [vpad_ref])):
            acc_ref[h] = a

    def step(diagonal):
        k_pages = [r.at[0] for r in k_refs]
        v_pages = [r.at[0] for r in v_refs]
        start = pl.multiple_of(j * ATTN_TK, ATTN_TK)
        if diagonal:
            row = lax.broadcasted_iota(jnp.int32, (ATTN_TQ, ATTN_TK), 0)
            col = lax.broadcasted_iota(jnp.int32, (ATTN_TQ, ATTN_TK), 1)
            mask = col < row
        else:
            mask = None
        zs_s = sample_logits(k_pages)
        zs_p = [lax.dot_general(qp_ref[:, cols[h]], kp_ref[pl.ds(start, ATTN_TK), cols[h]], nt,
                                preferred_element_type=F32) + bias_ref[h] for h in heads]
        cums_s = _sb_mass(zs_s, page_suffix, None)
        cums_p = _sb_mass(zs_p, tile_suffix, mask)
        ws_s, later_s = sample_weights(zs_s, cums_s, None)
        later_ref[...] = jnp.where(active, later_s, later_ref[...])
        ws_p = []
        for h in heads:
            later_p = jnp.zeros((ATTN_TQ, 1), F32) if diagonal else later_scr[h]
            w, later_p = _sb_weights(zs_p[h], cums_p[h], later_p, mask)
            later_scr[h] = later_p
            ws_p.append(w)
        for h, a in enumerate(sample_values(ws_s, v_pages)):
            acc_ref[h] = jnp.where(active, a, acc_ref[h])
        for h in heads:
            pv = jnp.dot(ws_p[h].astype(BF16), vp_ref[pl.ds(start, ATTN_TK), cols[h]],
                         preferred_element_type=F32)
            o_scr[:, cols[h]] = pv if diagonal else o_scr[:, cols[h]] + pv

    pl.when(j == qi)(functools.partial(step, True))
    pl.when(j != qi)(functools.partial(step, False))

    @pl.when(jnp.logical_and(active, g == steps_per_seq - 1))
    def _():
        for h in heads:
            os_ref[0, :, cols[h]] = acc_ref[h].astype(os_ref.dtype)

    @pl.when(j == 0)
    def _():
        op_ref[...] = o_scr[...].astype(op_ref.dtype)


def _attn_fused_call(q_s3, knew, vnew, cache_k, cache_v, page_table, q_p, k_p, v_p, b_sb, *, batch, seq):
    bsz, n_new, width = q_s3.shape
    n_heads = width // HEAD_DIM
    n_pages = page_table.shape[1]
    page_rows = cache_k.shape[1]
    pages = PAGES_PER_STEP
    steps_per_seq = n_pages // pages
    sample_steps = bsz * steps_per_seq
    nq = seq // ATTN_TQ
    sched = [(b, qi, j) for b in range(batch) for qi in range(nq) for j in range(qi, -1, -1)]
    n_steps = max(len(sched), sample_steps)
    assert len(sched) == n_steps, "prompt schedule must cover every grid step"
    sb, sqi, sj = (jnp.asarray([t[n] for t in sched], jnp.int32) for n in range(3))

    def sample_pos(s):
        s = jnp.minimum(s, sample_steps - 1)
        return s // steps_per_seq, s % steps_per_seq

    def page_spec(p):
        def index(s, pt, sb, sqi, sj):
            b, g = sample_pos(s)
            return (pt[b, n_pages - 1 - (g * pages + p)], 0, 0)
        return pl.BlockSpec((1, page_rows, HEAD_DIM), index)

    new_spec = pl.BlockSpec((n_new * n_heads, HEAD_DIM), lambda s, pt, sb, sqi, sj: (sample_pos(s)[0], 0))
    qs_spec = pl.BlockSpec((1, n_new, width), lambda s, pt, sb, sqi, sj: (sample_pos(s)[0], 0, 0))
    qp_spec = pl.BlockSpec((ATTN_TQ, width), lambda s, pt, sb, sqi, sj: (sb[s] * nq + sqi[s], 0))
    kv_spec = pl.BlockSpec((seq, width), lambda s, pt, sb, sqi, sj: (sb[s], 0), pipeline_mode=pl.Buffered(1))
    grid_spec = pltpu.PrefetchScalarGridSpec(
        num_scalar_prefetch=4,
        grid=(n_steps,),
        in_specs=([pl.BlockSpec(memory_space=pltpu.SMEM), qs_spec, new_spec, new_spec]
                  + [page_spec(p) for p in range(pages)] * 2
                  + [qp_spec, kv_spec, kv_spec]),
        out_specs=[qs_spec, qp_spec],
        scratch_shapes=[pltpu.VMEM((page_rows, HEAD_DIM), F32),
                        pltpu.VMEM((page_rows, HEAD_DIM), F32),
                        pltpu.VMEM((n_heads, n_new, HEAD_DIM), F32),
                        pltpu.VMEM((n_heads * n_new, 1), F32),
                        pltpu.VMEM((ATTN_TQ, width), F32),
                        pltpu.VMEM((n_heads, ATTN_TQ, 1), F32)],
    )
    return pl.pallas_call(
        functools.partial(_attn_fused_kernel, n_heads=n_heads, n_new=n_new, sample_steps=sample_steps,
                          steps_per_seq=steps_per_seq),
        grid_spec=grid_spec,
        out_shape=[jax.ShapeDtypeStruct((bsz, n_new, width), F32),
                   jax.ShapeDtypeStruct(q_p.shape, BF16)],
        compiler_params=_cparams(1),
        name="attn_fused",
    )(page_table, sb, sqi, sj, b_sb, q_s3, knew, vnew, *([cache_k] * pages), *([cache_v] * pages),
      q_p, k_p, v_p)


def _segmented_linear_scan(a, b, seg_len):
    rows = lax.broadcasted_iota(jnp.int32, a.shape, 0) % seg_len
    d = 1
    while d < seg_len:
        keep = rows >= d
        a_prev = pltpu.roll(a, d, 0)
        b_prev = pltpu.roll(b, d, 0)
        b = jnp.where(keep, a * b_prev + b, b)
        a = jnp.where(keep, a * a_prev, a)
        d *= 2
    return a, b


def _mixer_kernel(hn_ref, attn_ref, x_ref, gt_ref, sh_ref, sc_ref, g2_ref,
                  wxl_ref, wgl_ref, cw_ref, cb_ref, wa_ref, ba_ref, wx_ref, bx_ref, lam_ref, prev_ref, h0_ref,
                  wo_ref, x1_ref, hn2_ref, hlast_ref, xtail_ref, xp_ref, hcar_ref,
                  *, groups, rows, blocks_per_seq, fresh, n_sub):
    i = pl.program_id(0)
    width = xp_ref.shape[-1]
    d = x_ref.shape[-1]
    first_block = (i % blocks_per_seq) == 0
    pad = SUBLANES
    sub_rows = rows // n_sub
    sub = groups * sub_rows
    row_slices = [slice(s * sub, (s + 1) * sub) for s in range(n_sub)]

    @pl.when(first_block)
    def _():
        if fresh:
            xp_ref[:, 0:pad, :] = jnp.zeros((groups, pad, width), F32)
            hcar_ref[...] = jnp.zeros_like(hcar_ref)
        else:
            xp_ref[:, 0:pad, :] = prev_ref[...]
            hcar_ref[...] = h0_ref[...]

    xls = [jnp.dot(hn_ref[rs, :], wxl_ref[...], preferred_element_type=F32) for rs in row_slices]
    gls = [jnp.dot(hn_ref[rs, :], wgl_ref[...], preferred_element_type=F32) for rs in row_slices]
    mixes = [jnp.dot(attn_ref[rs, :].astype(BF16), wo_ref[0:width, :], preferred_element_type=F32)
             for rs in row_slices]
    for s in range(n_sub):
        xp_ref[:, pad + s * sub_rows:pad + (s + 1) * sub_rows, :] = (
            xls[s].reshape(groups, sub_rows, width))

    cw = cw_ref[...]
    bw = width // LRU_BLOCKS
    neg_c_softplus = (-LRU_C) * _softplus(-lam_ref[...])
    h_prev = hcar_ref[...] if rows == SUBLANES else hcar_ref[0]
    for s, rs in enumerate(row_slices):
        t0 = s * sub_rows
        xc = cb_ref[...] + cw[0:1] * xp_ref[:, pl.ds(pad - 3 + t0, sub_rows), :]
        for t in range(1, LRU_CONV_W):
            xc = xc + cw[t:t + 1] * xp_ref[:, pl.ds(pad - 3 + t0 + t, sub_rows), :]
        xc = xc.reshape(sub, width)

        xcb = xc.astype(BF16)
        r_parts, i_parts = [], []
        for n in range(LRU_BLOCKS):
            xb = xcb[:, n * bw:(n + 1) * bw]
            r_parts.append(jnp.dot(xb, wa_ref[n], preferred_element_type=F32))
            i_parts.append(jnp.dot(xb, wx_ref[n], preferred_element_type=F32))
        r = jax.nn.sigmoid(jnp.concatenate(r_parts, axis=1) + ba_ref[...])
        ig = jax.nn.sigmoid(jnp.concatenate(i_parts, axis=1) + bx_ref[...])
        log_a = r * neg_c_softplus
        a = jnp.exp(log_a)
        one_minus_a2 = -jnp.tanh(log_a) * (a * a + 1.0)
        mult = jnp.where(one_minus_a2 > 0.0, one_minus_a2 * lax.rsqrt(one_minus_a2), 0.0)
        if fresh and s == 0:
            t_row = lax.broadcasted_iota(jnp.int32, (sub, 1), 0)
            start_row = jnp.where(first_block, 0, -1)
            mult = jnp.where(t_row == start_row, 1.0, mult)
        b = mult * ig * xc

        a_cum, h = _segmented_linear_scan(a, b, SUBLANES)
        if rows == SUBLANES:
            h = h + a_cum * jnp.broadcast_to(h_prev, (groups, rows, width)).reshape(sub, width)
            h_prev = h.reshape(groups, rows, width)[:, rows - 1:rows, :]
        else:
            tiles = []
            for t in range(sub_rows // SUBLANES):
                ts = slice(t * SUBLANES, (t + 1) * SUBLANES)
                h_t = h[ts] + a_cum[ts] * h_prev
                tiles.append(h_t)
                h_prev = h_t[SUBLANES - 1:SUBLANES]
            h = jnp.concatenate(tiles, axis=0)
        lru = (h * _gelu_tanh(gls[s])).astype(BF16)

        mix = mixes[s] + jnp.dot(lru, wo_ref[width:2 * width, :], preferred_element_type=F32)
        x1 = x_ref[:, t0:t0 + sub_rows, :] + gt_ref[...] * mix.reshape(groups, sub_rows, d)
        x1_ref[:, t0:t0 + sub_rows, :] = x1
        ms = jnp.mean(x1 * x1, axis=-1, keepdims=True)
        y = x1 * lax.rsqrt(ms + EPS) * g2_ref[...]
        hn2 = y * (1.0 + sc_ref[...]) + sh_ref[...]
        hn2_ref[rs, :] = hn2.reshape(sub, d).astype(BF16)

    h_last = h_prev if rows == SUBLANES else h_prev[None]
    hcar_ref[...] = h_last
    hlast_ref[...] = h_last
    tail = xp_ref[:, rows:rows + pad, :]
    xtail_ref[...] = tail
    xp_ref[:, 0:pad, :] = tail


def _mixer_call(hn, attn, x3, gt, sh, sc, g2, w_in_bf, cw, cb, wa_bf, ba, wx_bf, bx, lam, prev_pad, h0,
                w_out_bf, *, n_seq, groups_per_block, mod_index, fresh):
    n_groups, rows, d = x3.shape
    g = groups_per_block
    tm = g * rows
    m = n_groups * rows
    n_blocks = n_groups // g
    blocks_per_seq = n_blocks // (n_seq // g)
    width = attn.shape[1]
    bw = width // LRU_BLOCKS
    resident = pl.Buffered(1)
    x_spec = pl.BlockSpec((g, rows, d), lambda i: (i, 0, 0))
    mod_spec = pl.BlockSpec((g, 1, d), lambda i: (mod_index(i), 0, 0))
    vec_spec = pl.BlockSpec((1, width), lambda i: (0, 0))
    gate_spec = pl.BlockSpec((LRU_BLOCKS, bw, bw), lambda i: (0, 0, 0))
    seq_index = lambda i: (i // blocks_per_seq, 0, 0)
    return pl.pallas_call(
        functools.partial(_mixer_kernel, groups=g, rows=rows, blocks_per_seq=blocks_per_seq, fresh=fresh,
                          n_sub=MIXER_SUB_BLOCKS if g == 1 else 1),
        grid=(n_blocks,),
        in_specs=[pl.BlockSpec((tm, d), lambda i: (i, 0)),
                  pl.BlockSpec((tm, width), lambda i: (i, 0)),
                  x_spec, mod_spec, mod_spec, mod_spec,
                  pl.BlockSpec((1, d), lambda i: (0, 0)),
                  pl.BlockSpec((d, width), lambda i: (0, 3), pipeline_mode=resident),
                  pl.BlockSpec((d, width), lambda i: (0, 4), pipeline_mode=resident),
                  pl.BlockSpec((LRU_CONV_W, width), lambda i: (0, 0)), vec_spec,
                  gate_spec, vec_spec, gate_spec, vec_spec, vec_spec,
                  pl.BlockSpec((g, SUBLANES, width), seq_index),
                  pl.BlockSpec((g, 1, width), seq_index),
                  pl.BlockSpec((d, d), lambda i: (0, 0), pipeline_mode=resident)],
        out_specs=[x_spec,
                   pl.BlockSpec((tm, d), lambda i: (i, 0)),
                   pl.BlockSpec((g, 1, width), seq_index),
                   pl.BlockSpec((g, SUBLANES, width), lambda i: (i, 0, 0))],
        out_shape=[jax.ShapeDtypeStruct((n_groups, rows, d), F32),
                   jax.ShapeDtypeStruct((m, d), BF16),
                   jax.ShapeDtypeStruct((n_seq, 1, width), F32),
                   jax.ShapeDtypeStruct((n_groups, SUBLANES, width), F32)],
        scratch_shapes=[pltpu.VMEM((g, rows + SUBLANES, width), F32),
                        pltpu.VMEM((g, 1, width), F32)],
        compiler_params=_cparams(1),
        name="mixer",
    )(hn, attn, x3, gt, sh, sc, g2.reshape(1, d), w_in_bf, w_in_bf, cw, cb.reshape(1, width), wa_bf,
      ba.reshape(1, width), wx_bf, bx.reshape(1, width), lam.reshape(1, width), prev_pad, h0, w_out_bf)


def _ffn_kernel(hn_ref, wg_ref, wv_ref, cwg_ref, cwv_ref, cbg_ref, cbv_ref, wd_ref,
                x1_ref, gt_ref, pg_ref, pv_ref, *rest,
                groups, rows, blocks_per_seq, fresh, emit_bf16_weights):
    if emit_bf16_weights:
        y_ref, tg_ref, tv_ref, wg_out_ref, wv_out_ref, wd_out_ref, cg_ref, cv_ref = rest
        for src, dst in ((wg_ref, wg_out_ref), (wv_ref, wv_out_ref), (wd_ref, wd_out_ref)):
            dst[...] = src[...].astype(BF16)
        wg_ref, wv_ref, wd_ref = wg_out_ref, wv_out_ref, wd_out_ref
    else:
        y_ref, tg_ref, tv_ref, cg_ref, cv_ref = rest
    i = pl.program_id(0)
    j = pl.program_id(1)
    tm = groups * rows
    tf = wg_ref.shape[1]
    first_block = (i % blocks_per_seq) == 0
    pad = SUBLANES
    if not fresh and blocks_per_seq != 1:
        raise NotImplementedError("continuing sequences must fit one row block")

    @pl.when(j == 0)
    def _():
        y_ref[...] = jnp.zeros_like(y_ref)

    @pl.when(jnp.logical_and(i == 0, j == 0))
    def _():
        cg_ref[...] = jnp.zeros_like(cg_ref)
        cv_ref[...] = jnp.zeros_like(cv_ref)

    n_sub = FFN_SUB_BLOCKS if groups == 1 else 1
    sub = tm // n_sub
    row_slices = [slice(s * sub, (s + 1) * sub) for s in range(n_sub)]
    ups = []
    for rs in row_slices:
        hn = hn_ref[rs, :]
        ups.append((jnp.dot(hn, wg_ref[...], preferred_element_type=F32),
                    jnp.dot(hn, wv_ref[...], preferred_element_type=F32)))

    def boundary_rows(p_ref, carry_ref):
        if fresh:
            prev = jnp.where(first_block, 0.0, carry_ref[j])
            return prev[:, pad - 2:pad - 1, :], prev[:, pad - 1:pad, :]
        return p_ref[:, 0:1, :], p_ref[:, 1:2, :]

    def conv(u, p0, p1, cw_ref, cb_ref):
        u1 = pltpu.roll(u, 1, 0)
        u2 = pltpu.roll(u, 2, 0)
        if groups == 1:
            top = lax.broadcasted_iota(jnp.int32, (pad, tf), 0)
            u1_top = jnp.where(top == 0, p1, u1[0:pad])
            u2_top = jnp.where(top == 0, p0, jnp.where(top == 1, p1, u2[0:pad]))
            u1 = jnp.concatenate([u1_top, u1[pad:]], axis=0)
            u2 = jnp.concatenate([u2_top, u2[pad:]], axis=0)
        else:
            t_row = lax.broadcasted_iota(jnp.int32, (tm, tf), 0) % rows
            p1r = jnp.broadcast_to(p1, (groups, rows, tf)).reshape(tm, tf)
            p0r = jnp.broadcast_to(p0, (groups, rows, tf)).reshape(tm, tf)
            u1 = jnp.where(t_row == 0, p1r, u1)
            u2 = jnp.where(t_row == 0, p0r, jnp.where(t_row == 1, p1r, u2))
        cw = cw_ref[...]
        uc = cb_ref[...] + cw[0:1] * u2
        uc = uc + cw[1:2] * u1
        return uc + cw[2:3] * u

    pg = boundary_rows(pg_ref, cg_ref)
    pv = boundary_rows(pv_ref, cv_ref)
    if groups == 1:
        pg = (pg[0][0], pg[1][0])
        pv = (pv[0][0], pv[1][0])
    for s, rs in enumerate(row_slices):
        ug, uv = ups[s]
        gated = (_gelu_tanh(conv(ug, pg[0], pg[1], cwg_ref, cbg_ref))
                 * conv(uv, pv[0], pv[1], cwv_ref, cbv_ref)).astype(BF16)
        y_ref[rs, :] += jnp.dot(gated, wd_ref[...], preferred_element_type=F32)
        pg = (ug[sub - 2:sub - 1], ug[sub - 1:sub])
        pv = (uv[sub - 2:sub - 1], uv[sub - 1:sub])

    ug3 = ups[-1][0].reshape(groups, rows // n_sub, tf)
    uv3 = ups[-1][1].reshape(groups, rows // n_sub, tf)
    last = rows // n_sub
    cg_ref[j] = ug3[:, last - pad:last, :]
    cv_ref[j] = uv3[:, last - pad:last, :]
    tg_ref[...] = ug3[:, last - 2:last, :]
    tv_ref[...] = uv3[:, last - 2:last, :]

    @pl.when(j == pl.num_programs(1) - 1)
    def _():
        down = y_ref[...].reshape(groups, rows, y_ref.shape[-1])
        y_ref[...] = x1_ref[...] + (gt_ref[...] * down).reshape(y_ref.shape)


def _ffn_call(hn2, w_gate, w_value, w_down, cw, cb, x1, gt, prev_g, prev_v, *,
              n_seq, groups_per_block, mod_index, fresh, tf):
    n_groups, rows, d = x1.shape
    g = groups_per_block
    tm = g * rows
    n_blocks = n_groups // g
    blocks_per_seq = n_blocks // (n_seq // g)
    d_ff = w_down.shape[0]
    nj = d_ff // tf
    emit = w_down.dtype != BF16
    assert not emit or n_blocks == 1, "f32 weight tiles must be visited exactly once"
    value_offset = nj if w_value.shape[1] == 2 * d_ff else 0
    seq_index = lambda i: i // blocks_per_seq
    x_spec = pl.BlockSpec((tm, d), lambda i, j: (i, 0))
    up_spec = pl.BlockSpec((d, tf), lambda i, j: (0, j))
    down_spec = pl.BlockSpec((tf, d), lambda i, j: (j, 0))
    prev_spec = pl.BlockSpec((g, FFN_CONV_W - 1, tf), lambda i, j: (seq_index(i), 0, j))
    tail_spec = pl.BlockSpec((g, FFN_CONV_W - 1, tf), lambda i, j: (i, 0, j))
    tail_out = jax.ShapeDtypeStruct((n_groups, FFN_CONV_W - 1, d_ff), F32)
    out_specs = [x_spec, tail_spec, tail_spec]
    out_shape = [jax.ShapeDtypeStruct((n_groups * rows, d), F32), tail_out, tail_out]
    if emit:
        out_specs += [up_spec, up_spec, down_spec]
        out_shape += [jax.ShapeDtypeStruct((d, d_ff), BF16), jax.ShapeDtypeStruct((d, d_ff), BF16),
                      jax.ShapeDtypeStruct((d_ff, d), BF16)]
    outs = pl.pallas_call(
        functools.partial(_ffn_kernel, groups=g, rows=rows, blocks_per_seq=blocks_per_seq, fresh=fresh,
                          emit_bf16_weights=emit),
        grid=(n_blocks, nj),
        in_specs=[pl.BlockSpec((tm, d), lambda i, j: (i, 0)),
                  up_spec,
                  pl.BlockSpec((d, tf), lambda i, j: (0, j + value_offset)),
                  pl.BlockSpec((FFN_CONV_W, tf), lambda i, j: (0, j)),
                  pl.BlockSpec((FFN_CONV_W, tf), lambda i, j: (0, j + nj)),
                  pl.BlockSpec((1, tf), lambda i, j: (0, j)),
                  pl.BlockSpec((1, tf), lambda i, j: (0, j + nj)),
                  down_spec,
                  x_spec,
                  pl.BlockSpec((g, 1, d), lambda i, j: (mod_index(i), 0, 0)),
                  prev_spec, prev_spec],
        out_specs=out_specs,
        out_shape=out_shape,
        scratch_shapes=[pltpu.VMEM((nj, g, SUBLANES, tf), F32),
                        pltpu.VMEM((nj, g, SUBLANES, tf), F32)],
        compiler_params=_cparams(2),
        name="convffn",
    )(hn2, w_gate, w_value, cw, cw, cb.reshape(1, 2 * d_ff), cb.reshape(1, 2 * d_ff), w_down,
      x1.reshape(n_groups * rows, d), gt, prev_g, prev_v)
    y, tail_g, tail_v = outs[:3]
    last = lambda t: t.reshape(n_seq, n_groups // n_seq, FFN_CONV_W - 1, d_ff)[:, -1]
    bf16_weights = tuple(outs[3:]) if emit else (w_gate, w_value, w_down)
    return y, last(tail_g), last(tail_v), bf16_weights


def _group_front(x, mod, groups_per_block, rows_per_block, p):
    n_seq, t, d = x.shape
    rows = rows_per_block // groups_per_block
    x3 = x.reshape(n_seq * t // rows, rows, d)
    blocks_per_seq = max(1, t // rows_per_block)
    if groups_per_block == 1:
        mod_index = lambda i: i // blocks_per_seq
    else:
        mod_index = lambda i: i
    mods = [mod[:, n:n + 1, :] for n in range(N_MOD)]
    q, kf, kb, vf, vb, hn = _inproj_call(
        x3, mods[0], mods[1], p["g_norm1"], p["w_in"], p["g_q"], p["g_k"],
        width=p["width"], groups_per_block=groups_per_block, mod_index=mod_index)
    return dict(x3=x3, mods=mods, mod_index=mod_index, groups_per_block=groups_per_block,
                shape=(n_seq, t, d), q=q, kf=kf, kb=kb, vf=vf, vb=vb, hn=hn)


def _group_back(ctx, attn, lru_prev_pad, lru_h0, ffn_prev, fresh, p, ffn_weights, ffn_tf):
    n_seq, t, d = ctx["shape"]
    g, mod_index = ctx["groups_per_block"], ctx["mod_index"]
    _, _, gt1, sh2, sc2, gt2 = ctx["mods"]
    x1, hn2, h_last, xl_tail = _mixer_call(
        ctx["hn"], attn, ctx["x3"], gt1, sh2, sc2, p["g_norm2"], p["w_in"], p["w_lru_conv"],
        p["b_lru_conv"], p["w_lru_a"], p["b_lru_a"], p["w_lru_x"], p["b_lru_x"], p["lru_lambda"],
        lru_prev_pad, lru_h0, p["w_out"], n_seq=n_seq, groups_per_block=g, mod_index=mod_index, fresh=fresh)
    w_gate, w_value, w_down = ffn_weights
    d_ff = w_down.shape[0]
    y, tail_g, tail_v, ffn_bf16 = _ffn_call(
        hn2, w_gate, w_value, w_down, p["w_ffn_conv"], p["b_ffn_conv"], x1, gt2,
        ffn_prev[:, :, :d_ff], ffn_prev[:, :, d_ff:],
        n_seq=n_seq, groups_per_block=g, mod_index=mod_index, fresh=fresh, tf=ffn_tf)
    width = xl_tail.shape[-1]
    lru_conv_new = xl_tail.reshape(n_seq, -1, SUBLANES, width)[:, -1, SUBLANES - (LRU_CONV_W - 1):, :]
    ffn_conv_new = jnp.concatenate([tail_g, tail_v], axis=-1)
    outs = (y.reshape(n_seq, t, d), ctx["kf"], ctx["vf"], lru_conv_new, h_last.reshape(n_seq, width),
            ffn_conv_new)
    return outs, ffn_bf16


def kernel(x_prompt, x_sample, c_prompt, c_sample, cache_k, cache_v, page_table, state_lru_conv, state_lru_h, state_ffn_conv, w_ada, b_ada, g_norm1, w_in, g_q, g_k, b_sb, w_lru_conv, b_lru_conv, w_lru_a, b_lru_a, w_lru_x, b_lru_x, lru_lambda, w_out, g_norm2, w_up, w_ffn_conv, b_ffn_conv, w_down):
    depth = w_ada.shape[0]
    assert depth == 1, "single-layer step"
    l = 0
    bsz, seq, d = x_prompt.shape
    dec_b, dec_t, _ = x_sample.shape
    page, n_heads = cache_k.shape[2], cache_k.shape[3]
    width = n_heads * HEAD_DIM
    lru_w = state_lru_h.shape[-1]
    d_ff2 = state_ffn_conv.shape[-1]

    p = dict(
        width=width, g_norm1=g_norm1[l], w_in=w_in[l].astype(BF16), g_q=g_q[l], g_k=g_k[l],
        w_lru_conv=w_lru_conv[l], b_lru_conv=b_lru_conv[l],
        w_lru_a=w_lru_a[l].astype(BF16), b_lru_a=b_lru_a[l],
        w_lru_x=w_lru_x[l].astype(BF16), b_lru_x=b_lru_x[l], lru_lambda=lru_lambda[l],
        w_out=w_out[l].astype(BF16), g_norm2=g_norm2[l],
        w_ffn_conv=w_ffn_conv[l], b_ffn_conv=b_ffn_conv[l])

    n_c = bsz + dec_b
    pad_c = (-n_c) % SUBLANES
    c_all = jnp.concatenate([c_prompt, c_sample, jnp.zeros((pad_c, d), F32)], axis=0)
    mod = _ada_call(c_all, w_ada[l], b_ada[l])
    mod_p = mod[:bsz].reshape(bsz, N_MOD, d)
    mod_s = mod[bsz:n_c].reshape(dec_b, N_MOD, d)

    front_p = _group_front(x_prompt, mod_p, 1, ROW_TILE, p)
    front_s = _group_front(x_sample, mod_s, dec_b, dec_b * dec_t, p)

    pool = cache_k.shape[1]
    attn_s, attn_p = _attn_fused_call(
        front_s["q"].astype(F32).reshape(dec_b, dec_t, width), front_s["kf"], front_s["vf"],
        cache_k[l].reshape(pool, page * n_heads, HEAD_DIM), cache_v[l].reshape(pool, page * n_heads, HEAD_DIM),
        page_table, front_p["q"], front_p["kb"], front_p["vb"], b_sb[l], batch=bsz, seq=seq)

    prev_pad = jnp.concatenate(
        [jnp.zeros((dec_b, SUBLANES - (LRU_CONV_W - 1), lru_w), F32), state_lru_conv[l]], axis=1)
    (ys, ks, vs, lcs, lhs, fcs), ffn_bf16 = _group_back(
        front_s, attn_s.reshape(dec_b * dec_t, width), prev_pad, state_lru_h[l].reshape(dec_b, 1, lru_w),
        state_ffn_conv[l], False, p, (w_up[l], w_up[l], w_down[l]), FFN_TF_SAMPLE)

    zeros_prev = jnp.zeros((bsz, SUBLANES, lru_w), F32)
    zeros_h = jnp.zeros((bsz, 1, lru_w), F32)
    zeros_ffn = jnp.zeros((bsz, FFN_CONV_W - 1, d_ff2), F32)
    (yp, kp, vp, lcp, lhp, fcp), _ = _group_back(
        front_p, attn_p, zeros_prev, zeros_h, zeros_ffn, True, p, ffn_bf16, FFN_TF)

    n_pages_p = seq // page
    return (yp, ys,
            kp.reshape(1, bsz, n_pages_p, page, n_heads, HEAD_DIM),
            vp.reshape(1, bsz, n_pages_p, page, n_heads, HEAD_DIM),
            ks.reshape(1, dec_b, dec_t, n_heads, HEAD_DIM),
            vs.reshape(1, dec_b, dec_t, n_heads, HEAD_DIM),
            lcp[None], lcs[None], lhp[None], lhs[None], fcp[None], fcs[None])
```

```python
import functools
import math

import jax
import jax.numpy as jnp
from jax import lax
from jax.experimental import pallas as pl
from jax.experimental.pallas import tpu as pltpu

F32 = jnp.float32
BF16 = jnp.bfloat16

EPS = 1e-6
LRU_C = 8.0
HEAD_DIM = 128
LRU_BLOCKS = 8
LRU_CONV_W = 4
FFN_CONV_W = 3
N_MOD = 6
SB_SCALE = 1.0 / math.sqrt(HEAD_DIM)

SUBLANES = 8
VMEM_LIMIT = 56 * 1024 * 1024

ROW_TILE = 512
INPROJ_SUB_BLOCKS = 2
MIXER_SUB_BLOCKS = 2
ATTN_TQ = 256
ATTN_TK = 256
ADA_TN = 1024
FFN_TF = 1024
FFN_TF_SAMPLE = 256
FFN_SUB_BLOCKS = 2
PAGES_PER_STEP = 8


def _cparams(n_axes):
    return pltpu.CompilerParams(dimension_semantics=("arbitrary",) * n_axes,
                                vmem_limit_bytes=VMEM_LIMIT)


def _softplus(z):
    return jnp.maximum(z, 0.0) + jnp.log1p(jnp.exp(-jnp.abs(z)))


def _softplus_logits(z):
    return jnp.maximum(z, 0.0) + jnp.log(1.0 + jnp.exp(-jnp.abs(z)))


def _gelu_tanh(x):
    c = math.sqrt(2.0 / math.pi)
    return x * (0.5 * (1.0 + jnp.tanh(c * (x + 0.044715 * (x * x * x)))))


def _suffix_sum_matrix(n_keys):
    j = lax.broadcasted_iota(jnp.int32, (n_keys, n_keys), 0)
    s = lax.broadcasted_iota(jnp.int32, (n_keys, n_keys), 1)
    return jnp.where(j >= s, 1.0, 0.0).astype(BF16)


def _ada_kernel(c_ref, w_ref, b_ref, o_ref):
    c = c_ref[...]
    s = (c * jax.nn.sigmoid(c)).astype(BF16)
    o_ref[...] = jnp.dot(s, w_ref[...].astype(BF16), preferred_element_type=F32) + b_ref[...]


def _ada_call(c_all, w_ada, b_ada):
    m, d = c_all.shape
    n = w_ada.shape[1]
    return pl.pallas_call(
        _ada_kernel,
        grid=(n // ADA_TN,),
        in_specs=[pl.BlockSpec((m, d), lambda j: (0, 0)),
                  pl.BlockSpec((d, ADA_TN), lambda j: (0, j)),
                  pl.BlockSpec((1, ADA_TN), lambda j: (0, j))],
        out_specs=pl.BlockSpec((m, ADA_TN), lambda j: (0, j)),
        out_shape=jax.ShapeDtypeStruct((m, n), F32),
        compiler_params=_cparams(1),
        name="adaln",
    )(c_all, w_ada, b_ada.reshape(1, n))


def _head_rmsnorm(a, g, n_heads):
    outs = []
    for h in range(n_heads):
        ah = a[:, h * HEAD_DIM:(h + 1) * HEAD_DIM]
        ms = jnp.mean(ah * ah, axis=-1, keepdims=True)
        outs.append(ah * lax.rsqrt(ms + EPS) * g)
    return outs


def _inproj_kernel(x_ref, sh_ref, sc_ref, g1_ref, wq_ref, wk_ref, wv_ref, gq_ref, gk_ref,
                   q_ref, kf_ref, kb_ref, vf_ref, vb_ref, hn_ref, *, n_heads, n_sub):
    tm, d = hn_ref.shape
    sub = tm // n_sub
    row_slices = [slice(s * sub, (s + 1) * sub) for s in range(n_sub)]
    head_cols = [slice(h * HEAD_DIM, (h + 1) * HEAD_DIM) for h in range(n_heads)]

    def normalize(rs):
        x = x_ref[...] if n_sub == 1 else x_ref[:, rs, :]
        ms = jnp.mean(x * x, axis=-1, keepdims=True)
        y = x * lax.rsqrt(ms + EPS) * g1_ref[...]
        hn = y * (1.0 + sc_ref[...]) + sh_ref[...]
        hn_ref[rs, :] = hn.reshape(sub, d).astype(BF16)

    def project(t, epilogue):
        w_ref = (wq_ref, wk_ref, wv_ref)[t]
        accs = [jnp.dot(hn_ref[rs, :], w_ref[...], preferred_element_type=F32) for rs in row_slices]
        for rs, acc in zip(row_slices, accs):
            epilogue(rs, acc)

    def head_rows(rs, h):
        return pl.ds(rs.start * n_heads + h, sub, stride=n_heads)

    def q_epilogue(rs, acc):
        for h, qh in enumerate(_head_rmsnorm(acc, gq_ref[...], n_heads)):
            q_ref[rs, head_cols[h]] = (qh * SB_SCALE).astype(BF16)

    def k_epilogue(rs, acc):
        for h, kh in enumerate(_head_rmsnorm(acc, gk_ref[...], n_heads)):
            kf_ref[head_rows(rs, h), :] = kh
            kb_ref[rs, head_cols[h]] = kh.astype(BF16)

    def v_epilogue(rs, acc):
        for h in range(n_heads):
            vf_ref[head_rows(rs, h), :] = acc[:, head_cols[h]]
        vb_ref[rs, :] = acc.astype(BF16)

    for rs in row_slices:
        normalize(rs)
    for t, epilogue in enumerate([q_epilogue, k_epilogue, v_epilogue]):
        project(t, epilogue)


def _inproj_call(x3, sh, sc, g1, w_in_bf, g_q, g_k, *, width, groups_per_block, mod_index):
    n_groups, rows, d = x3.shape
    g = groups_per_block
    tm = g * rows
    m = n_groups * rows
    n_blocks = n_groups // g
    n_heads = width // HEAD_DIM
    row_spec = pl.BlockSpec((tm, width), lambda i: (i, 0))
    head_spec = pl.BlockSpec((tm * n_heads, HEAD_DIM), lambda i: (i, 0))
    mod_spec = pl.BlockSpec((g, 1, d), lambda i: (mod_index(i), 0, 0))
    head_out = jax.ShapeDtypeStruct((m * n_heads, HEAD_DIM), F32)
    bf_out = jax.ShapeDtypeStruct((m, width), BF16)
    return pl.pallas_call(
        functools.partial(_inproj_kernel, n_heads=n_heads, n_sub=INPROJ_SUB_BLOCKS if g == 1 else 1),
        grid=(n_blocks,),
        in_specs=[pl.BlockSpec((g, rows, d), lambda i: (i, 0, 0)),
                  mod_spec, mod_spec,
                  pl.BlockSpec((1, d), lambda i: (0, 0)),
                  *[pl.BlockSpec((d, width), lambda i, t=t: (0, t), pipeline_mode=pl.Buffered(1))
                    for t in range(3)],
                  pl.BlockSpec((1, HEAD_DIM), lambda i: (0, 0)),
                  pl.BlockSpec((1, HEAD_DIM), lambda i: (0, 0))],
        out_specs=[row_spec, head_spec, row_spec, head_spec, row_spec,
                   pl.BlockSpec((tm, d), lambda i: (i, 0))],
        out_shape=[bf_out, head_out, bf_out, head_out, bf_out, jax.ShapeDtypeStruct((m, d), BF16)],
        compiler_params=_cparams(1),
        name="inproj",
    )(x3, sh, sc, g1.reshape(1, d), w_in_bf, w_in_bf, w_in_bf, g_q.reshape(1, HEAD_DIM),
      g_k.reshape(1, HEAD_DIM))


def _sb_mass(zs, suffix_mat, mask):
    cums = []
    for z in zs:
        sp = _softplus_logits(z)
        if mask is not None:
            sp = jnp.where(mask, sp, 0.0)
        cums.append(jnp.dot(sp.astype(BF16), suffix_mat, preferred_element_type=F32))
    return cums


def _sb_weights(z, cums, later, mask):
    w = jnp.exp(z - cums - later)
    if mask is not None:
        w = jnp.where(mask, w, 0.0)
    return w, later + cums[:, 0:1]


def _attn_fused_kernel(pt_ref, sb_ref, sqi_ref, sj_ref, bias_ref, qs_ref, knew_ref, vnew_ref, *rest,
                       n_heads, n_new, sample_steps, steps_per_seq):
    del pt_ref, sb_ref
    pages = PAGES_PER_STEP
    k_refs = rest[:pages]
    v_refs = rest[pages:2 * pages]
    qp_ref, kp_ref, vp_ref, os_ref, op_ref = rest[2 * pages:2 * pages + 5]
    kpad_ref, vpad_ref, acc_ref, later_ref, o_scr, later_scr = rest[2 * pages + 5:]
    s = pl.program_id(0)
    active = s < sample_steps
    g = s % steps_per_seq
    qi = sqi_ref[s]
    j = sj_ref[s]
    page_len = kpad_ref.shape[0] // n_heads
    page_suffix = _suffix_sum_matrix(page_len)
    tile_suffix = _suffix_sum_matrix(ATTN_TK)
    heads = range(n_heads)
    cols = [slice(h * HEAD_DIM, (h + 1) * HEAD_DIM) for h in heads]
    bias_rows = jnp.concatenate([jnp.full((n_new, 1), bias_ref[h], F32) for h in heads], axis=0)
    nt = (((1,), (1,)), ((), ()))

    def sample_logits(k_pages):
        qh = [qs_ref[0, :, cols[h]].astype(BF16) for h in heads]
        zs = []
        for kp in k_pages:
            z = [lax.dot_general(qh[h], kp[pl.ds(h, page_len, stride=n_heads), :].astype(BF16), nt,
                                 preferred_element_type=F32) for h in heads]
            zs.append(jnp.concatenate(z, axis=0) + bias_rows)
        return zs

    def sample_weights(zs, cums, mask):
        later = later_ref[...]
        ws = []
        for z, c in zip(zs, cums):
            w, later = _sb_weights(z, c, later, mask)
            ws.append(w)
        return ws, later

    def sample_values(ws, v_pages):
        acc = [acc_ref[h] for h in heads]
        for w, vp in zip(ws, v_pages):
            for h in heads:
                wh = w[h * n_new:(h + 1) * n_new].astype(BF16)
                vh = vp[pl.ds(h, page_len, stride=n_heads), :].astype(BF16)
                acc[h] = acc[h] + jnp.dot(wh, vh, preferred_element_type=F32)
        return acc

    @pl.when(jnp.logical_and(active, g == 0))
    def _():
        acc_ref[...] = jnp.zeros_like(acc_ref)
        later_ref[...] = jnp.zeros_like(later_ref)
        kpad_ref[...] = jnp.zeros_like(kpad_ref)
        vpad_ref[...] = jnp.zeros_like(vpad_ref)
        kpad_ref[0:n_new * n_heads, :] = knew_ref[...]
        vpad_ref[0:n_new * n_heads, :] = vnew_ref[...]
        row = lax.broadcasted_iota(jnp.int32, (n_heads * n_new, page_len), 0) % n_new
        col = lax.broadcasted_iota(jnp.int32, (n_heads * n_new, page_len), 1)
        mask = col < row
        zs = sample_logits([kpad_ref])
        ws, later = sample_weights(zs, _sb_mass(zs, page_suffix, mask), mask)
        later_ref[...] = later
        for h, a in enumerate(sample_values(ws, [vpad_ref])):
            acc_ref[h] = a

    def step(diagonal):
        k_pages = [r.at[0] for r in k_refs]
        v_pages = [r.at[0] for r in v_refs]
        start = pl.multiple_of(j * ATTN_TK, ATTN_TK)
        if diagonal:
            row = lax.broadcasted_iota(jnp.int32, (ATTN_TQ, ATTN_TK), 0)
            col = lax.broadcasted_iota(jnp.int32, (ATTN_TQ, ATTN_TK), 1)
            mask = col < row
        else:
            mask = None
        zs_p = [lax.dot_general(qp_ref[:, cols[h]], kp_ref[pl.ds(start, ATTN_TK), cols[h]], nt,
                                preferred_element_type=F32) + bias_ref[h] for h in heads]
        zs_s = sample_logits(k_pages)
        cums_p = _sb_mass(zs_p, tile_suffix, mask)
        cums_s = _sb_mass(zs_s, page_suffix, None)
        ws_s, later_s = sample_weights(zs_s, cums_s, None)
        later_ref[...] = jnp.where(active, later_s, later_ref[...])
        for h, a in enumerate(sample_values(ws_s, v_pages)):
            acc_ref[h] = jnp.where(active, a, acc_ref[h])
        ws_p = []
        for h in heads:
            later_p = jnp.zeros((ATTN_TQ, 1), F32) if diagonal else later_scr[h]
            w, later_p = _sb_weights(zs_p[h], cums_p[h], later_p, mask)
            later_scr[h] = later_p
            ws_p.append(w)
        for h in heads:
            pv = jnp.dot(ws_p[h].astype(BF16), vp_ref[pl.ds(start, ATTN_TK), cols[h]],
                         preferred_element_type=F32)
            o_scr[:, cols[h]] = pv if diagonal else o_scr[:, cols[h]] + pv

    pl.when(j == qi)(functools.partial(step, True))
    pl.when(j != qi)(functools.partial(step, False))

    @pl.when(jnp.logical_and(active, g == steps_per_seq - 1))
    def _():
        for h in heads:
            os_ref[0, :, cols[h]] = acc_ref[h].astype(os_ref.dtype)

    @pl.when(j == 0)
    def _():
        op_ref[...] = o_scr[...].astype(op_ref.dtype)


def _attn_fused_call(q_s3, knew, vnew, cache_k, cache_v, page_table, q_p, k_p, v_p, b_sb, *, batch, seq):
    bsz, n_new, width = q_s3.shape
    n_heads = width // HEAD_DIM
    n_pages = page_table.shape[1]
    page_rows = cache_k.shape[1]
    pages = PAGES_PER_STEP
    steps_per_seq = n_pages // pages
    sample_steps = bsz * steps_per_seq
    nq = seq // ATTN_TQ
    sched = [(b, qi, j) for b in range(batch) for qi in range(nq) for j in range(qi, -1, -1)]
    n_steps = max(len(sched), sample_steps)
    assert len(sched) == n_steps, "prompt schedule must cover every grid step"
    sb, sqi, sj = (jnp.asarray([t[n] for t in sched], jnp.int32) for n in range(3))

    def sample_pos(s):
        s = jnp.minimum(s, sample_steps - 1)
        return s // steps_per_seq, s % steps_per_seq

    def page_spec(p):
        def index(s, pt, sb, sqi, sj):
            b, g = sample_pos(s)
            return (pt[b, n_pages - 1 - (g * pages + p)], 0, 0)
        return pl.BlockSpec((1, page_rows, HEAD_DIM), index)

    new_spec = pl.BlockSpec((n_new * n_heads, HEAD_DIM), lambda s, pt, sb, sqi, sj: (sample_pos(s)[0], 0))
    qs_spec = pl.BlockSpec((1, n_new, width), lambda s, pt, sb, sqi, sj: (sample_pos(s)[0], 0, 0))
    qp_spec = pl.BlockSpec((ATTN_TQ, width), lambda s, pt, sb, sqi, sj: (sb[s] * nq + sqi[s], 0))
    kv_spec = pl.BlockSpec((seq, width), lambda s, pt, sb, sqi, sj: (sb[s], 0), pipeline_mode=pl.Buffered(1))
    grid_spec = pltpu.PrefetchScalarGridSpec(
        num_scalar_prefetch=4,
        grid=(n_steps,),
        in_specs=([pl.BlockSpec(memory_space=pltpu.SMEM), qs_spec, new_spec, new_spec]
                  + [page_spec(p) for p in range(pages)] * 2
                  + [qp_spec, kv_spec, kv_spec]),
        out_specs=[qs_spec, qp_spec],
        scratch_shapes=[pltpu.VMEM((page_rows, HEAD_DIM), F32),
                        pltpu.VMEM((page_rows, HEAD_DIM), F32),
                        pltpu.VMEM((n_heads, n_new, HEAD_DIM), F32),
                        pltpu.VMEM((n_heads * n_new, 1), F32),
                        pltpu.VMEM((ATTN_TQ, width), F32),
                        pltpu.VMEM((n_heads, ATTN_TQ, 1), F32)],
    )
    return pl.pallas_call(
        functools.partial(_attn_fused_kernel, n_heads=n_heads, n_new=n_new, sample_steps=sample_steps,
                          steps_per_seq=steps_per_seq),
        grid_spec=grid_spec,
        out_shape=[jax.ShapeDtypeStruct((bsz, n_new, width), F32),
                   jax.ShapeDtypeStruct(q_p.shape, BF16)],
        compiler_params=_cparams(1),
        name="attn_fused",
    )(page_table, sb, sqi, sj, b_sb, q_s3, knew, vnew, *([cache_k] * pages), *([cache_v] * pages),
      q_p, k_p, v_p)


def _segmented_linear_scan(a, b, seg_len):
    rows = lax.broadcasted_iota(jnp.int32, a.shape, 0) % seg_len
    d = 1
    while d < seg_len:
        keep = rows >= d
        a_prev = pltpu.roll(a, d, 0)
        b_prev = pltpu.roll(b, d, 0)
        b = jnp.where(keep, a * b_prev + b, b)
        a = jnp.where(keep, a * a_prev, a)
        d *= 2
    return a, b


def _mixer_kernel(hn_ref, attn_ref, x_ref, gt_ref, sh_ref, sc_ref, g2_ref,
                  wxl_ref, wgl_ref, cw_ref, cb_ref, wa_ref, ba_ref, wx_ref, bx_ref, lam_ref, prev_ref, h0_ref,
                  wo_ref, x1_ref, hn2_ref, hlast_ref, xtail_ref, xp_ref, hcar_ref,
                  *, groups, rows, blocks_per_seq, fresh, n_sub):
    i = pl.program_id(0)
    width = xp_ref.shape[-1]
    d = x_ref.shape[-1]
    first_block = (i % blocks_per_seq) == 0
    pad = SUBLANES
    sub_rows = rows // n_sub
    sub = groups * sub_rows
    row_slices = [slice(s * sub, (s + 1) * sub) for s in range(n_sub)]

    @pl.when(first_block)
    def _():
        if fresh:
            xp_ref[:, 0:pad, :] = jnp.zeros((groups, pad, width), F32)
            hcar_ref[...] = jnp.zeros_like(hcar_ref)
        else:
            xp_ref[:, 0:pad, :] = prev_ref[...]
            hcar_ref[...] = h0_ref[...]

    xls = [jnp.dot(hn_ref[rs, :], wxl_ref[...], preferred_element_type=F32) for rs in row_slices]
    gls = [jnp.dot(hn_ref[rs, :], wgl_ref[...], preferred_element_type=F32) for rs in row_slices]
    mixes = [jnp.dot(attn_ref[rs, :].astype(BF16), wo_ref[0:width, :], preferred_element_type=F32)
             for rs in row_slices]
    for s in range(n_sub):
        xp_ref[:, pad + s * sub_rows:pad + (s + 1) * sub_rows, :] = (
            xls[s].reshape(groups, sub_rows, width))

    cw = cw_ref[...]
    bw = width // LRU_BLOCKS
    neg_c_softplus = (-LRU_C) * _softplus(-lam_ref[...])
    h_prev = hcar_ref[...] if rows == SUBLANES else hcar_ref[0]
    for s, rs in enumerate(row_slices):
        t0 = s * sub_rows
        xc = cb_ref[...] + cw[0:1] * xp_ref[:, pl.ds(pad - 3 + t0, sub_rows), :]
        for t in range(1, LRU_CONV_W):
            xc = xc + cw[t:t + 1] * xp_ref[:, pl.ds(pad - 3 + t0 + t, sub_rows), :]
        xc = xc.reshape(sub, width)

        xcb = xc.astype(BF16)
        r_parts, i_parts = [], []
        for n in range(LRU_BLOCKS):
            xb = xcb[:, n * bw:(n + 1) * bw]
            r_parts.append(jnp.dot(xb, wa_ref[n], preferred_element_type=F32))
            i_parts.append(jnp.dot(xb, wx_ref[n], preferred_element_type=F32))
        r = jax.nn.sigmoid(jnp.concatenate(r_parts, axis=1) + ba_ref[...])
        ig = jax.nn.sigmoid(jnp.concatenate(i_parts, axis=1) + bx_ref[...])
        log_a = r * neg_c_softplus
        a = jnp.exp(log_a)
        one_minus_a2 = -jnp.tanh(log_a) * (a * a + 1.0)
        mult = jnp.where(one_minus_a2 > 0.0, one_minus_a2 * lax.rsqrt(one_minus_a2), 0.0)
        if fresh and s == 0:
            t_row = lax.broadcasted_iota(jnp.int32, (sub, 1), 0)
            start_row = jnp.where(first_block, 0, -1)
            mult = jnp.where(t_row == start_row, 1.0, mult)
        b = mult * ig * xc

        a_cum, h = _segmented_linear_scan(a, b, SUBLANES)
        if rows == SUBLANES:
            h = h + a_cum * jnp.broadcast_to(h_prev, (groups, rows, width)).reshape(sub, width)
            h_prev = h.reshape(groups, rows, width)[:, rows - 1:rows, :]
        else:
            tiles = []
            for t in range(sub_rows // SUBLANES):
                ts = slice(t * SUBLANES, (t + 1) * SUBLANES)
                h_t = h[ts] + a_cum[ts] * h_prev
                tiles.append(h_t)
                h_prev = h_t[SUBLANES - 1:SUBLANES]
            h = jnp.concatenate(tiles, axis=0)
        lru = (h * _gelu_tanh(gls[s])).astype(BF16)

        mix = mixes[s] + jnp.dot(lru, wo_ref[width:2 * width, :], preferred_element_type=F32)
        x1 = x_ref[:, t0:t0 + sub_rows, :] + gt_ref[...] * mix.reshape(groups, sub_rows, d)
        x1_ref[:, t0:t0 + sub_rows, :] = x1
        ms = jnp.mean(x1 * x1, axis=-1, keepdims=True)
        y = x1 * lax.rsqrt(ms + EPS) * g2_ref[...]
        hn2 = y * (1.0 + sc_ref[...]) + sh_ref[...]
        hn2_ref[rs, :] = hn2.reshape(sub, d).astype(BF16)

    h_last = h_prev if rows == SUBLANES else h_prev[None]
    hcar_ref[...] = h_last
    hlast_ref[...] = h_last
    tail = xp_ref[:, rows:rows + pad, :]
    xtail_ref[...] = tail
    xp_ref[:, 0:pad, :] = tail


def _mixer_call(hn, attn, x3, gt, sh, sc, g2, w_in_bf, cw, cb, wa_bf, ba, wx_bf, bx, lam, prev_pad, h0,
                w_out_bf, *, n_seq, groups_per_block, mod_index, fresh):
    n_groups, rows, d = x3.shape
    g = groups_per_block
    tm = g * rows
    m = n_groups * rows
    n_blocks = n_groups // g
    blocks_per_seq = n_blocks // (n_seq // g)
    width = attn.shape[1]
    bw = width // LRU_BLOCKS
    resident = pl.Buffered(1)
    x_spec = pl.BlockSpec((g, rows, d), lambda i: (i, 0, 0))
    mod_spec = pl.BlockSpec((g, 1, d), lambda i: (mod_index(i), 0, 0))
    vec_spec = pl.BlockSpec((1, width), lambda i: (0, 0))
    gate_spec = pl.BlockSpec((LRU_BLOCKS, bw, bw), lambda i: (0, 0, 0))
    seq_index = lambda i: (i // blocks_per_seq, 0, 0)
    return pl.pallas_call(
        functools.partial(_mixer_kernel, groups=g, rows=rows, blocks_per_seq=blocks_per_seq, fresh=fresh,
                          n_sub=MIXER_SUB_BLOCKS if g == 1 else 1),
        grid=(n_blocks,),
        in_specs=[pl.BlockSpec((tm, d), lambda i: (i, 0)),
                  pl.BlockSpec((tm, width), lambda i: (i, 0)),
                  x_spec, mod_spec, mod_spec, mod_spec,
                  pl.BlockSpec((1, d), lambda i: (0, 0)),
                  pl.BlockSpec((d, width), lambda i: (0, 3), pipeline_mode=resident),
                  pl.BlockSpec((d, width), lambda i: (0, 4), pipeline_mode=resident),
                  pl.BlockSpec((LRU_CONV_W, width), lambda i: (0, 0)), vec_spec,
                  gate_spec, vec_spec, gate_spec, vec_spec, vec_spec,
                  pl.BlockSpec((g, SUBLANES, width), seq_index),
                  pl.BlockSpec((g, 1, width), seq_index),
                  pl.BlockSpec((d, d), lambda i: (0, 0), pipeline_mode=resident)],
        out_specs=[x_spec,
                   pl.BlockSpec((tm, d), lambda i: (i, 0)),
                   pl.BlockSpec((g, 1, width), seq_index),
                   pl.BlockSpec((g, SUBLANES, width), lambda i: (i, 0, 0))],
        out_shape=[jax.ShapeDtypeStruct((n_groups, rows, d), F32),
                   jax.ShapeDtypeStruct((m, d), BF16),
                   jax.ShapeDtypeStruct((n_seq, 1, width), F32),
                   jax.ShapeDtypeStruct((n_groups, SUBLANES, width), F32)],
        scratch_shapes=[pltpu.VMEM((g, rows + SUBLANES, width), F32),
                        pltpu.VMEM((g, 1, width), F32)],
        compiler_params=_cparams(1),
        name="mixer",
    )(hn, attn, x3, gt, sh, sc, g2.reshape(1, d), w_in_bf, w_in_bf, cw, cb.reshape(1, width), wa_bf,
      ba.reshape(1, width), wx_bf, bx.reshape(1, width), lam.reshape(1, width), prev_pad, h0, w_out_bf)


def _ffn_kernel(hn_ref, wg_ref, wv_ref, cwg_ref, cwv_ref, cbg_ref, cbv_ref, wd_ref,
                x1_ref, gt_ref, pg_ref, pv_ref, *rest,
                groups, rows, blocks_per_seq, fresh, emit_bf16_weights):
    if emit_bf16_weights:
        y_ref, tg_ref, tv_ref, wg_out_ref, wv_out_ref, wd_out_ref, cg_ref, cv_ref = rest
        for src, dst in ((wg_ref, wg_out_ref), (wv_ref, wv_out_ref), (wd_ref, wd_out_ref)):
            dst[...] = src[...].astype(BF16)
        wg_ref, wv_ref, wd_ref = wg_out_ref, wv_out_ref, wd_out_ref
    else:
        y_ref, tg_ref, tv_ref, cg_ref, cv_ref = rest
    i = pl.program_id(0)
    j = pl.program_id(1)
    tm = groups * rows
    tf = wg_ref.shape[1]
    first_block = (i % blocks_per_seq) == 0
    pad = SUBLANES
    if not fresh and blocks_per_seq != 1:
        raise NotImplementedError("continuing sequences must fit one row block")

    @pl.when(j == 0)
    def _():
        y_ref[...] = jnp.zeros_like(y_ref)

    @pl.when(jnp.logical_and(i == 0, j == 0))
    def _():
        cg_ref[...] = jnp.zeros_like(cg_ref)
        cv_ref[...] = jnp.zeros_like(cv_ref)

    n_sub = FFN_SUB_BLOCKS if groups == 1 else 1
    sub = tm // n_sub
    row_slices = [slice(s * sub, (s + 1) * sub) for s in range(n_sub)]
    ups = []
    for rs in row_slices:
        hn = hn_ref[rs, :]
        ups.append((jnp.dot(hn, wg_ref[...], preferred_element_type=F32),
                    jnp.dot(hn, wv_ref[...], preferred_element_type=F32)))

    def boundary_rows(p_ref, carry_ref):
        if fresh:
            prev = jnp.where(first_block, 0.0, carry_ref[j])
            return prev[:, pad - 2:pad - 1, :], prev[:, pad - 1:pad, :]
        return p_ref[:, 0:1, :], p_ref[:, 1:2, :]

    def conv(u, p0, p1, cw_ref, cb_ref):
        u1 = pltpu.roll(u, 1, 0)
        u2 = pltpu.roll(u, 2, 0)
        if groups == 1:
            top = lax.broadcasted_iota(jnp.int32, (pad, tf), 0)
            u1_top = jnp.where(top == 0, p1, u1[0:pad])
            u2_top = jnp.where(top == 0, p0, jnp.where(top == 1, p1, u2[0:pad]))
            u1 = jnp.concatenate([u1_top, u1[pad:]], axis=0)
            u2 = jnp.concatenate([u2_top, u2[pad:]], axis=0)
        else:
            t_row = lax.broadcasted_iota(jnp.int32, (tm, tf), 0) % rows
            p1r = jnp.broadcast_to(p1, (groups, rows, tf)).reshape(tm, tf)
            p0r = jnp.broadcast_to(p0, (groups, rows, tf)).reshape(tm, tf)
            u1 = jnp.where(t_row == 0, p1r, u1)
            u2 = jnp.where(t_row == 0, p0r, jnp.where(t_row == 1, p1r, u2))
        cw = cw_ref[...]
        uc = cb_ref[...] + cw[0:1] * u2
        uc = uc + cw[1:2] * u1
        return uc + cw[2:3] * u

    pg = boundary_rows(pg_ref, cg_ref)
    pv = boundary_rows(pv_ref, cv_ref)
    if groups == 1:
        pg = (pg[0][0], pg[1][0])
        pv = (pv[0][0], pv[1][0])
    for s, rs in enumerate(row_slices):
        ug, uv = ups[s]
        gated = (_gelu_tanh(conv(ug, pg[0], pg[1], cwg_ref, cbg_ref))
                 * conv(uv, pv[0], pv[1], cwv_ref, cbv_ref)).astype(BF16)
        y_ref[rs, :] += jnp.dot(gated, wd_ref[...], preferred_element_type=F32)
        pg = (ug[sub - 2:sub - 1], ug[sub - 1:sub])
        pv = (uv[sub - 2:sub - 1], uv[sub - 1:sub])

    ug3 = ups[-1][0].reshape(groups, rows // n_sub, tf)
    uv3 = ups[-1][1].reshape(groups, rows // n_sub, tf)
    last = rows // n_sub
    cg_ref[j] = ug3[:, last - pad:last, :]
    cv_ref[j] = uv3[:, last - pad:last, :]
    tg_ref[...] = ug3[:, last - 2:last, :]
    tv_ref[...] = uv3[:, last - 2:last, :]

    @pl.when(j == pl.num_programs(1) - 1)
    def _():
        down = y_ref[...].reshape(groups, rows, y_ref.shape[-1])
        y_ref[...] = x1_ref[...] + (gt_ref[...] * down).reshape(y_ref.shape)


def _ffn_call(hn2, w_gate, w_value, w_down, cw, cb, x1, gt, prev_g, prev_v, *,
              n_seq, groups_per_block, mod_index, fresh, tf):
    n_groups, rows, d = x1.shape
    g = groups_per_block
    tm = g * rows
    n_blocks = n_groups // g
    blocks_per_seq = n_blocks // (n_seq // g)
    d_ff = w_down.shape[0]
    nj = d_ff // tf
    emit = w_down.dtype != BF16
    assert not emit or n_blocks == 1, "f32 weight tiles must be visited exactly once"
    value_offset = nj if w_value.shape[1] == 2 * d_ff else 0
    seq_index = lambda i: i // blocks_per_seq
    x_spec = pl.BlockSpec((tm, d), lambda i, j: (i, 0))
    up_spec = pl.BlockSpec((d, tf), lambda i, j: (0, j))
    down_spec = pl.BlockSpec((tf, d), lambda i, j: (j, 0))
    prev_spec = pl.BlockSpec((g, FFN_CONV_W - 1, tf), lambda i, j: (seq_index(i), 0, j))
    tail_spec = pl.BlockSpec((g, FFN_CONV_W - 1, tf), lambda i, j: (i, 0, j))
    tail_out = jax.ShapeDtypeStruct((n_groups, FFN_CONV_W - 1, d_ff), F32)
    out_specs = [x_spec, tail_spec, tail_spec]
    out_shape = [jax.ShapeDtypeStruct((n_groups * rows, d), F32), tail_out, tail_out]
    if emit:
        out_specs += [up_spec, up_spec, down_spec]
        out_shape += [jax.ShapeDtypeStruct((d, d_ff), BF16), jax.ShapeDtypeStruct((d, d_ff), BF16),
                      jax.ShapeDtypeStruct((d_ff, d), BF16)]
    outs = pl.pallas_call(
        functools.partial(_ffn_kernel, groups=g, rows=rows, blocks_per_seq=blocks_per_seq, fresh=fresh,
                          emit_bf16_weights=emit),
        grid=(n_blocks, nj),
        in_specs=[pl.BlockSpec((tm, d), lambda i, j: (i, 0)),
                  up_spec,
                  pl.BlockSpec((d, tf), lambda i, j: (0, j + value_offset)),
                  pl.BlockSpec((FFN_CONV_W, tf), lambda i, j: (0, j)),
                  pl.BlockSpec((FFN_CONV_W, tf), lambda i, j: (0, j + nj)),
                  pl.BlockSpec((1, tf), lambda i, j: (0, j)),
                  pl.BlockSpec((1, tf), lambda i, j: (0, j + nj)),
                  down_spec,
                  x_spec,
                  pl.BlockSpec((g, 1, d), lambda i, j: (mod_index(i), 0, 0)),
                  prev_spec, prev_spec],
        out_specs=out_specs,
        out_shape=out_shape,
        scratch_shapes=[pltpu.VMEM((nj, g, SUBLANES, tf), F32),
                        pltpu.VMEM((nj, g, SUBLANES, tf), F32)],
        compiler_params=_cparams(2),
        name="convffn",
    )(hn2, w_gate, w_value, cw, cw, cb.reshape(1, 2 * d_ff), cb.reshape(1, 2 * d_ff), w_down,
      x1.reshape(n_groups * rows, d), gt, prev_g, prev_v)
    y, tail_g, tail_v = outs[:3]
    last = lambda t: t.reshape(n_seq, n_groups // n_seq, FFN_CONV_W - 1, d_ff)[:, -1]
    bf16_weights = tuple(outs[3:]) if emit else (w_gate, w_value, w_down)
    return y, last(tail_g), last(tail_v), bf16_weights


def _group_front(x, mod, groups_per_block, rows_per_block, p):
    n_seq, t, d = x.shape
    rows = rows_per_block // groups_per_block
    x3 = x.reshape(n_seq * t // rows, rows, d)
    blocks_per_seq = max(1, t // rows_per_block)
    if groups_per_block == 1:
        mod_index = lambda i: i // blocks_per_seq
    else:
        mod_index = lambda i: i
    mods = [mod[:, n:n + 1, :] for n in range(N_MOD)]
    q, kf, kb, vf, vb, hn = _inproj_call(
        x3, mods[0], mods[1], p["g_norm1"], p["w_in"], p["g_q"], p["g_k"],
        width=p["width"], groups_per_block=groups_per_block, mod_index=mod_index)
    return dict(x3=x3, mods=mods, mod_index=mod_index, groups_per_block=groups_per_block,
                shape=(n_seq, t, d), q=q, kf=kf, kb=kb, vf=vf, vb=vb, hn=hn)


def _group_back(ctx, attn, lru_prev_pad, lru_h0, ffn_prev, fresh, p, ffn_weights, ffn_tf):
    n_seq, t, d = ctx["shape"]
    g, mod_index = ctx["groups_per_block"], ctx["mod_index"]
    _, _, gt1, sh2, sc2, gt2 = ctx["mods"]
    x1, hn2, h_last, xl_tail = _mixer_call(
        ctx["hn"], attn, ctx["x3"], gt1, sh2, sc2, p["g_norm2"], p["w_in"], p["w_lru_conv"],
        p["b_lru_conv"], p["w_lru_a"], p["b_lru_a"], p["w_lru_x"], p["b_lru_x"], p["lru_lambda"],
        lru_prev_pad, lru_h0, p["w_out"], n_seq=n_seq, groups_per_block=g, mod_index=mod_index, fresh=fresh)
    w_gate, w_value, w_down = ffn_weights
    d_ff = w_down.shape[0]
    y, tail_g, tail_v, ffn_bf16 = _ffn_call(
        hn2, w_gate, w_value, w_down, p["w_ffn_conv"], p["b_ffn_conv"], x1, gt2,
        ffn_prev[:, :, :d_ff], ffn_prev[:, :, d_ff:],
        n_seq=n_seq, groups_per_block=g, mod_index=mod_index, fresh=fresh, tf=ffn_tf)
    width = xl_tail.shape[-1]
    lru_conv_new = xl_tail.reshape(n_seq, -1, SUBLANES, width)[:, -1, SUBLANES - (LRU_CONV_W - 1):, :]
    ffn_conv_new = jnp.concatenate([tail_g, tail_v], axis=-1)
    outs = (y.reshape(n_seq, t, d), ctx["kf"], ctx["vf"], lru_conv_new, h_last.reshape(n_seq, width),
            ffn_conv_new)
    return outs, ffn_bf16


def kernel(x_prompt, x_sample, c_prompt, c_sample, cache_k, cache_v, page_table, state_lru_conv, state_lru_h, state_ffn_conv, w_ada, b_ada, g_norm1, w_in, g_q, g_k, b_sb, w_lru_conv, b_lru_conv, w_lru_a, b_lru_a, w_lru_x, b_lru_x, lru_lambda, w_out, g_norm2, w_up, w_ffn_conv, b_ffn_conv, w_down):
    depth = w_ada.shape[0]
    assert depth == 1, "single-layer step"
    l = 0
    bsz, seq, d = x_prompt.shape
    dec_b, dec_t, _ = x_sample.shape
    page, n_heads = cache_k.shape[2], cache_k.shape[3]
    width = n_heads * HEAD_DIM
    lru_w = state_lru_h.shape[-1]
    d_ff2 = state_ffn_conv.shape[-1]

    p = dict(
        width=width, g_norm1=g_norm1[l], w_in=w_in[l].astype(BF16), g_q=g_q[l], g_k=g_k[l],
        w_lru_conv=w_lru_conv[l], b_lru_conv=b_lru_conv[l],
        w_lru_a=w_lru_a[l].astype(BF16), b_lru_a=b_lru_a[l],
        w_lru_x=w_lru_x[l].astype(BF16), b_lru_x=b_lru_x[l], lru_lambda=lru_lambda[l],
        w_out=w_out[l].astype(BF16), g_norm2=g_norm2[l],
        w_ffn_conv=w_ffn_conv[l], b_ffn_conv=b_ffn_conv[l])

    n_c = bsz + dec_b
    pad_c = (-n_c) % SUBLANES
    c_all = jnp.concatenate([c_prompt, c_sample, jnp.zeros((pad_c, d), F32)], axis=0)
    mod = _ada_call(c_all, w_ada[l], b_ada[l])
    mod_p = mod[:bsz].reshape(bsz, N_MOD, d)
    mod_s = mod[bsz:n_c].reshape(dec_b, N_MOD, d)

    front_p = _group_front(x_prompt, mod_p, 1, ROW_TILE, p)
    front_s = _group_front(x_sample, mod_s, dec_b, dec_b * dec_t, p)

    pool = cache_k.shape[1]
    attn_s, attn_p = _attn_fused_call(
        front_s["q"].astype(F32).reshape(dec_b, dec_t, width), front_s["kf"], front_s["vf"],
        cache_k[l].reshape(pool, page * n_heads, HEAD_DIM), cache_v[l].reshape(pool, page * n_heads, HEAD_DIM),
        page_table, front_p["q"], front_p["kb"], front_p["vb"], b_sb[l], batch=bsz, seq=seq)

    prev_pad = jnp.concatenate(
        [jnp.zeros((dec_b, SUBLANES - (LRU_CONV_W - 1), lru_w), F32), state_lru_conv[l]], axis=1)
    (ys, ks, vs, lcs, lhs, fcs), ffn_bf16 = _group_back(
        front_s, attn_s.reshape(dec_b * dec_t, width), prev_pad, state_lru_h[l].reshape(dec_b, 1, lru_w),
        state_ffn_conv[l], False, p, (w_up[l], w_up[l], w_down[l]), FFN_TF_SAMPLE)

    zeros_prev = jnp.zeros((bsz, SUBLANES, lru_w), F32)
    zeros_h = jnp.zeros((bsz, 1, lru_w), F32)
    zeros_ffn = jnp.zeros((bsz, FFN_CONV_W - 1, d_ff2), F32)
    (yp, kp, vp, lcp, lhp, fcp), _ = _group_back(
        front_p, attn_p, zeros_prev, zeros_h, zeros_ffn, True, p, ffn_bf16, FFN_TF)

    n_pages_p = seq // page
    return (yp, ys,
            kp.reshape(1, bsz, n_pages_p, page, n_heads, HEAD_DIM),
            vp.reshape(1, bsz, n_pages_p, page, n_heads, HEAD_DIM),
            ks.reshape(1, dec_b, dec_t, n_heads, HEAD_DIM),
            vs.reshape(1, dec_b, dec_t, n_heads, HEAD_DIM),
            lcp[None], lcs[None], lhp[None], lhs[None], fcp[None], fcs[None])
```
